```python
import math, functools
import jax, jax.numpy as jnp
from jax import lax
import numpy as np

D_MODEL = 1024
BATCH = 8
SEQ = 2048
DEPTH = 2
DEC_BATCH = 32
DEC_SEQ = 1
PAST_LEN = 8192
PAGE_SIZE = 128

D_SSM = D_MODEL // 2
SSM_GROUP = 16
N_GROUPS = D_SSM // SSM_GROUP
SSM_N = 64
N_HEADS = 8
HEAD_DIM = 64
D_ATT = N_HEADS * HEAD_DIM
IDX_HEADS = 8
IDX_DIM = 64
INDEX_TOPK = 256
Q_BLOCK = 32
IDX_SCALE = IDX_DIM ** -0.5 * IDX_HEADS ** -0.5
ATT_SCALE = HEAD_DIM ** -0.5
D_FF = 2816
EPS = 1e-6
IN_WIDTHS = (D_SSM, D_ATT, D_ATT, D_ATT, IDX_HEADS * IDX_DIM, IDX_DIM, IDX_HEADS, D_MODEL, D_MODEL)
IN_WIDTH = sum(IN_WIDTHS)

kernel_name = "hybrid_s5_dsa_macaron_decode_step"


def rmsnorm(x, g):
    xf = x.astype(jnp.float32)
    xf = xf * lax.rsqrt(jnp.mean(xf * xf, axis=-1, keepdims=True) + EPS)
    return (xf * g.astype(jnp.float32)).astype(x.dtype)


def modulate(h, shift, scale):
    return h * (1.0 + scale) + shift


def swiglu(h, w_in, w_out):
    g, u = jnp.split(h @ w_in, 2, axis=-1)
    return (jax.nn.silu(g) * u) @ w_out


def cmul(ar, ai, br, bi):
    return ar * br - ai * bi, ar * bi + ai * br


def _scan_combine(e1, e2):
    a1r, a1i, b1r, b1i = e1
    a2r, a2i, b2r, b2i = e2
    ar, ai = cmul(a2r, a2i, a1r, a1i)
    br, bi = cmul(a2r, a2i, b1r, b1i)
    return ar, ai, br + b2r, bi + b2i


def s5_mixer(u, h0_re, h0_im, log_dt, a_re, a_im, b_re, b_im, c_re, c_im, d_skip):
    f32 = jnp.float32
    bsz, t_len, _ = u.shape
    uf = u.astype(f32).reshape(bsz, t_len, N_GROUPS, SSM_GROUP)
    dt = jnp.exp(log_dt.astype(f32))[:, None]
    ar, ai = a_re.astype(f32), a_im.astype(f32)
    mag = jnp.exp(dt * ar)
    abar_re, abar_im = mag * jnp.cos(dt * ai), mag * jnp.sin(dt * ai)
    den = ar * ar + ai * ai
    z_re = ((abar_re - 1.0) * ar + abar_im * ai) / den
    z_im = (abar_im * ar - (abar_re - 1.0) * ai) / den
    bbar_re, bbar_im = cmul(z_re[..., None], z_im[..., None], b_re.astype(f32), b_im.astype(f32))
    bu_re = jnp.einsum('gnj,btgj->btgn', bbar_re, uf)
    bu_im = jnp.einsum('gnj,btgj->btgn', bbar_im, uf)
    carry_re, carry_im = cmul(abar_re, abar_im, h0_re.astype(f32), h0_im.astype(f32))
    bu_re = bu_re.at[:, 0].add(carry_re)
    bu_im = bu_im.at[:, 0].add(carry_im)
    a_seq_re = jnp.broadcast_to(abar_re, bu_re.shape)
    a_seq_im = jnp.broadcast_to(abar_im, bu_im.shape)
    _, _, s_re, s_im = lax.associative_scan(_scan_combine, (a_seq_re, a_seq_im, bu_re, bu_im), axis=1)
    y = (jnp.einsum('gjn,btgn->btgj', c_re.astype(f32), s_re)
         - jnp.einsum('gjn,btgn->btgj', c_im.astype(f32), s_im))
    y = y.reshape(bsz, t_len, D_SSM) + d_skip.astype(f32) * u.astype(f32)
    return y.astype(u.dtype), s_re[:, -1].astype(h0_re.dtype), s_im[:, -1].astype(h0_im.dtype)


def index_scores(q_idx, w_idx, k_idx):
    s = jnp.einsum('bqhd,bsd->bqhs', q_idx.astype(jnp.float32), k_idx.astype(jnp.float32))
    return jnp.einsum('bqhs,bqh->bqs', jax.nn.relu(s), w_idx.astype(jnp.float32) * IDX_SCALE)


_gather_rows = jax.vmap(lambda rows, ids: rows[ids])


def sparse_attend(q, k_sel, v_sel, valid):
    logits = jnp.einsum('bqhd,bqkhd->bqhk', q.astype(jnp.float32), k_sel.astype(jnp.float32)) * ATT_SCALE
    logits = jnp.where(valid[:, :, None, :], logits, -jnp.inf)
    p = jax.nn.softmax(logits, axis=-1)
    return jnp.einsum('bqhk,bqkhd->bqhd', p, v_sel.astype(jnp.float32)).astype(q.dtype)


def prompt_attention(q, k, v, q_idx, k_idx, w_idx):
    bsz, t_len = q.shape[:2]
    topk = min(INDEX_TOPK, t_len // 4)
    n_blocks = t_len // Q_BLOCK
    key_pos = jnp.arange(t_len)

    def block(args):
        qb, qib, wib, start = args
        scores = index_scores(qib, wib, k_idx)
        qpos = start + jnp.arange(Q_BLOCK)
        scores = jnp.where((key_pos[None, :] <= qpos[:, None])[None], scores, -jnp.inf)
        vals, idx = lax.top_k(scores, topk)
        return sparse_attend(qb, _gather_rows(k, idx), _gather_rows(v, idx), jnp.isfinite(vals))

    to_blocks = lambda a: jnp.moveaxis(a.reshape(bsz, n_blocks, Q_BLOCK, *a.shape[2:]), 1, 0)
    out = lax.map(block, (to_blocks(q), to_blocks(q_idx), to_blocks(w_idx), jnp.arange(n_blocks) * Q_BLOCK))
    return jnp.moveaxis(out, 0, 1).reshape(bsz, t_len, N_HEADS, HEAD_DIM)


def sample_attention(q, k_new, v_new, q_idx, k_idx_new, w_idx, layer, cache_k, cache_v, cache_idx_k, page_table):
    bsz, t_new = q.shape[:2]
    n_pages = PAST_LEN // PAGE_SIZE
    n_keys = PAST_LEN + t_new
    topk = min(INDEX_TOPK, n_keys // 4)
    k_idx_past = cache_idx_k[layer, page_table].reshape(bsz, n_pages * PAGE_SIZE, IDX_DIM)
    k_idx_all = jnp.concatenate([k_idx_past, k_idx_new.astype(k_idx_past.dtype)], axis=1)
    scores = index_scores(q_idx, w_idx, k_idx_all)
    qpos = PAST_LEN + jnp.arange(t_new)
    scores = jnp.where((jnp.arange(n_keys)[None, :] <= qpos[:, None])[None], scores, -jnp.inf)
    vals, idx = lax.top_k(scores, topk)
    in_past = (idx < PAST_LEN)[..., None, None]
    pidx = jnp.minimum(idx, PAST_LEN - 1)
    phys = _gather_rows(page_table, pidx // PAGE_SIZE)
    slot = pidx % PAGE_SIZE
    nidx = jnp.clip(idx - PAST_LEN, 0, t_new - 1)
    k_sel = jnp.where(in_past, cache_k[layer, phys, slot], _gather_rows(k_new, nidx).astype(cache_k.dtype))
    v_sel = jnp.where(in_past, cache_v[layer, phys, slot], _gather_rows(v_new, nidx).astype(cache_v.dtype))
    return sparse_attend(q, k_sel, v_sel, jnp.isfinite(vals))


def decoder_layer(x, c, params, h0_re, h0_im, attention_fn):
    (mod_w, mod_b, norm_pre, norm_post, ffn1_in, ffn1_out, w_in, ssm_log_dt, ssm_a_re, ssm_a_im,
     ssm_b_re, ssm_b_im, ssm_c_re, ssm_c_im, ssm_d, glu_w, glu_v, w_branch_attn, w_out,
     ffn2_in, ffn2_out) = params
    bsz, t_len, _ = x.shape
    mod = jax.nn.silu(c) @ mod_w + mod_b
    sh0, sc0, gt0, sh1, sc1, gt1, sh2, sc2, gt2 = jnp.split(mod[:, None, :], 9, axis=-1)
    h = modulate(rmsnorm(x, norm_pre[0]), sh0, sc0)
    x = x + 0.5 * gt0 * rmsnorm(swiglu(h, ffn1_in, ffn1_out), norm_post[0])
    h = modulate(rmsnorm(x, norm_pre[1]), sh1, sc1)
    splits = [int(s) for s in np.cumsum(IN_WIDTHS)[:-1]]
    u, q, k, v, q_idx, k_idx, w_idx, gate_a, gate_b = jnp.split(h @ w_in, splits, axis=-1)
    y_ssm, s_re, s_im = s5_mixer(u, h0_re, h0_im, ssm_log_dt, ssm_a_re, ssm_a_im,
                                 ssm_b_re, ssm_b_im, ssm_c_re, ssm_c_im, ssm_d)
    y_a = (y_ssm @ glu_w) * jax.nn.sigmoid(y_ssm @ glu_v)
    k = k.reshape(bsz, t_len, N_HEADS, HEAD_DIM)
    v = v.reshape(bsz, t_len, N_HEADS, HEAD_DIM)
    attn = attention_fn(q.reshape(bsz, t_len, N_HEADS, HEAD_DIM), k, v,
                        q_idx.reshape(bsz, t_len, IDX_HEADS, IDX_DIM), k_idx, w_idx)
    y_b = attn.reshape(bsz, t_len, D_ATT) @ w_branch_attn
    mixed = (jax.nn.sigmoid(gate_a) * y_a + jax.nn.sigmoid(gate_b) * y_b) @ w_out
    x = x + gt1 * rmsnorm(mixed, norm_post[1])
    h = modulate(rmsnorm(x, norm_pre[2]), sh2, sc2)
    x = x + 0.5 * gt2 * rmsnorm(swiglu(h, ffn2_in, ffn2_out), norm_post[2])
    return x, (k, v, k_idx, s_re, s_im)


def setup_inputs(seed: int = 0) -> dict:
    key = jax.random.key(seed)
    ks = iter(jax.random.split(key, 48))
    f32 = jnp.float32
    nrm = lambda shape, scale: scale * jax.random.normal(next(ks), shape, f32)
    n_pages = PAST_LEN // PAGE_SIZE
    n_used = DEC_BATCH * n_pages
    n_phys = n_used + (n_used + 3) // 4
    perm = jax.random.permutation(next(ks), n_phys)
    page_table = perm[:n_used].reshape(DEC_BATCH, n_pages).astype(jnp.int32)
    log_dt = math.log(0.001) + jax.random.uniform(next(ks), (DEPTH, N_GROUPS), f32) * (math.log(0.1) - math.log(0.001))
    return {
        'x_prompt': nrm((BATCH, SEQ, D_MODEL), 1.0),
        'x_sample': nrm((DEC_BATCH, DEC_SEQ, D_MODEL), 1.0),
        'cache_k': nrm((DEPTH, n_phys, PAGE_SIZE, N_HEADS, HEAD_DIM), 1.0),
        'cache_v': nrm((DEPTH, n_phys, PAGE_SIZE, N_HEADS, HEAD_DIM), 1.0),
        'cache_idx_k': nrm((DEPTH, n_phys, PAGE_SIZE, IDX_DIM), 1.0),
        'state_ssm_re': nrm((DEPTH, DEC_BATCH, N_GROUPS, SSM_N), 0.3),
        'state_ssm_im': nrm((DEPTH, DEC_BATCH, N_GROUPS, SSM_N), 0.3),
        'page_table': page_table,
        'c_prompt': nrm((BATCH, D_MODEL), 1.0),
        'c_sample': nrm((DEC_BATCH, D_MODEL), 1.0),
        'mod_w': nrm((DEPTH, D_MODEL, 9 * D_MODEL), 0.3 * D_MODEL ** -0.5),
        'mod_b': nrm((DEPTH, 9 * D_MODEL), 0.02),
        'norm_pre': 1.0 + nrm((DEPTH, 3, D_MODEL), 0.05),
        'norm_post': 1.0 + nrm((DEPTH, 3, D_MODEL), 0.05),
        'ffn1_in': nrm((DEPTH, D_MODEL, 2 * D_FF), D_MODEL ** -0.5),
        'ffn1_out': nrm((DEPTH, D_FF, D_MODEL), D_FF ** -0.5),
        'w_in': nrm((DEPTH, D_MODEL, IN_WIDTH), D_MODEL ** -0.5),
        'ssm_log_dt': log_dt,
        'ssm_a_re': -0.5 + nrm((DEPTH, N_GROUPS, SSM_N), 0.01),
        'ssm_a_im': math.pi * jnp.arange(SSM_N, dtype=f32) + nrm((DEPTH, N_GROUPS, SSM_N), 0.01),
        'ssm_b_re': nrm((DEPTH, N_GROUPS, SSM_N, SSM_GROUP), (2 * SSM_GROUP) ** -0.5),
        'ssm_b_im': nrm((DEPTH, N_GROUPS, SSM_N, SSM_GROUP), (2 * SSM_GROUP) ** -0.5),
        'ssm_c_re': nrm((DEPTH, N_GROUPS, SSM_GROUP, SSM_N), (2 * SSM_N) ** -0.5),
        'ssm_c_im': nrm((DEPTH, N_GROUPS, SSM_GROUP, SSM_N), (2 * SSM_N) ** -0.5),
        'ssm_d': nrm((DEPTH, D_SSM), 0.5),
        'glu_w': nrm((DEPTH, D_SSM, D_MODEL), D_SSM ** -0.5),
        'glu_v': nrm((DEPTH, D_SSM, D_MODEL), D_SSM ** -0.5),
        'w_branch_attn': nrm((DEPTH, D_ATT, D_MODEL), D_ATT ** -0.5),
        'w_out': nrm((DEPTH, D_MODEL, D_MODEL), D_MODEL ** -0.5),
        'ffn2_in': nrm((DEPTH, D_MODEL, 2 * D_FF), D_MODEL ** -0.5),
        'ffn2_out': nrm((DEPTH, D_FF, D_MODEL), D_FF ** -0.5),
    }


def reference(x_prompt, x_sample, cache_k, cache_v, cache_idx_k, state_ssm_re, state_ssm_im, page_table,
              c_prompt, c_sample, mod_w, mod_b, norm_pre, norm_post, ffn1_in, ffn1_out, w_in,
              ssm_log_dt, ssm_a_re, ssm_a_im, ssm_b_re, ssm_b_im, ssm_c_re, ssm_c_im, ssm_d,
              glu_w, glu_v, w_branch_attn, w_out, ffn2_in, ffn2_out):
    weights = (mod_w, mod_b, norm_pre, norm_post, ffn1_in, ffn1_out, w_in, ssm_log_dt, ssm_a_re, ssm_a_im,
               ssm_b_re, ssm_b_im, ssm_c_re, ssm_c_im, ssm_d, glu_w, glu_v, w_branch_attn, w_out,
               ffn2_in, ffn2_out)
    y_prompt, y_sample = x_prompt, x_sample
    zeros = jnp.zeros((x_prompt.shape[0], N_GROUPS, SSM_N), state_ssm_re.dtype)
    new_p, new_s = [], []
    for l in range(DEPTH):
        p = tuple(w[l] for w in weights)
        y_prompt, st_p = decoder_layer(y_prompt, c_prompt, p, zeros, zeros, prompt_attention)
        attn_s = functools.partial(sample_attention, layer=l, cache_k=cache_k, cache_v=cache_v,
                                   cache_idx_k=cache_idx_k, page_table=page_table)
        y_sample, st_s = decoder_layer(y_sample, c_sample, p, state_ssm_re[l], state_ssm_im[l], attn_s)
        new_p.append(st_p)
        new_s.append(st_s)
    stack = lambda states, i: jnp.stack([s[i] for s in states])
    return (y_prompt, y_sample,
            stack(new_p, 0), stack(new_p, 1), stack(new_p, 2), stack(new_p, 3), stack(new_p, 4),
            stack(new_s, 0), stack(new_s, 1), stack(new_s, 2), stack(new_s, 3), stack(new_s, 4))
```

```python
import functools
import math

import jax
import jax.numpy as jnp
import numpy as np
from jax import lax
from jax.experimental import pallas as pl
from jax.experimental.pallas import tpu as pltpu

F32 = jnp.float32
BF16 = jnp.bfloat16
I32 = jnp.int32

EPS = 1e-6
SSM_GROUP = 16
SSM_N = 64
N_HEADS = 8
HEAD_DIM = 64
IDX_HEADS = 8
IDX_DIM = 64
INDEX_TOPK = 256
PAGE_SIZE = 128
IDX_SCALE = IDX_DIM ** -0.5 * IDX_HEADS ** -0.5
ATT_SCALE = HEAD_DIM ** -0.5

LANES = 128
SUBLANES = 8
VMEM_LIMIT_BYTES = 56 * 1024 * 1024
CHUNK = 8
OCTET = LANES // SSM_GROUP
QB = 256
NEG_BIG = float(np.finfo(np.float32).min)
FLOAT_MID_ITERS = 14


def _cparams(*sem):
    return pltpu.CompilerParams(dimension_semantics=sem, vmem_limit_bytes=VMEM_LIMIT_BYTES)


def _const_spec(shape):
    nd = len(shape)
    return pl.BlockSpec(shape, lambda *_: (0,) * nd, pipeline_mode=pl.Buffered(1))


def _rms(x, g):
    ms = jnp.mean(x * x, axis=-1, keepdims=True)
    return x * lax.rsqrt(ms + EPS) * g


def _dot(a, b):
    return jnp.dot(a, b, preferred_element_type=F32)


def _dot_nt(a, b):
    return lax.dot_general(a, b, (((1,), (1,)), ((), ())), preferred_element_type=F32)


def _mod_kernel(c_ref, w_ref, b_ref, o_ref):
    a = jax.nn.silu(c_ref[...]).astype(BF16)
    o_ref[...] = _dot(a, w_ref[...].astype(BF16)) + b_ref[...]


def _modulation(c_all, mod_w, mod_b):
    depth, d, n = mod_w.shape
    r = c_all.shape[0]
    tn = 1024
    return pl.pallas_call(
        _mod_kernel,
        grid=(depth, n // tn),
        in_specs=[
            pl.BlockSpec((r, d), lambda l, j: (0, 0)),
            pl.BlockSpec((None, d, tn), lambda l, j: (l, 0, j)),
            pl.BlockSpec((None, 1, tn), lambda l, j: (l, 0, j)),
        ],
        out_specs=pl.BlockSpec((None, r, tn), lambda l, j: (l, 0, j)),
        out_shape=jax.ShapeDtypeStruct((depth, r, n), F32),
        compiler_params=_cparams("parallel", "parallel"),
        name="modulation",
    )(c_all, mod_w, mod_b.reshape(depth, 1, n))


def _mod_spec(nb, d, rows_per_batch_block):
    return pl.BlockSpec((9, None, nb, d), lambda i: (0, i // rows_per_batch_block, 0, 0))


def _ffn_kernel(x_ref, mod_ref, npre_ref, npost_ref, win_ref, wout_ref, o_ref, h_ref, a_ref, *, k, dff, ck):
    x = x_ref[...]
    sh, sc, gt = mod_ref[3 * k], mod_ref[3 * k + 1], mod_ref[3 * k + 2]
    h = _rms(x, npre_ref[k:k + 1, :]) * (1.0 + sc) + sh
    h_ref[...] = h.astype(BF16)
    for c in range(dff // ck):
        hb = h_ref[...]
        g = _dot(hb, win_ref[:, c * ck:(c + 1) * ck])
        u = _dot(hb, win_ref[:, dff + c * ck:dff + (c + 1) * ck])
        a_ref[:, c * ck:(c + 1) * ck] = (jax.nn.silu(g) * u).astype(BF16)
    y = _dot(a_ref[...], wout_ref[...])
    o_ref[...] = x + 0.5 * gt * _rms(y, npost_ref[k:k + 1, :])


def _ffn(x, mod, npre, npost, w_in, w_out, *, k, bm, blocks_per_batch):
    m, d = x.shape
    dff = w_out.shape[0]
    nb = mod.shape[2]
    kern = functools.partial(_ffn_kernel, k=k, dff=dff, ck=256)
    return pl.pallas_call(
        kern,
        grid=(m // bm,),
        in_specs=[
            pl.BlockSpec((bm, d), lambda i: (i, 0)),
            _mod_spec(nb, d, blocks_per_batch),
            _const_spec(npre.shape),
            _const_spec(npost.shape),
            _const_spec(w_in.shape),
            _const_spec(w_out.shape),
        ],
        out_specs=pl.BlockSpec((bm, d), lambda i: (i, 0)),
        out_shape=jax.ShapeDtypeStruct((m, d), F32),
        scratch_shapes=[pltpu.VMEM((bm, d), BF16), pltpu.VMEM((bm, dff), BF16)],
        compiler_params=_cparams("parallel"),
        name=f"ffn{k}",
    )(x, mod, npre, npost, w_in, w_out)


def _proj_prompt_kernel(x_ref, mod_ref, npre_ref, wrow_ref, wt_ref,
                        u_ref, k_ref, v_ref, kidx_ref, kh_ref, kidxb_ref, v3_ref, qt_ref, qit_ref, wt_out_ref,
                        *, d_ssm, d_att):
    x = x_ref[...]
    sh, sc = mod_ref[3], mod_ref[4]
    h = (_rms(x, npre_ref[1:2, :]) * (1.0 + sc) + sh).astype(BF16)
    pr = _dot(h, wrow_ref[...])
    o = d_ssm
    u_ref[...] = pr[:, 0:o]
    k = pr[:, o:o + d_att]
    v = pr[:, o + d_att:o + 2 * d_att]
    kidx = pr[:, o + 2 * d_att:o + 2 * d_att + IDX_DIM]
    k_ref[...] = k
    v_ref[...] = v
    kidx_ref[...] = kidx
    kidxb_ref[...] = kidx.astype(BF16)
    for hh in range(N_HEADS):
        kh_ref[hh] = k[:, hh * HEAD_DIM:(hh + 1) * HEAD_DIM].astype(BF16)
    pt = _dot_nt(wt_ref[...], h)
    qt_ref[...] = (pt[0:d_att] * ATT_SCALE).astype(BF16)
    qit_ref[...] = pt[d_att:2 * d_att].astype(BF16)
    vt = pt[2 * d_att:3 * d_att].astype(BF16)
    for jj in range(v3_ref.shape[0]):
        v3_ref[jj] = vt[:, jj * QB:(jj + 1) * QB]
    wt_out_ref[...] = pt[3 * d_att:3 * d_att + IDX_HEADS] * IDX_SCALE


def _proj_prompt(x, mod, npre, w_row, w_t, *, bm, blocks_per_batch, d_ssm, d_att):
    m, d = x.shape
    row = lambda w: pl.BlockSpec((bm, w), lambda i: (i, 0))
    col = lambda r: pl.BlockSpec((r, bm), lambda i: (0, i))
    kern = functools.partial(_proj_prompt_kernel, d_ssm=d_ssm, d_att=d_att)
    return pl.pallas_call(
        kern,
        grid=(m // bm,),
        in_specs=[
            pl.BlockSpec((bm, d), lambda i: (i, 0)),
            _mod_spec(1, d, blocks_per_batch),
            _const_spec(npre.shape),
            _const_spec(w_row.shape),
            _const_spec(w_t.shape),
        ],
        out_specs=[
            row(d_ssm), row(d_att), row(d_att), row(IDX_DIM),
            pl.BlockSpec((N_HEADS, bm, HEAD_DIM), lambda i: (0, i, 0)),
            row(IDX_DIM),
            pl.BlockSpec((bm // QB, d_att, QB), lambda i: (i, 0, 0)),
            col(d_att), col(d_att), col(IDX_HEADS),
        ],
        out_shape=[
            jax.ShapeDtypeStruct((m, d_ssm), F32),
            jax.ShapeDtypeStruct((m, d_att), F32),
            jax.ShapeDtypeStruct((m, d_att), F32),
            jax.ShapeDtypeStruct((m, IDX_DIM), F32),
            jax.ShapeDtypeStruct((N_HEADS, m, HEAD_DIM), BF16),
            jax.ShapeDtypeStruct((m, IDX_DIM), BF16),
            jax.ShapeDtypeStruct((m // QB, d_att, QB), BF16),
            jax.ShapeDtypeStruct((d_att, m), BF16),
            jax.ShapeDtypeStruct((d_att, m), BF16),
            jax.ShapeDtypeStruct((IDX_HEADS, m), F32),
        ],
        compiler_params=_cparams("parallel"),
        name="proj_prompt",
    )(x, mod, npre, w_row, w_t)


def _proj_sample_kernel(x_ref, mod_ref, npre_ref, w_ref, o_ref):
    x = x_ref[...]
    sh, sc = mod_ref[3], mod_ref[4]
    h = (_rms(x, npre_ref[1:2, :]) * (1.0 + sc) + sh).astype(BF16)
    o_ref[...] = _dot(h, w_ref[...])


def _proj_sample(x, mod, npre, w):
    m, d = x.shape
    n = w.shape[1]
    return pl.pallas_call(
        _proj_sample_kernel,
        grid=(1,),
        in_specs=[
            pl.BlockSpec((m, d), lambda i: (0, 0)),
            _mod_spec(m, d, 1),
            _const_spec(npre.shape),
            _const_spec(w.shape),
        ],
        out_specs=pl.BlockSpec((m, n), lambda i: (0, 0)),
        out_shape=jax.ShapeDtypeStruct((m, n), F32),
        compiler_params=_cparams("arbitrary"),
        name="proj_sample",
    )(x, mod, npre, w)


def _ssm_prep_kernel(ldt_ref, ara_ref, aia_ref, arb_ref, aib_ref, bre_ref, bim_ref, btre_ref, btim_ref,
                     cre_ref, cim_ref,
                     kk_ref, wre_ref, wim_ref, pre_ref, pim_ref, al_ref, ab_ref, bbre_ref, bbim_ref):
    dt = jnp.exp(ldt_ref[...])

    def powers(ar, ai, k):
        mag = jnp.exp(dt * ar * k)
        ph = dt * ai * k
        return mag * jnp.cos(ph), mag * jnp.sin(ph)

    def zoh(ar, ai, b_re, b_im):
        abr, abi = powers(ar, ai, 1.0)
        den = ar * ar + ai * ai
        z_re = ((abr - 1.0) * ar + abi * ai) / den
        z_im = (abi * ar - (abr - 1.0) * ai) / den
        return z_re * b_re - z_im * b_im, z_re * b_im + z_im * b_re

    ar_b, ai_b = arb_ref[...], aib_ref[...]
    bb_re, bb_im = zoh(ar_b, ai_b, bre_ref[...], bim_ref[...])
    bbre_ref[...] = bb_re
    bbim_ref[...] = bb_im
    for tau in range(CHUNK):
        pr, pi = powers(ar_b, ai_b, float(CHUNK - 1 - tau))
        wre_ref[tau] = pr * bb_re - pi * bb_im
        wim_ref[tau] = pr * bb_im + pi * bb_re

    ar_a, ai_a = ara_ref[...], aia_ref[...]
    c_re, c_im = cre_ref[...], cim_ref[...]
    bt_re, bt_im = zoh(ar_a, ai_a, btre_ref[...], btim_ref[...])
    lr, li = powers(ar_a, ai_a, float(CHUNK))
    al_ref[0] = lr
    al_ref[1] = li
    r1, i1 = powers(ar_a, ai_a, 1.0)
    ab_ref[0] = r1
    ab_ref[1] = i1
    ck_re, ck_im = [], []
    for k in range(CHUNK):
        pr, pi = powers(ar_a, ai_a, float(k))
        ck_re.append(c_re * pr - c_im * pi)
        ck_im.append(c_re * pi + c_im * pr)
        pr1, pi1 = powers(ar_a, ai_a, float(k + 1))
        pre_ref[k] = c_re * pr1 - c_im * pi1
        pim_ref[k] = -(c_re * pi1 + c_im * pr1)
    bnt = functools.partial(lax.dot_general, dimension_numbers=(((2,), (2,)), ((0,), (0,))),
                            precision=lax.Precision.HIGHEST, preferred_element_type=F32)
    kk_ref[...] = (bnt(jnp.concatenate(ck_re, axis=1), bt_re) - bnt(jnp.concatenate(ck_im, axis=1), bt_im))


def _ssm_prep(log_dt, a_re, a_im, b_re, b_im, c_re, c_im):
    g, n = a_re.shape
    j = SSM_GROUP
    shp = lambda *s: jax.ShapeDtypeStruct(s, F32)
    return pl.pallas_call(
        _ssm_prep_kernel,
        out_shape=[shp(g, CHUNK * j, j), shp(CHUNK, g, n, j), shp(CHUNK, g, n, j),
                   shp(CHUNK, g, j, n), shp(CHUNK, g, j, n), shp(2, g, 1, n), shp(2, g, 1, n),
                   shp(g, n, j), shp(g, n, j)],
        compiler_params=pltpu.CompilerParams(vmem_limit_bytes=VMEM_LIMIT_BYTES),
        name="ssm_prep",
    )(log_dt.reshape(g, 1, 1), a_re.reshape(g, 1, n), a_im.reshape(g, 1, n),
      a_re.reshape(g, n, 1), a_im.reshape(g, n, 1), b_re, b_im,
      b_re.transpose(0, 2, 1), b_im.transpose(0, 2, 1), c_re, c_im)


def _ssm_operators(prep):
    kk, w_re, w_im, p_re, p_im, al, ab, bb_re, bb_im = prep
    g = kk.shape[0]
    no, j, n = g // OCTET, SSM_GROUP, SSM_N
    eye = jnp.eye(OCTET, dtype=F32)
    kk = kk.reshape(no, OCTET, CHUNK, j, j)
    lag = np.arange(CHUNK)[None, :] - np.arange(CHUNK)[:, None]
    kt = jnp.where((lag >= 0)[None, None, :, :, None, None], kk[:, :, np.clip(lag, 0, None)], 0.0)
    m_op = jnp.einsum("ogstjk,gh->osgkthj", kt, eye).reshape(no, CHUNK * LANES, CHUNK * LANES)
    w = jnp.stack([w_re, w_im], 0).reshape(2, CHUNK, no, OCTET, n, j)
    w_op = jnp.einsum("ctognk,gh->otgkchn", w, eye).reshape(no, CHUNK * LANES, 2 * OCTET * n)
    p = jnp.stack([p_re, p_im], 0).reshape(2, CHUNK, no, OCTET, j, n)
    p_op = jnp.einsum("ctogjn,gh->ocgnthj", p, eye).reshape(no, 2 * OCTET * n, CHUNK * LANES)
    a_chunk = al.reshape(2, no, OCTET * n).transpose(1, 0, 2)
    eg = jnp.eye(g, dtype=F32)
    b_step = jnp.einsum("cgnk,gh->gkchn", jnp.stack([bb_re, bb_im], 0), eg).reshape(g * j, 2 * g * n)
    return dict(m=m_op.astype(BF16), w=w_op.astype(BF16), p=p_op.astype(BF16), a_chunk=a_chunk,
                b_step=b_step, abar=ab.reshape(2, g * n))


def _ssm_prompt_kernel(u_ref, m_ref, w_ref, p_ref, al_ref, d_ref, y_ref, sfin_ref,
                       uo_ref, v_ref, sc_ref, *, r):
    half = sc_ref.shape[1] // 2
    for tau in range(CHUNK):
        uo_ref[:, tau * LANES:(tau + 1) * LANES] = u_ref[pl.ds(tau, r, stride=CHUNK), :].astype(BF16)
    uo = uo_ref[...]
    v_ref[...] = _dot(uo, w_ref[...])
    a_r, a_i = al_ref[0:1, :], al_ref[1:2, :]

    def step(c, carry):
        s_r, s_i = carry
        sc_ref[pl.ds(c, 1), 0:half] = s_r
        sc_ref[pl.ds(c, 1), half:2 * half] = s_i
        v = v_ref[pl.ds(c, 1), :]
        return (a_r * s_r - a_i * s_i + v[:, 0:half], a_r * s_i + a_i * s_r + v[:, half:2 * half])

    zero = jnp.zeros((1, half), F32)
    s_r, s_i = lax.fori_loop(0, r, step, (zero, zero))
    sfin_ref[0:1, :] = s_r
    sfin_ref[1:2, :] = s_i
    y = _dot(uo, m_ref[...]) + _dot(sc_ref[...].astype(BF16), p_ref[...])
    d = d_ref[...]
    for t in range(CHUNK):
        rows = pl.ds(t, r, stride=CHUNK)
        y_ref[rows, :] = y[:, t * LANES:(t + 1) * LANES] + d * u_ref[rows, :]


def _ssm_prompt(u, ops, d_skip, *, batch, t_len):
    m, d_ssm = u.shape
    no = d_ssm // LANES
    r = t_len // CHUNK
    kw = CHUNK * LANES
    sw = ops["w"].shape[2]
    op_spec = lambda a, b: pl.BlockSpec((None, a, b), lambda o, bb: (o, 0, 0))
    y, sfin = pl.pallas_call(
        functools.partial(_ssm_prompt_kernel, r=r),
        grid=(no, batch),
        in_specs=[
            pl.BlockSpec((t_len, LANES), lambda o, bb: (bb, o)),
            op_spec(kw, kw), op_spec(kw, sw), op_spec(sw, kw),
            pl.BlockSpec((None, 2, sw // 2), lambda o, bb: (o, 0, 0)),
            pl.BlockSpec((1, LANES), lambda o, bb: (0, o)),
        ],
        out_specs=[
            pl.BlockSpec((t_len, LANES), lambda o, bb: (bb, o)),
            pl.BlockSpec((None, None, 2, sw // 2), lambda o, bb: (bb, o, 0, 0)),
        ],
        out_shape=[jax.ShapeDtypeStruct((m, d_ssm), F32),
                   jax.ShapeDtypeStruct((batch, no, 2, sw // 2), F32)],
        scratch_shapes=[pltpu.VMEM((r, kw), BF16), pltpu.VMEM((r, sw), F32), pltpu.VMEM((r, sw), F32)],
        compiler_params=_cparams("parallel", "parallel"),
        name="ssm_prompt",
    )(u, ops["m"], ops["w"], ops["p"], ops["a_chunk"], d_skip.reshape(1, d_ssm))
    s = sfin.reshape(batch, no, 2, OCTET, SSM_N).transpose(2, 0, 1, 3, 4).reshape(2, batch, no * OCTET, SSM_N)
    return y, s[0], s[1]


def _ssm_sample_kernel(u_ref, h_ref, ab_ref, bstep_ref, cstep_ref, d_ref, y_ref, s_ref):
    u = u_ref[...]
    half = h_ref.shape[2]
    bu = jnp.dot(u, bstep_ref[...], precision=lax.Precision.HIGHEST, preferred_element_type=F32)
    a_r, a_i = ab_ref[0:1, :], ab_ref[1:2, :]
    h_r, h_i = h_ref[0], h_ref[1]
    s_r = a_r * h_r - a_i * h_i + bu[:, 0:half]
    s_i = a_r * h_i + a_i * h_r + bu[:, half:2 * half]
    s_ref[0] = s_r
    s_ref[1] = s_i
    s = jnp.concatenate([s_r, s_i], axis=1).astype(BF16)
    y_ref[...] = _dot(s, cstep_ref[...]) + d_ref[...] * u


def _ssm_sample(u, h_re, h_im, ops, c_step, d_skip):
    bsz, d_ssm = u.shape
    gn = h_re.shape[1] * h_re.shape[2]
    h = jnp.stack([h_re.reshape(bsz, gn), h_im.reshape(bsz, gn)], 0)
    y, s = pl.pallas_call(
        _ssm_sample_kernel,
        out_shape=[jax.ShapeDtypeStruct((bsz, d_ssm), F32), jax.ShapeDtypeStruct((2, bsz, gn), F32)],
        compiler_params=pltpu.CompilerParams(vmem_limit_bytes=VMEM_LIMIT_BYTES),
        name="ssm_sample",
    )(u, h, ops["abar"], ops["b_step"], c_step, d_skip.reshape(1, d_ssm))
    return y, s[0].reshape(h_re.shape), s[1].reshape(h_im.shape)


def _f2key(x):
    b = lax.bitcast_convert_type(x, I32)
    return b ^ ((b >> 31) & 0x7FFFFFFF)


def _key2f(k):
    return lax.bitcast_convert_type(k ^ ((k >> 31) & 0x7FFFFFFF), F32)


def _search_step(lo, hi, it):
    mid_k = (lo >> 1) + (hi >> 1) + (lo & hi & 1)
    fm = _f2key(0.5 * _key2f(lo) + 0.5 * _key2f(hi))
    use_f = (fm > lo) & (fm < hi) & (it < FLOAT_MID_ITERS)
    return jnp.where(use_f, fm, mid_k)


def _search_update(state, mid, cnt, topk):
    lo, hi, thr, done, tie = state
    hit = cnt == topk
    lo_n = jnp.where(cnt > topk, mid, lo)
    hi_n = jnp.where(cnt < topk, mid, hi)
    adj = (hi_n == lo_n + 1) & jnp.logical_not(hit)
    fin = hit | adj
    thr_n = jnp.where(hit, _key2f(mid), _key2f(lo_n))
    act = done == 0
    lo = jnp.where(act, lo_n, lo)
    hi = jnp.where(act, hi_n, hi)
    thr = jnp.where(act & fin, thr_n, thr)
    tie = jnp.where(act & adj, 1, tie)
    done = jnp.where(act & fin, 1, done)
    return lo, hi, thr, done, tie


def _attn_prompt_kernel(qt_ref, qit_ref, wt_ref, kh_ref, kidx_ref, v3_ref, o_ref, sc_ref, lg_ref, *, topk):
    i = pl.program_id(1)
    nk = i + 1
    qpos = i * QB + lax.broadcasted_iota(I32, (QB, QB), 1)
    krow = lax.broadcasted_iota(I32, (QB, QB), 0)
    fold = lambda x: x.reshape(QB // SUBLANES, SUBLANES, QB)

    def score_tile(j, _):
        kx = kidx_ref[pl.ds(pl.multiple_of(j * QB, QB), QB), :]
        acc = jnp.zeros((QB, QB), F32)
        for h in range(IDX_HEADS):
            s = _dot(kx, qit_ref[h * IDX_DIM:(h + 1) * IDX_DIM, :])
            acc = acc + jnp.maximum(s, 0.0) * wt_ref[h:h + 1, :]
        valid = (j * QB + krow) <= qpos
        sc_ref[j] = jnp.where(valid, acc, -jnp.inf)
        return 0

    lax.fori_loop(0, nk, score_tile, 0)

    def minmax_tile(j, carry):
        mn, mx = carry
        x = sc_ref[j]
        mx = jnp.maximum(mx, jnp.max(fold(x), axis=0))
        mn = jnp.minimum(mn, jnp.min(fold(jnp.where(x == -jnp.inf, jnp.inf, x)), axis=0))
        return mn, mx

    mn, mx = lax.fori_loop(0, nk, minmax_tile,
                           (jnp.full((SUBLANES, QB), jnp.inf, F32), jnp.full((SUBLANES, QB), -jnp.inf, F32)))
    rep = lambda x: jnp.broadcast_to(x, (SUBLANES, QB))
    mn = rep(jnp.min(mn, axis=0, keepdims=True))
    mx = rep(jnp.max(mx, axis=0, keepdims=True))
    n_valid = qpos[0:SUBLANES, :] + 1
    need = n_valid > topk

    def count_ge(t):
        t1 = t[0:1, :]

        def body(j, c):
            return c + jnp.sum(fold((sc_ref[j] >= t1).astype(I32)), axis=0)

        c = lax.fori_loop(0, nk, body, jnp.zeros((SUBLANES, QB), I32))
        return rep(jnp.sum(c, axis=0, keepdims=True))

    def cond(carry):
        it, state = carry
        return jnp.logical_and(it < 80, jnp.min(state[3]) == 0)

    def body(carry):
        it, state = carry
        mid = _search_step(state[0], state[1], it)
        cnt = count_ge(_key2f(mid))
        return it + 1, _search_update(state, mid, cnt, topk)

    init = (_f2key(mn), _f2key(mx) + 1, jnp.full((SUBLANES, QB), NEG_BIG, F32),
            jnp.where(need, 0, 1).astype(I32), jnp.zeros((SUBLANES, QB), I32))
    _, (_, _, thr, _, tie) = lax.while_loop(cond, body, (jnp.int32(0), init))

    @pl.when(jnp.max(tie) > 0)
    def _():
        tri = (lax.broadcasted_iota(I32, (QB, QB), 0) >= lax.broadcasted_iota(I32, (QB, QB), 1)).astype(BF16)
        thr1, tie1 = thr[0:1, :], tie[0:1, :] > 0

        def gt_tile(j, c):
            return c + jnp.sum((sc_ref[j] > thr1).astype(I32), axis=0, keepdims=True)

        quota = topk - lax.fori_loop(0, nk, gt_tile, jnp.zeros((1, QB), I32))

        def tie_tile(j, before):
            x = sc_ref[j]
            t = (x == thr1) & tie1
            rank = before + _dot(tri, t.astype(BF16)).astype(I32)
            sc_ref[j] = jnp.where(t & (rank > quota), -jnp.inf, x)
            return before + jnp.sum(t.astype(I32), axis=0, keepdims=True)

        lax.fori_loop(0, nk, tie_tile, jnp.zeros((1, QB), I32))

    thr1 = thr[0:1, :]
    outs = []
    for h in range(N_HEADS):
        q_h = qt_ref[h * HEAD_DIM:(h + 1) * HEAD_DIM, :]

        def logit_tile(j, m):
            k_t = kh_ref[h, pl.ds(pl.multiple_of(j * QB, QB), QB), :]
            lg = jnp.where(sc_ref[j] >= thr1, _dot(k_t, q_h), -jnp.inf)
            lg_ref[j] = lg
            return jnp.maximum(m, jnp.max(fold(lg), axis=0))

        m = lax.fori_loop(0, nk, logit_tile, jnp.full((SUBLANES, QB), -jnp.inf, F32))
        m1 = jnp.max(m, axis=0, keepdims=True)

        def pv_tile(j, carry):
            l, acc = carry
            p = jnp.exp(lg_ref[j] - m1)
            l = l + jnp.sum(fold(p), axis=0)
            acc = acc + _dot(v3_ref[j, h * HEAD_DIM:(h + 1) * HEAD_DIM, :], p.astype(BF16))
            return l, acc

        l, acc = lax.fori_loop(0, nk, pv_tile,
                               (jnp.zeros((SUBLANES, QB), F32), jnp.zeros((HEAD_DIM, QB), F32)))
        outs.append(acc / jnp.sum(l, axis=0, keepdims=True))
    o_ref[...] = jnp.transpose(jnp.concatenate(outs, axis=0)).astype(BF16)


def _attn_prompt(qt, qit, wt, kh, kidxb, v3, *, batch, t_len, topk):
    d_att, m = qt.shape
    nq = t_len // QB
    return pl.pallas_call(
        functools.partial(_attn_prompt_kernel, topk=topk),
        grid=(batch, nq),
        in_specs=[
            pl.BlockSpec((d_att, QB), lambda b, i: (0, b * nq + i)),
            pl.BlockSpec((d_att, QB), lambda b, i: (0, b * nq + i)),
            pl.BlockSpec((IDX_HEADS, QB), lambda b, i: (0, b * nq + i)),
            pl.BlockSpec((N_HEADS, t_len, HEAD_DIM), lambda b, i: (0, b, 0)),
            pl.BlockSpec((t_len, IDX_DIM), lambda b, i: (b, 0)),
            pl.BlockSpec((nq, d_att, QB), lambda b, i: (b, 0, 0)),
        ],
        out_specs=pl.BlockSpec((QB, d_att), lambda b, i: (b * nq + i, 0)),
        out_shape=jax.ShapeDtypeStruct((m, d_att), BF16),
        scratch_shapes=[pltpu.VMEM((nq, QB, QB), F32), pltpu.VMEM((nq, QB, QB), F32)],
        compiler_params=_cparams("parallel", "arbitrary"),
        name="attn_prompt",
    )(qt, qit, wt, kh, kidxb, v3)


def _select_sample_kernel(pt_ref, qi_ref, w_ref, kn_ref, cache_ref, o_ref, kbuf_ref, sc_ref, sem,
                          *, layer, n_pages, topk):
    b = pl.program_id(0)
    copies = []
    for p in range(n_pages):
        cp = pltpu.make_async_copy(cache_ref.at[layer, pt_ref[b, p]], kbuf_ref.at[p], sem)
        cp.start()
        copies.append(cp)
    for cp in copies:
        cp.wait()

    qi = qi_ref[...].astype(BF16)
    w = w_ref[...] * IDX_SCALE

    def page_scores(p, _):
        s = _dot_nt(qi, kbuf_ref[p].astype(BF16))
        sc_ref[pl.ds(p, 1), :] = jnp.sum(jnp.maximum(s, 0.0) * w, axis=0, keepdims=True)
        return 0

    lax.fori_loop(0, n_pages, page_scores, 0)
    s_new_h = jnp.sum(qi.astype(F32) * kn_ref[...].astype(BF16).astype(F32), axis=1, keepdims=True)
    s_new = jnp.sum(jnp.maximum(s_new_h, 0.0) * w, axis=0, keepdims=True)
    sc = sc_ref[...]

    def total(x, op=jnp.sum):
        return op(op(x, axis=1, keepdims=True), axis=0, keepdims=True)

    mx = jnp.maximum(total(sc, jnp.max), s_new)
    mn = jnp.minimum(total(sc, jnp.min), s_new)

    def count_ge(t):
        return total((sc >= t).astype(I32)) + (s_new >= t).astype(I32)

    def cond(carry):
        it, state = carry
        return jnp.logical_and(it < 80, jnp.min(state[3]) == 0)

    def body(carry):
        it, state = carry
        mid = _search_step(state[0], state[1], it)
        return it + 1, _search_update(state, mid, count_ge(_key2f(mid)), topk)

    n_keys = n_pages * PAGE_SIZE + 1
    done0 = jnp.full((1, 1), 0 if n_keys > topk else 1, I32)
    init = (_f2key(mn), _f2key(mx) + 1, jnp.full((1, 1), NEG_BIG, F32), done0, jnp.zeros((1, 1), I32))
    _, (_, _, thr, _, tie) = lax.while_loop(cond, body, (jnp.int32(0), init))

    upper = (lax.broadcasted_iota(I32, (PAGE_SIZE, PAGE_SIZE), 0)
             <= lax.broadcasted_iota(I32, (PAGE_SIZE, PAGE_SIZE), 1)).astype(BF16)
    lower = (lax.broadcasted_iota(I32, (n_pages, n_pages), 1)
             < lax.broadcasted_iota(I32, (n_pages, n_pages), 0)).astype(BF16)

    def flat_rank(mask):
        mb = mask.astype(BF16)
        incl = _dot(mb, upper)
        before = jnp.sum(_dot(lower, mb), axis=1, keepdims=True)
        return (incl + before).astype(I32)

    gt = sc > thr
    tied = tie > 0
    is_tie = (sc == thr) & tied
    quota = jnp.where(tied, topk - total(gt.astype(I32)) - (s_new > thr).astype(I32), topk)
    sel = gt | ((sc == thr) & (flat_rank(is_tie) <= quota))
    n_tie_past = total(is_tie.astype(I32))
    new_sel = (s_new > thr) | ((s_new == thr) & (n_tie_past < quota))

    pos = flat_rank(sel) - 1
    n_past = total(sel.astype(I32))
    kio = lax.broadcasted_iota(I32, (topk, PAGE_SIZE), 0)
    lane = lax.broadcasted_iota(I32, (PAGE_SIZE, PAGE_SIZE), 1)
    slot_col = jnp.where(lane == 0, lax.broadcasted_iota(I32, (PAGE_SIZE, PAGE_SIZE), 0), 0)
    acc = jnp.zeros((topk, PAGE_SIZE), F32)
    for p in range(n_pages):
        oh = ((pos[p:p + 1, :] == kio) & sel[p:p + 1, :]).astype(BF16)
        vals = jnp.where(lane == 1, p, slot_col).astype(BF16)
        acc = acc + _dot(oh, vals)
    acc_t = jnp.transpose(acc)
    rowi = lax.broadcasted_iota(I32, (SUBLANES, topk), 0)
    out = acc_t[0:SUBLANES, :]
    out = jnp.where(rowi == 2, n_past.astype(F32), out)
    out = jnp.where(rowi == 3, new_sel.astype(F32), out)
    o_ref[...] = out


def _select_sample(page_table, q_idx, w_idx, k_idx_new, cache_idx_k, *, layer, topk):
    bsz, n_pages = page_table.shape
    return pl.pallas_call(
        functools.partial(_select_sample_kernel, layer=layer, n_pages=n_pages, topk=topk),
        grid_spec=pltpu.PrefetchScalarGridSpec(
            num_scalar_prefetch=1,
            grid=(bsz,),
            in_specs=[
                pl.BlockSpec((None, IDX_HEADS, IDX_DIM), lambda b, pt: (b, 0, 0)),
                pl.BlockSpec((None, IDX_HEADS, 1), lambda b, pt: (b, 0, 0)),
                pl.BlockSpec((None, 1, IDX_DIM), lambda b, pt: (b, 0, 0)),
                pl.BlockSpec(memory_space=pl.ANY),
            ],
            out_specs=pl.BlockSpec((None, SUBLANES, topk), lambda b, pt: (b, 0, 0)),
            scratch_shapes=[pltpu.VMEM((n_pages, PAGE_SIZE, IDX_DIM), F32),
                            pltpu.VMEM((n_pages, PAGE_SIZE), F32),
                            pltpu.SemaphoreType.DMA(())],
        ),
        out_shape=jax.ShapeDtypeStruct((bsz, SUBLANES, topk), F32),
        compiler_params=_cparams("arbitrary"),
        name="select_sample",
    )(page_table, q_idx, w_idx, k_idx_new, cache_idx_k)


def _attend_sample_kernel(pt_ref, page_ref, slot_ref, cnt_ref, q_ref, kn_ref, vn_ref, ck_ref, cv_ref, o_ref,
                          kbuf_ref, vbuf_ref, sem_k, sem_v, *, layer, topk):
    b = pl.program_id(0)

    def row_copies(k, page, slot):
        return (pltpu.make_async_copy(ck_ref.at[layer, page, slot], kbuf_ref.at[:, k], sem_k),
                pltpu.make_async_copy(cv_ref.at[layer, page, slot], vbuf_ref.at[:, k], sem_v))

    def issue(k, _):
        for cp in row_copies(k, pt_ref[b, page_ref[b, k]], slot_ref[b, k]):
            cp.start()
        return 0

    lax.fori_loop(0, topk, issue, 0)

    def drain(k, _):
        for cp in row_copies(k, 0, 0):
            cp.wait()
        return 0

    lax.fori_loop(0, topk, drain, 0)

    n_past = cnt_ref[b, 0]
    new_sel = cnt_ref[b, 1]
    q = q_ref[...]
    qb = q.astype(BF16)
    kn, vn = kn_ref[...], vn_ref[...]
    kpos = lax.broadcasted_iota(I32, (N_HEADS, topk), 1)
    hrow = lax.broadcasted_iota(I32, (N_HEADS, topk), 0)
    logits = jnp.zeros((N_HEADS, topk), F32)
    for h in range(N_HEADS):
        logits = jnp.where(hrow == h, _dot_nt(qb, kbuf_ref[h].astype(BF16)), logits)
    logits = jnp.where(kpos < n_past, logits * ATT_SCALE, -jnp.inf)
    lg_new = jnp.sum(qb.astype(F32) * kn.astype(BF16).astype(F32), axis=1, keepdims=True) * ATT_SCALE
    lg_new = jnp.where(new_sel > 0, lg_new, -jnp.inf)
    m = jnp.maximum(jnp.max(logits, axis=1, keepdims=True), lg_new)
    p = jnp.exp(logits - m)
    p_new = jnp.exp(lg_new - m)
    denom = jnp.sum(p, axis=1, keepdims=True) + p_new
    p = (p / denom).astype(BF16)
    p_new = (p_new / denom).astype(BF16).astype(F32)
    hrow_o = lax.broadcasted_iota(I32, (N_HEADS, HEAD_DIM), 0)
    out = p_new * vn.astype(BF16).astype(F32)
    for h in range(N_HEADS):
        out = out + jnp.where(hrow_o == h, _dot(p, vbuf_ref[h].astype(BF16)), 0.0)
    o_ref[...] = out


def _attend_sample(page_table, page, slot, counts, q, k_new, v_new, cache_k, cache_v, *, layer, topk):
    bsz = q.shape[0]
    head = pl.BlockSpec((None, N_HEADS, HEAD_DIM), lambda b, *_: (b, 0, 0))
    return pl.pallas_call(
        functools.partial(_attend_sample_kernel, layer=layer, topk=topk),
        grid_spec=pltpu.PrefetchScalarGridSpec(
            num_scalar_prefetch=4,
            grid=(bsz,),
            in_specs=[head, head, head, pl.BlockSpec(memory_space=pl.ANY), pl.BlockSpec(memory_space=pl.ANY)],
            out_specs=head,
            scratch_shapes=[pltpu.VMEM((N_HEADS, topk, HEAD_DIM), F32),
                            pltpu.VMEM((N_HEADS, topk, HEAD_DIM), F32),
                            pltpu.SemaphoreType.DMA(()), pltpu.SemaphoreType.DMA(())],
        ),
        out_shape=jax.ShapeDtypeStruct((bsz, N_HEADS, HEAD_DIM), F32),
        compiler_params=_cparams("arbitrary"),
        name="attend_sample",
    )(page_table, page, slot, counts, q, k_new, v_new, cache_k, cache_v)


def _mix_kernel(x_ref, mod_ref, npre_ref, npost_ref, ys_ref, at_ref, gw_ref, gv_ref, wba_ref, wg_ref, wo_ref,
                o_ref):
    x = x_ref[...]
    d = x.shape[1]
    sh, sc, gt = mod_ref[3], mod_ref[4], mod_ref[5]
    h = (_rms(x, npre_ref[1:2, :]) * (1.0 + sc) + sh).astype(BF16)
    gates = _dot(h, wg_ref[...])
    ys = ys_ref[...].astype(BF16)
    y_a = _dot(ys, gw_ref[...]) * jax.nn.sigmoid(_dot(ys, gv_ref[...]))
    y_b = _dot(at_ref[...].astype(BF16), wba_ref[...])
    mixed = jax.nn.sigmoid(gates[:, 0:d]) * y_a + jax.nn.sigmoid(gates[:, d:2 * d]) * y_b
    y = _dot(mixed.astype(BF16), wo_ref[...])
    o_ref[...] = x + gt * _rms(y, npost_ref[1:2, :])


def _mix(x, mod, npre, npost, y_ssm, attn, glu_w, glu_v, wba, w_gates, w_out, *, bm, blocks_per_batch):
    m, d = x.shape
    nb = mod.shape[2]
    return pl.pallas_call(
        _mix_kernel,
        grid=(m // bm,),
        in_specs=[
            pl.BlockSpec((bm, d), lambda i: (i, 0)),
            _mod_spec(nb, d, blocks_per_batch),
            _const_spec(npre.shape), _const_spec(npost.shape),
            pl.BlockSpec((bm, y_ssm.shape[1]), lambda i: (i, 0)),
            pl.BlockSpec((bm, attn.shape[1]), lambda i: (i, 0)),
            _const_spec(glu_w.shape), _const_spec(glu_v.shape), _const_spec(wba.shape),
            _const_spec(w_gates.shape), _const_spec(w_out.shape),
        ],
        out_specs=pl.BlockSpec((bm, d), lambda i: (i, 0)),
        out_shape=jax.ShapeDtypeStruct((m, d), F32),
        compiler_params=_cparams("parallel"),
        name="mix",
    )(x, mod, npre, npost, y_ssm, attn, glu_w, glu_v, wba, w_gates, w_out)


def _pad_cols(w, n):
    return jnp.pad(w, ((0, 0), (0, n - w.shape[1])))


def kernel(x_prompt, x_sample, cache_k, cache_v, cache_idx_k, state_ssm_re, state_ssm_im, page_table,
           c_prompt, c_sample, mod_w, mod_b, norm_pre, norm_post, ffn1_in, ffn1_out, w_in,
           ssm_log_dt, ssm_a_re, ssm_a_im, ssm_b_re, ssm_b_im, ssm_c_re, ssm_c_im, ssm_d,
           glu_w, glu_v, w_branch_attn, w_out, ffn2_in, ffn2_out):
    batch, t_len, d = x_prompt.shape
    dec_batch, dec_seq, _ = x_sample.shape
    depth = mod_w.shape[0]
    d_ssm = ssm_d.shape[1]
    d_att = N_HEADS * HEAD_DIM
    n_groups = d_ssm // SSM_GROUP
    assert dec_seq == 1 and t_len % QB == 0 and d_ssm % LANES == 0
    m = batch * t_len
    bm = 512 if m % 512 == 0 else QB
    topk_p = min(INDEX_TOPK, t_len // 4)
    n_pages = page_table.shape[1]
    topk_s = min(INDEX_TOPK, (n_pages * PAGE_SIZE + dec_seq) // 4)

    mod = _modulation(jnp.concatenate([c_prompt, c_sample], 0), mod_w, mod_b).reshape(depth, -1, 9, d)

    widths = (d_ssm, d_att, d_att, d_att, IDX_HEADS * IDX_DIM, IDX_DIM, IDX_HEADS, d, d)
    off = np.concatenate([[0], np.cumsum(widths)])
    o_u, o_q, o_k, o_v, o_qi, o_ki, o_wi, o_ga, o_gb, o_end = (int(v) for v in off)

    xp = x_prompt.reshape(m, d)
    xs = x_sample.reshape(dec_batch, d)
    new_p, new_s = [], []
    for l in range(depth):
        bf = lambda w: w[l].astype(BF16)
        modp = mod[l, :batch].transpose(1, 0, 2).reshape(9, batch, 1, d)
        mods = mod[l, batch:].transpose(1, 0, 2).reshape(9, 1, dec_batch, d)
        npre, npost = norm_pre[l], norm_post[l]
        f1_in, f1_out, f2_in, f2_out = bf(ffn1_in), bf(ffn1_out), bf(ffn2_in), bf(ffn2_out)
        wl = w_in[l]
        w_row = _pad_cols(jnp.concatenate([wl[:, o_u:o_q], wl[:, o_k:o_qi], wl[:, o_ki:o_wi]], 1),
                          13 * LANES).astype(BF16)
        w_t = jnp.concatenate([wl[:, o_q:o_k], wl[:, o_qi:o_ki], wl[:, o_v:o_qi], wl[:, o_wi:o_ga]], 1).T.astype(BF16)
        w_gates = wl[:, o_ga:o_end].astype(BF16)
        w_all = _pad_cols(wl, 37 * LANES).astype(BF16)
        g_w, g_v, wba, wo = bf(glu_w), bf(glu_v), bf(w_branch_attn), bf(w_out)
        prep = _ssm_prep(ssm_log_dt[l], ssm_a_re[l], ssm_a_im[l], ssm_b_re[l], ssm_b_im[l],
                         ssm_c_re[l], ssm_c_im[l])
        ops = _ssm_operators(prep)
        eg = jnp.eye(n_groups, dtype=F32)
        c_step = jnp.einsum("cgjn,gh->cgnhj", jnp.stack([ssm_c_re[l], -ssm_c_im[l]], 0), eg
                            ).reshape(2 * n_groups * SSM_N, d_ssm).astype(BF16)

        xp = _ffn(xp, modp, npre, npost, f1_in, f1_out, k=0, bm=bm, blocks_per_batch=t_len // bm)
        u, k_p, v_p, kidx_p, kh, kidxb, v3, qt, qit, wt = _proj_prompt(
            xp, modp, npre, w_row, w_t, bm=bm, blocks_per_batch=t_len // bm, d_ssm=d_ssm, d_att=d_att)
        y_ssm, sp_re, sp_im = _ssm_prompt(u, ops, ssm_d[l], batch=batch, t_len=t_len)
        attn = _attn_prompt(qt, qit, wt, kh, kidxb, v3, batch=batch, t_len=t_len, topk=topk_p)
        xp = _mix(xp, modp, npre, npost, y_ssm, attn, g_w, g_v, wba, w_gates, wo,
                  bm=bm, blocks_per_batch=t_len // bm)
        xp = _ffn(xp, modp, npre, npost, f2_in, f2_out, k=2, bm=bm, blocks_per_batch=t_len // bm)
        new_p.append((k_p.reshape(batch, t_len, N_HEADS, HEAD_DIM), v_p.reshape(batch, t_len, N_HEADS, HEAD_DIM),
                      kidx_p.reshape(batch, t_len, IDX_DIM), sp_re, sp_im))

        xs = _ffn(xs, mods, npre, npost, f1_in, f1_out, k=0, bm=dec_batch, blocks_per_batch=1)
        pr = _proj_sample(xs, mods, npre, w_all)
        u_s, q_s, k_s, v_s = pr[:, o_u:o_q], pr[:, o_q:o_k], pr[:, o_k:o_v], pr[:, o_v:o_qi]
        qi_s, ki_s, wi_s = pr[:, o_qi:o_ki], pr[:, o_ki:o_wi], pr[:, o_wi:o_ga]
        y_ssm_s, ss_re, ss_im = _ssm_sample(u_s, state_ssm_re[l], state_ssm_im[l], ops, c_step, ssm_d[l])
        sel = _select_sample(page_table, qi_s.reshape(dec_batch, IDX_HEADS, IDX_DIM),
                             wi_s.reshape(dec_batch, IDX_HEADS, 1),
                             ki_s.reshape(dec_batch, 1, IDX_DIM), cache_idx_k, layer=l, topk=topk_s)
        slot, page = sel[:, 0, :].astype(I32), sel[:, 1, :].astype(I32)
        counts = sel[:, 2:4, 0].astype(I32)
        attn_s = _attend_sample(page_table, page, slot, counts, q_s.reshape(dec_batch, N_HEADS, HEAD_DIM),
                                k_s.reshape(dec_batch, N_HEADS, HEAD_DIM),
                                v_s.reshape(dec_batch, N_HEADS, HEAD_DIM), cache_k, cache_v,
                                layer=l, topk=topk_s)
        xs = _mix(xs, mods, npre, npost, y_ssm_s, attn_s.reshape(dec_batch, d_att), g_w, g_v, wba, w_gates, wo,
                  bm=dec_batch, blocks_per_batch=1)
        xs = _ffn(xs, mods, npre, npost, f2_in, f2_out, k=2, bm=dec_batch, blocks_per_batch=1)
        new_s.append((k_s.reshape(dec_batch, 1, N_HEADS, HEAD_DIM), v_s.reshape(dec_batch, 1, N_HEADS, HEAD_DIM),
                      ki_s.reshape(dec_batch, 1, IDX_DIM), ss_re, ss_im))

    stack = lambda states, i: jnp.stack([s[i] for s in states])
    return (xp.reshape(batch, t_len, d), xs.reshape(dec_batch, 1, d),
            stack(new_p, 0), stack(new_p, 1), stack(new_p, 2), stack(new_p, 3), stack(new_p, 4),
            stack(new_s, 0), stack(new_s, 1), stack(new_s, 2), stack(new_s, 3), stack(new_s, 4))
```

```python
import functools
import math

import jax
import jax.numpy as jnp
import numpy as np
from jax import lax
from jax.experimental import pallas as pl
from jax.experimental.pallas import tpu as pltpu

F32 = jnp.float32
BF16 = jnp.bfloat16
I32 = jnp.int32

EPS = 1e-6
SSM_GROUP = 16
SSM_N = 64
N_HEADS = 8
HEAD_DIM = 64
IDX_HEADS = 8
IDX_DIM = 64
INDEX_TOPK = 256
PAGE_SIZE = 128
IDX_SCALE = IDX_DIM ** -0.5 * IDX_HEADS ** -0.5
ATT_SCALE = HEAD_DIM ** -0.5

LANES = 128
SUBLANES = 8
VMEM_LIMIT_BYTES = 56 * 1024 * 1024
CHUNK = 8
OCTET = LANES // SSM_GROUP
QB = 256
NEG_BIG = float(np.finfo(np.float32).min)
FLOAT_MID_ITERS = 40
SEARCH_PROBES_PER_TEST = 4
MIN_NORMAL_KEY = 0x00800000


def _cparams(*sem):
    return pltpu.CompilerParams(dimension_semantics=sem, vmem_limit_bytes=VMEM_LIMIT_BYTES)


def _const_spec(shape):
    nd = len(shape)
    return pl.BlockSpec(shape, lambda *_: (0,) * nd, pipeline_mode=pl.Buffered(1))


def _rms(x, g):
    ms = jnp.mean(x * x, axis=-1, keepdims=True)
    return x * lax.rsqrt(ms + EPS) * g


def _dot(a, b):
    return jnp.dot(a, b, preferred_element_type=F32)


def _dot_nt(a, b):
    return lax.dot_general(a, b, (((1,), (1,)), ((), ())), preferred_element_type=F32)


def _mod_kernel(c_ref, w_ref, b_ref, o_ref):
    a = jax.nn.silu(c_ref[...]).astype(BF16)
    o_ref[...] = _dot(a, w_ref[...].astype(BF16)) + b_ref[...]


def _modulation(c_all, mod_w, mod_b):
    depth, d, n = mod_w.shape
    r = c_all.shape[0]
    tn = 1024
    return pl.pallas_call(
        _mod_kernel,
        grid=(depth, n // tn),
        in_specs=[
            pl.BlockSpec((r, d), lambda l, j: (0, 0)),
            pl.BlockSpec((None, d, tn), lambda l, j: (l, 0, j)),
            pl.BlockSpec((None, 1, tn), lambda l, j: (l, 0, j)),
        ],
        out_specs=pl.BlockSpec((None, r, tn), lambda l, j: (l, 0, j)),
        out_shape=jax.ShapeDtypeStruct((depth, r, n), F32),
        compiler_params=_cparams("parallel", "parallel"),
        name="modulation",
    )(c_all, mod_w, mod_b.reshape(depth, 1, n))


def _mod_spec(nb, d, rows_per_batch_block):
    return pl.BlockSpec((9, None, nb, d), lambda i: (0, i // rows_per_batch_block, 0, 0))


def _ffn_kernel(x_ref, mod_ref, npre_ref, npost_ref, win_ref, wout_ref, o_ref, h_ref, a_ref, *, k, dff, ck):
    x = x_ref[...]
    sh, sc, gt = mod_ref[3 * k], mod_ref[3 * k + 1], mod_ref[3 * k + 2]
    h = _rms(x, npre_ref[k:k + 1, :]) * (1.0 + sc) + sh
    h_ref[...] = h.astype(BF16)
    for c in range(dff // ck):
        hb = h_ref[...]
        g = _dot(hb, win_ref[:, c * ck:(c + 1) * ck])
        u = _dot(hb, win_ref[:, dff + c * ck:dff + (c + 1) * ck])
        a_ref[:, c * ck:(c + 1) * ck] = (jax.nn.silu(g) * u).astype(BF16)
    y = _dot(a_ref[...], wout_ref[...])
    o_ref[...] = x + 0.5 * gt * _rms(y, npost_ref[k:k + 1, :])


def _ffn(x, mod, npre, npost, w_in, w_out, *, k, bm, blocks_per_batch):
    m, d = x.shape
    dff = w_out.shape[0]
    nb = mod.shape[2]
    kern = functools.partial(_ffn_kernel, k=k, dff=dff, ck=256)
    return pl.pallas_call(
        kern,
        grid=(m // bm,),
        in_specs=[
            pl.BlockSpec((bm, d), lambda i: (i, 0)),
            _mod_spec(nb, d, blocks_per_batch),
            _const_spec(npre.shape),
            _const_spec(npost.shape),
            _const_spec(w_in.shape),
            _const_spec(w_out.shape),
        ],
        out_specs=pl.BlockSpec((bm, d), lambda i: (i, 0)),
        out_shape=jax.ShapeDtypeStruct((m, d), F32),
        scratch_shapes=[pltpu.VMEM((bm, d), BF16), pltpu.VMEM((bm, dff), BF16)],
        compiler_params=_cparams("parallel"),
        name=f"ffn{k}",
    )(x, mod, npre, npost, w_in, w_out)


def _proj_prompt_kernel(x_ref, mod_ref, npre_ref, wrow_ref, wt_ref,
                        u_ref, k_ref, v_ref, kidx_ref, kh_ref, kidxb_ref, v3_ref, qt_ref, qit_ref, wt_out_ref,
                        *, d_ssm, d_att):
    x = x_ref[...]
    sh, sc = mod_ref[3], mod_ref[4]
    h = (_rms(x, npre_ref[1:2, :]) * (1.0 + sc) + sh).astype(BF16)
    pr = _dot(h, wrow_ref[...])
    o = d_ssm
    u_ref[...] = pr[:, 0:o]
    k = pr[:, o:o + d_att]
    v = pr[:, o + d_att:o + 2 * d_att]
    kidx = pr[:, o + 2 * d_att:o + 2 * d_att + IDX_DIM]
    k_ref[...] = k
    v_ref[...] = v
    kidx_ref[...] = kidx
    kidxb_ref[...] = kidx.astype(BF16)
    for hh in range(N_HEADS):
        kh_ref[hh] = k[:, hh * HEAD_DIM:(hh + 1) * HEAD_DIM].astype(BF16)
    pt = _dot_nt(wt_ref[...], h)
    qt_ref[...] = (pt[0:d_att] * ATT_SCALE).astype(BF16)
    qit_ref[...] = pt[d_att:2 * d_att].astype(BF16)
    vt = pt[2 * d_att:3 * d_att].astype(BF16)
    for jj in range(v3_ref.shape[0]):
        v3_ref[jj] = vt[:, jj * QB:(jj + 1) * QB]
    wt_out_ref[...] = pt[3 * d_att:3 * d_att + IDX_HEADS] * IDX_SCALE


def _proj_prompt(x, mod, npre, w_row, w_t, *, bm, blocks_per_batch, d_ssm, d_att):
    m, d = x.shape
    row = lambda w: pl.BlockSpec((bm, w), lambda i: (i, 0))
    col = lambda r: pl.BlockSpec((r, bm), lambda i: (0, i))
    kern = functools.partial(_proj_prompt_kernel, d_ssm=d_ssm, d_att=d_att)
    return pl.pallas_call(
        kern,
        grid=(m // bm,),
        in_specs=[
            pl.BlockSpec((bm, d), lambda i: (i, 0)),
            _mod_spec(1, d, blocks_per_batch),
            _const_spec(npre.shape),
            _const_spec(w_row.shape),
            _const_spec(w_t.shape),
        ],
        out_specs=[
            row(d_ssm), row(d_att), row(d_att), row(IDX_DIM),
            pl.BlockSpec((N_HEADS, bm, HEAD_DIM), lambda i: (0, i, 0)),
            row(IDX_DIM),
            pl.BlockSpec((bm // QB, d_att, QB), lambda i: (i, 0, 0)),
            col(d_att), col(d_att), col(IDX_HEADS),
        ],
        out_shape=[
            jax.ShapeDtypeStruct((m, d_ssm), F32),
            jax.ShapeDtypeStruct((m, d_att), F32),
            jax.ShapeDtypeStruct((m, d_att), F32),
            jax.ShapeDtypeStruct((m, IDX_DIM), F32),
            jax.ShapeDtypeStruct((N_HEADS, m, HEAD_DIM), BF16),
            jax.ShapeDtypeStruct((m, IDX_DIM), BF16),
            jax.ShapeDtypeStruct((m // QB, d_att, QB), BF16),
            jax.ShapeDtypeStruct((d_att, m), BF16),
            jax.ShapeDtypeStruct((d_att, m), BF16),
            jax.ShapeDtypeStruct((IDX_HEADS, m), F32),
        ],
        compiler_params=_cparams("parallel"),
        name="proj_prompt",
    )(x, mod, npre, w_row, w_t)


def _proj_sample_kernel(x_ref, mod_ref, npre_ref, w_ref, o_ref):
    x = x_ref[...]
    sh, sc = mod_ref[3], mod_ref[4]
    h = (_rms(x, npre_ref[1:2, :]) * (1.0 + sc) + sh).astype(BF16)
    o_ref[...] = _dot(h, w_ref[...])


def _proj_sample(x, mod, npre, w):
    m, d = x.shape
    n = w.shape[1]
    return pl.pallas_call(
        _proj_sample_kernel,
        grid=(1,),
        in_specs=[
            pl.BlockSpec((m, d), lambda i: (0, 0)),
            _mod_spec(m, d, 1),
            _const_spec(npre.shape),
            _const_spec(w.shape),
        ],
        out_specs=pl.BlockSpec((m, n), lambda i: (0, 0)),
        out_shape=jax.ShapeDtypeStruct((m, n), F32),
        compiler_params=_cparams("arbitrary"),
        name="proj_sample",
    )(x, mod, npre, w)


def _ssm_prep_kernel(ldt_ref, ara_ref, aia_ref, arb_ref, aib_ref, bre_ref, bim_ref, btre_ref, btim_ref,
                     cre_ref, cim_ref,
                     kk_ref, wre_ref, wim_ref, pre_ref, pim_ref, al_ref, ab_ref, bbre_ref, bbim_ref):
    dt = jnp.exp(ldt_ref[...])

    def powers(ar, ai, k):
        mag = jnp.exp(dt * ar * k)
        ph = dt * ai * k
        return mag * jnp.cos(ph), mag * jnp.sin(ph)

    def zoh(ar, ai, b_re, b_im):
        abr, abi = powers(ar, ai, 1.0)
        den = ar * ar + ai * ai
        z_re = ((abr - 1.0) * ar + abi * ai) / den
        z_im = (abi * ar - (abr - 1.0) * ai) / den
        return z_re * b_re - z_im * b_im, z_re * b_im + z_im * b_re

    ar_b, ai_b = arb_ref[...], aib_ref[...]
    bb_re, bb_im = zoh(ar_b, ai_b, bre_ref[...], bim_ref[...])
    bbre_ref[...] = bb_re
    bbim_ref[...] = bb_im
    for tau in range(CHUNK):
        pr, pi = powers(ar_b, ai_b, float(CHUNK - 1 - tau))
        wre_ref[tau] = pr * bb_re - pi * bb_im
        wim_ref[tau] = pr * bb_im + pi * bb_re

    ar_a, ai_a = ara_ref[...], aia_ref[...]
    c_re, c_im = cre_ref[...], cim_ref[...]
    bt_re, bt_im = zoh(ar_a, ai_a, btre_ref[...], btim_ref[...])
    lr, li = powers(ar_a, ai_a, float(CHUNK))
    al_ref[0] = lr
    al_ref[1] = li
    r1, i1 = powers(ar_a, ai_a, 1.0)
    ab_ref[0] = r1
    ab_ref[1] = i1
    ck_re, ck_im = [], []
    for k in range(CHUNK):
        pr, pi = powers(ar_a, ai_a, float(k))
        ck_re.append(c_re * pr - c_im * pi)
        ck_im.append(c_re * pi + c_im * pr)
        pr1, pi1 = powers(ar_a, ai_a, float(k + 1))
        pre_ref[k] = c_re * pr1 - c_im * pi1
        pim_ref[k] = -(c_re * pi1 + c_im * pr1)
    bnt = functools.partial(lax.dot_general, dimension_numbers=(((2,), (2,)), ((0,), (0,))),
                            precision=lax.Precision.HIGHEST, preferred_element_type=F32)
    kk_ref[...] = (bnt(jnp.concatenate(ck_re, axis=1), bt_re) - bnt(jnp.concatenate(ck_im, axis=1), bt_im))


def _ssm_prep(log_dt, a_re, a_im, b_re, b_im, c_re, c_im):
    g, n = a_re.shape
    j = SSM_GROUP
    shp = lambda *s: jax.ShapeDtypeStruct(s, F32)
    return pl.pallas_call(
        _ssm_prep_kernel,
        out_shape=[shp(g, CHUNK * j, j), shp(CHUNK, g, n, j), shp(CHUNK, g, n, j),
                   shp(CHUNK, g, j, n), shp(CHUNK, g, j, n), shp(2, g, 1, n), shp(2, g, 1, n),
                   shp(g, n, j), shp(g, n, j)],
        compiler_params=pltpu.CompilerParams(vmem_limit_bytes=VMEM_LIMIT_BYTES),
        name="ssm_prep",
    )(log_dt.reshape(g, 1, 1), a_re.reshape(g, 1, n), a_im.reshape(g, 1, n),
      a_re.reshape(g, n, 1), a_im.reshape(g, n, 1), b_re, b_im,
      b_re.transpose(0, 2, 1), b_im.transpose(0, 2, 1), c_re, c_im)


def _ssm_operators(prep):
    kk, w_re, w_im, p_re, p_im, al, ab, bb_re, bb_im = prep
    g = kk.shape[0]
    no, j, n = g // OCTET, SSM_GROUP, SSM_N
    eye = jnp.eye(OCTET, dtype=F32)
    kk = kk.reshape(no, OCTET, CHUNK, j, j)
    lag = np.arange(CHUNK)[None, :] - np.arange(CHUNK)[:, None]
    kt = jnp.where((lag >= 0)[None, None, :, :, None, None], kk[:, :, np.clip(lag, 0, None)], 0.0)
    m_op = jnp.einsum("ogstjk,gh->osgkthj", kt, eye).reshape(no, CHUNK * LANES, CHUNK * LANES)
    w = jnp.stack([w_re, w_im], 0).reshape(2, CHUNK, no, OCTET, n, j)
    w_op = jnp.einsum("ctognk,gh->otgkchn", w, eye).reshape(no, CHUNK * LANES, 2 * OCTET * n)
    p = jnp.stack([p_re, p_im], 0).reshape(2, CHUNK, no, OCTET, j, n)
    p_op = jnp.einsum("ctogjn,gh->ocgnthj", p, eye).reshape(no, 2 * OCTET * n, CHUNK * LANES)
    a_chunk = al.reshape(2, no, OCTET * n).transpose(1, 0, 2)
    eg = jnp.eye(g, dtype=F32)
    b_step = jnp.einsum("cgnk,gh->gkchn", jnp.stack([bb_re, bb_im], 0), eg).reshape(g * j, 2 * g * n)
    return dict(m=m_op.astype(BF16), w=w_op.astype(BF16), p=p_op.astype(BF16), a_chunk=a_chunk,
                b_step=b_step, abar=ab.reshape(2, g * n))


def _ssm_prompt_kernel(u_ref, m_ref, w_ref, p_ref, al_ref, d_ref, y_ref, sfin_ref,
                       uo_ref, v_ref, sc_ref, *, r):
    half = sc_ref.shape[1] // 2
    for tau in range(CHUNK):
        uo_ref[:, tau * LANES:(tau + 1) * LANES] = u_ref[pl.ds(tau, r, stride=CHUNK), :].astype(BF16)
    uo = uo_ref[...]
    v_ref[...] = _dot(uo, w_ref[...])
    a_r, a_i = al_ref[0:1, :], al_ref[1:2, :]

    def step(c, carry):
        s_r, s_i = carry
        sc_ref[pl.ds(c, 1), 0:half] = s_r
        sc_ref[pl.ds(c, 1), half:2 * half] = s_i
        v = v_ref[pl.ds(c, 1), :]
        return (a_r * s_r - a_i * s_i + v[:, 0:half], a_r * s_i + a_i * s_r + v[:, half:2 * half])

    zero = jnp.zeros((1, half), F32)
    s_r, s_i = lax.fori_loop(0, r, step, (zero, zero))
    sfin_ref[0:1, :] = s_r
    sfin_ref[1:2, :] = s_i
    y = _dot(uo, m_ref[...]) + _dot(sc_ref[...].astype(BF16), p_ref[...])
    d = d_ref[...]
    for t in range(CHUNK):
        rows = pl.ds(t, r, stride=CHUNK)
        y_ref[rows, :] = y[:, t * LANES:(t + 1) * LANES] + d * u_ref[rows, :]


def _ssm_prompt(u, ops, d_skip, *, batch, t_len):
    m, d_ssm = u.shape
    no = d_ssm // LANES
    r = t_len // CHUNK
    kw = CHUNK * LANES
    sw = ops["w"].shape[2]
    op_spec = lambda a, b: pl.BlockSpec((None, a, b), lambda o, bb: (o, 0, 0))
    y, sfin = pl.pallas_call(
        functools.partial(_ssm_prompt_kernel, r=r),
        grid=(no, batch),
        in_specs=[
            pl.BlockSpec((t_len, LANES), lambda o, bb: (bb, o)),
            op_spec(kw, kw), op_spec(kw, sw), op_spec(sw, kw),
            pl.BlockSpec((None, 2, sw // 2), lambda o, bb: (o, 0, 0)),
            pl.BlockSpec((1, LANES), lambda o, bb: (0, o)),
        ],
        out_specs=[
            pl.BlockSpec((t_len, LANES), lambda o, bb: (bb, o)),
            pl.BlockSpec((None, None, 2, sw // 2), lambda o, bb: (bb, o, 0, 0)),
        ],
        out_shape=[jax.ShapeDtypeStruct((m, d_ssm), F32),
                   jax.ShapeDtypeStruct((batch, no, 2, sw // 2), F32)],
        scratch_shapes=[pltpu.VMEM((r, kw), BF16), pltpu.VMEM((r, sw), F32), pltpu.VMEM((r, sw), F32)],
        compiler_params=_cparams("parallel", "parallel"),
        name="ssm_prompt",
    )(u, ops["m"], ops["w"], ops["p"], ops["a_chunk"], d_skip.reshape(1, d_ssm))
    s = sfin.reshape(batch, no, 2, OCTET, SSM_N).transpose(2, 0, 1, 3, 4).reshape(2, batch, no * OCTET, SSM_N)
    return y, s[0], s[1]


def _ssm_sample_kernel(u_ref, h_ref, ab_ref, bstep_ref, cstep_ref, d_ref, y_ref, s_ref):
    u = u_ref[...]
    half = h_ref.shape[2]
    bu = jnp.dot(u, bstep_ref[...], precision=lax.Precision.HIGHEST, preferred_element_type=F32)
    a_r, a_i = ab_ref[0:1, :], ab_ref[1:2, :]
    h_r, h_i = h_ref[0], h_ref[1]
    s_r = a_r * h_r - a_i * h_i + bu[:, 0:half]
    s_i = a_r * h_i + a_i * h_r + bu[:, half:2 * half]
    s_ref[0] = s_r
    s_ref[1] = s_i
    s = jnp.concatenate([s_r, s_i], axis=1).astype(BF16)
    y_ref[...] = _dot(s, cstep_ref[...]) + d_ref[...] * u


def _ssm_sample(u, h_re, h_im, ops, c_step, d_skip):
    bsz, d_ssm = u.shape
    gn = h_re.shape[1] * h_re.shape[2]
    h = jnp.stack([h_re.reshape(bsz, gn), h_im.reshape(bsz, gn)], 0)
    y, s = pl.pallas_call(
        _ssm_sample_kernel,
        out_shape=[jax.ShapeDtypeStruct((bsz, d_ssm), F32), jax.ShapeDtypeStruct((2, bsz, gn), F32)],
        compiler_params=pltpu.CompilerParams(vmem_limit_bytes=VMEM_LIMIT_BYTES),
        name="ssm_sample",
    )(u, h, ops["abar"], ops["b_step"], c_step, d_skip.reshape(1, d_ssm))
    return y, s[0].reshape(h_re.shape), s[1].reshape(h_im.shape)


def _f2key(x):
    b = lax.bitcast_convert_type(x, I32)
    return b ^ ((b >> 31) & 0x7FFFFFFF)


def _key2f(k):
    return lax.bitcast_convert_type(k ^ ((k >> 31) & 0x7FFFFFFF), F32)


def _search_init(mn, mx, n_valid, need):
    z = jnp.zeros_like(n_valid)
    return (_f2key(mn), _f2key(mx) + 1, n_valid, z, jnp.full(mn.shape, NEG_BIG, F32),
            jnp.where(need, 0, 1).astype(I32), z)


def _search_probe(state, it, topk):
    lo, hi, c_lo, c_hi = state[:4]
    lo_f, hi_f = _key2f(lo), _key2f(hi)
    frac = (c_lo.astype(F32) - (topk + 0.5)) / jnp.maximum(c_lo - c_hi, 1).astype(F32)
    frac = jnp.where(it % 3 == 2, 0.5, frac)
    cand = _f2key(lo_f + (hi_f - lo_f) * frac)
    mid_k = (lo >> 1) + (hi >> 1) + (lo & hi & 1)
    inside = (cand > lo) & (cand < hi) & (it < FLOAT_MID_ITERS)
    probe = jnp.where(inside, cand, mid_k)
    zero_k = jnp.where(it == 0, 0, MIN_NORMAL_KEY)
    return jnp.where((it < 2) & (zero_k > lo) & (zero_k < hi), zero_k, probe)


def _search_update(state, mid, cnt, topk):
    lo, hi, c_lo, c_hi, thr, done, tie = state
    hit = cnt == topk
    up, dn = cnt > topk, cnt < topk
    lo_n, hi_n = jnp.where(up, mid, lo), jnp.where(dn, mid, hi)
    adj = ((hi_n == lo_n + 1) | ((lo_n == 0) & (hi_n == MIN_NORMAL_KEY))) & jnp.logical_not(hit)
    fin = hit | adj
    thr_n = jnp.where(hit, _key2f(mid), _key2f(lo_n))
    act = done == 0
    lo = jnp.where(act, lo_n, lo)
    hi = jnp.where(act, hi_n, hi)
    c_lo = jnp.where(act & up, cnt, c_lo)
    c_hi = jnp.where(act & dn, cnt, c_hi)
    thr = jnp.where(act & fin, thr_n, thr)
    tie = jnp.where(act & adj, 1, tie)
    done = jnp.where(act & fin, 1, done)
    return lo, hi, c_lo, c_hi, thr, done, tie


def _search(count_ge, init, topk):
    def pending(state, it):
        return jnp.logical_and(jnp.min(state[5].astype(F32)) == 0.0, it < 96)

    def probe(u, carry):
        it, state = carry
        mid = _search_probe(state, it, topk)
        return it + 1, _search_update(state, mid, count_ge(_key2f(mid)), topk)

    def body(carry):
        it, state = lax.fori_loop(0, SEARCH_PROBES_PER_TEST, probe, carry[1:])
        return pending(state, it), it, state

    _, _, state = lax.while_loop(lambda c: c[0], body, (pending(init, 0), jnp.int32(0), init))
    return state[4], state[6]


def _attn_prompt_kernel(qt_ref, qit_ref, wt_ref, kh_ref, kidx_ref, v3_ref, o_ref, sc_ref, lg_ref, acc_ref, *, topk):
    i = pl.program_id(1)
    nk = i + 1
    qpos = i * QB + lax.broadcasted_iota(I32, (QB, QB), 1)
    krow = lax.broadcasted_iota(I32, (QB, QB), 0)
    fold = lambda x: x.reshape(QB // SUBLANES, SUBLANES, QB)

    tile_rows = lambda j: pl.ds(pl.multiple_of(j * QB, QB), QB)
    rep = lambda x: jnp.broadcast_to(x, (SUBLANES, QB))

    def score_tile(j, carry):
        mn, mx = carry
        kx = kidx_ref[tile_rows(j), :]
        acc = jnp.zeros((QB, QB), F32)
        for h in range(IDX_HEADS):
            s = _dot(kx, qit_ref[h * IDX_DIM:(h + 1) * IDX_DIM, :])
            acc = acc + jnp.maximum(s, 0.0) * wt_ref[h:h + 1, :]
        valid = (j * QB + krow) <= qpos
        sc = jnp.where(valid, acc, -jnp.inf)
        sc_ref[j] = sc
        mx = jnp.maximum(mx, jnp.max(fold(sc), axis=0))
        mn = jnp.minimum(mn, jnp.min(fold(jnp.where(valid, acc, jnp.inf)), axis=0))
        return mn, mx

    mn, mx = lax.fori_loop(0, nk, score_tile,
                           (jnp.full((SUBLANES, QB), jnp.inf, F32), jnp.full((SUBLANES, QB), -jnp.inf, F32)))
    mn = rep(jnp.min(mn, axis=0, keepdims=True))
    mx = rep(jnp.max(mx, axis=0, keepdims=True))

    def count_ge(t):
        t1 = t[0:1, :]

        def body(j, c):
            return c + jnp.sum(fold((sc_ref[j] >= t1).astype(I32)), axis=0)

        c = lax.fori_loop(0, nk, body, jnp.zeros((SUBLANES, QB), I32))
        return rep(jnp.sum(c, axis=0, keepdims=True))

    n_valid = qpos[0:SUBLANES, :] + 1
    thr, tie = _search(count_ge, _search_init(mn, mx, n_valid, n_valid > topk), topk)

    @pl.when(jnp.max(tie) > 0)
    def _():
        tri = (lax.broadcasted_iota(I32, (QB, QB), 0) >= lax.broadcasted_iota(I32, (QB, QB), 1)).astype(BF16)
        thr1, tie1 = thr[0:1, :], tie[0:1, :] > 0

        def gt_tile(j, c):
            return c + jnp.sum((sc_ref[j] > thr1).astype(I32), axis=0, keepdims=True)

        quota = topk - lax.fori_loop(0, nk, gt_tile, jnp.zeros((1, QB), I32))

        def tie_tile(j, before):
            x = sc_ref[j]
            t = (x == thr1) & tie1
            rank = before + _dot(tri, t.astype(BF16)).astype(I32)
            sc_ref[j] = jnp.where(t & (rank > quota), -jnp.inf, x)
            return before + jnp.sum(t.astype(I32), axis=0, keepdims=True)

        lax.fori_loop(0, nk, tie_tile, jnp.zeros((1, QB), I32))

    thr1 = thr[0:1, :]
    heads = range(N_HEADS)
    hslice = lambda h: slice(h * HEAD_DIM, (h + 1) * HEAD_DIM)

    def logit_tile(j, ms):
        bias = jnp.where(sc_ref[j] >= thr1, 0.0, -jnp.inf)
        rows = tile_rows(j)
        out = []
        for h in heads:
            lg = _dot(kh_ref[h, rows, :], qt_ref[hslice(h), :]) + bias
            lg_ref[h, j] = lg
            out.append(jnp.maximum(ms[h], jnp.max(fold(lg), axis=0)))
        return tuple(out)

    ms = lax.fori_loop(0, nk, logit_tile, tuple(jnp.full((SUBLANES, QB), -jnp.inf, F32) for _ in heads))
    m1 = [jnp.max(m, axis=0, keepdims=True) for m in ms]
    acc_ref[...] = jnp.zeros(acc_ref.shape, F32)

    def pv_tile(j, ls):
        out = []
        for h in heads:
            p = jnp.exp(lg_ref[h, j] - m1[h])
            acc_ref[h] += _dot(v3_ref[j, hslice(h), :], p.astype(BF16))
            out.append(ls[h] + jnp.sum(fold(p), axis=0))
        return tuple(out)

    ls = lax.fori_loop(0, nk, pv_tile, tuple(jnp.zeros((SUBLANES, QB), F32) for _ in heads))
    outs = [acc_ref[h] / jnp.sum(ls[h], axis=0, keepdims=True) for h in heads]
    o_ref[...] = jnp.transpose(jnp.concatenate(outs, axis=0)).astype(BF16)


def _attn_prompt(qt, qit, wt, kh, kidxb, v3, *, batch, t_len, topk):
    d_att, m = qt.shape
    nq = t_len // QB
    return pl.pallas_call(
        functools.partial(_attn_prompt_kernel, topk=topk),
        grid=(batch, nq),
        in_specs=[
            pl.BlockSpec((d_att, QB), lambda b, i: (0, b * nq + i)),
            pl.BlockSpec((d_att, QB), lambda b, i: (0, b * nq + i)),
            pl.BlockSpec((IDX_HEADS, QB), lambda b, i: (0, b * nq + i)),
            pl.BlockSpec((N_HEADS, t_len, HEAD_DIM), lambda b, i: (0, b, 0)),
            pl.BlockSpec((t_len, IDX_DIM), lambda b, i: (b, 0)),
            pl.BlockSpec((nq, d_att, QB), lambda b, i: (b, 0, 0)),
        ],
        out_specs=pl.BlockSpec((QB, d_att), lambda b, i: (b * nq + i, 0)),
        out_shape=jax.ShapeDtypeStruct((m, d_att), BF16),
        scratch_shapes=[pltpu.VMEM((nq, QB, QB), F32), pltpu.VMEM((N_HEADS, nq, QB, QB), F32),
                        pltpu.VMEM((N_HEADS, HEAD_DIM, QB), F32)],
        compiler_params=_cparams("parallel", "arbitrary"),
        name="attn_prompt",
    )(qt, qit, wt, kh, kidxb, v3)


def _attn_sample_kernel(pt_ref, qi_ref, w_ref, kin_ref, q_ref, kn_ref, vn_ref, ci_ref, ck_ref, cv_ref, o_ref,
                        ibuf, kbuf, vbuf, qib, qb, sc_ref, lg_ref, sem_i, sem_k, sem_v,
                        *, layer, n_pages, topk):
    b = pl.program_id(0)

    def copies(src_ref, buf, sem, p, page):
        return pltpu.make_async_copy(src_ref.at[layer, page], buf.at[p], sem)

    streams = ((ci_ref, ibuf, sem_i), (ck_ref, kbuf, sem_k), (cv_ref, vbuf, sem_v))
    for src_ref, buf, sem in streams:
        def start(p, _, src_ref=src_ref, buf=buf, sem=sem):
            copies(src_ref, buf, sem, p, pt_ref[b, p]).start()
            return 0
        lax.fori_loop(0, n_pages, start, 0)

    def wait_all(src_ref, buf, sem):
        def wait(p, _):
            copies(src_ref, buf, sem, p, 0).wait()
            return 0
        lax.fori_loop(0, n_pages, wait, 0)

    for h in range(IDX_HEADS):
        qib[h] = jnp.broadcast_to(qi_ref[h], (IDX_DIM, PAGE_SIZE))
    for h in range(N_HEADS):
        qb[h] = jnp.broadcast_to(q_ref[h] * ATT_SCALE, (HEAD_DIM, PAGE_SIZE))
    w = w_ref[...] * IDX_SCALE
    dsum = lambda x: jnp.sum(x, axis=0, keepdims=True)

    wait_all(*streams[0])

    def idx_page(p, _):
        page = ibuf[p]
        acc = jnp.zeros((1, PAGE_SIZE), F32)
        for h in range(IDX_HEADS):
            acc = acc + jnp.maximum(dsum(page * qib[h]), 0.0) * w[h:h + 1, :]
        sc_ref[pl.ds(p, 1), :] = acc
        return 0

    lax.fori_loop(0, n_pages, idx_page, 0)
    kin = kin_ref[...]
    s_new = jnp.zeros((1, 1), F32)
    for h in range(IDX_HEADS):
        s_new = s_new + jnp.maximum(dsum(qi_ref[h] * kin), 0.0) * w[h:h + 1, :]
    sc = sc_ref[...]

    def total(x, op=jnp.sum):
        return op(op(x, axis=1, keepdims=True), axis=0, keepdims=True)

    mx = jnp.maximum(total(sc, jnp.max), s_new)
    mn = jnp.minimum(total(sc, jnp.min), s_new)

    def count_ge(t):
        return total((sc >= t).astype(I32)) + (s_new >= t).astype(I32)

    n_keys = jnp.full((1, 1), n_pages * PAGE_SIZE + 1, I32)
    thr, tie = _search(count_ge, _search_init(mn, mx, n_keys, n_keys > topk), topk)

    upper = (lax.broadcasted_iota(I32, (PAGE_SIZE, PAGE_SIZE), 0)
             <= lax.broadcasted_iota(I32, (PAGE_SIZE, PAGE_SIZE), 1)).astype(BF16)
    lower = (lax.broadcasted_iota(I32, (n_pages, n_pages), 1)
             < lax.broadcasted_iota(I32, (n_pages, n_pages), 0)).astype(BF16)

    def flat_rank(mask):
        mb = mask.astype(BF16)
        incl = _dot(mb, upper)
        before = jnp.sum(_dot(lower, mb), axis=1, keepdims=True)
        return (incl + before).astype(I32)

    gt = sc > thr
    tied = tie > 0
    is_tie = (sc == thr) & tied
    quota = jnp.where(tied, topk - total(gt.astype(I32)) - (s_new > thr).astype(I32), topk)
    sel = gt | ((sc == thr) & (flat_rank(is_tie) <= quota))
    n_tie_past = total(is_tie.astype(I32))
    new_sel = (s_new > thr) | ((s_new == thr) & (n_tie_past < quota))

    wait_all(*streams[1])
    for h in range(N_HEADS):
        q_h = qb[h]

        def k_page(p, _, h=h, q_h=q_h):
            lg_ref[h, pl.ds(p, 1), :] = dsum(kbuf[p, h] * q_h)
            return 0

        lax.fori_loop(0, n_pages, k_page, 0, unroll=4)

    p_new = []
    for h in range(N_HEADS):
        lg = jnp.where(sel, lg_ref[h], -jnp.inf)
        lg_new = jnp.where(new_sel, dsum(q_ref[h] * ATT_SCALE * kn_ref[h]), -jnp.inf)
        m = jnp.maximum(total(lg, jnp.max), lg_new)
        p = jnp.exp(lg - m)
        e_new = jnp.exp(lg_new - m)
        denom = total(p) + e_new
        lg_ref[h] = p / denom
        p_new.append(e_new / denom)

    wait_all(*streams[2])
    for h in range(N_HEADS):
        def v_page(p, acc, h=h):
            return acc + vbuf[p, h] * lg_ref[h, pl.ds(p, 1), :]

        acc = lax.fori_loop(0, n_pages, v_page, jnp.zeros((HEAD_DIM, PAGE_SIZE), F32), unroll=4)
        o_ref[h] = jnp.sum(acc, axis=1, keepdims=True) + p_new[h] * vn_ref[h]


def _attn_sample(page_table, q_idx, w_idx, k_idx_new, q, k_new, v_new, cache_idx_t, cache_k_t, cache_v_t,
                 *, layer, topk):
    bsz, n_pages = page_table.shape
    col = lambda r: pl.BlockSpec((None, r, 1), lambda b, pt: (b, 0, 0))
    hcol = lambda r: pl.BlockSpec((None, N_HEADS, r, 1), lambda b, pt: (b, 0, 0, 0))
    any_spec = pl.BlockSpec(memory_space=pl.ANY)
    return pl.pallas_call(
        functools.partial(_attn_sample_kernel, layer=layer, n_pages=n_pages, topk=topk),
        grid_spec=pltpu.PrefetchScalarGridSpec(
            num_scalar_prefetch=1,
            grid=(bsz,),
            in_specs=[hcol(IDX_DIM), col(IDX_HEADS), col(IDX_DIM), hcol(HEAD_DIM), hcol(HEAD_DIM), hcol(HEAD_DIM),
                      any_spec, any_spec, any_spec],
            out_specs=hcol(HEAD_DIM),
            scratch_shapes=[pltpu.VMEM((n_pages, IDX_DIM, PAGE_SIZE), F32),
                            pltpu.VMEM((n_pages, N_HEADS, HEAD_DIM, PAGE_SIZE), F32),
                            pltpu.VMEM((n_pages, N_HEADS, HEAD_DIM, PAGE_SIZE), F32),
                            pltpu.VMEM((IDX_HEADS, IDX_DIM, PAGE_SIZE), F32),
                            pltpu.VMEM((N_HEADS, HEAD_DIM, PAGE_SIZE), F32),
                            pltpu.VMEM((n_pages, PAGE_SIZE), F32),
                            pltpu.VMEM((N_HEADS, n_pages, PAGE_SIZE), F32),
                            pltpu.SemaphoreType.DMA(()), pltpu.SemaphoreType.DMA(()), pltpu.SemaphoreType.DMA(())],
        ),
        out_shape=jax.ShapeDtypeStruct((bsz, N_HEADS, HEAD_DIM, 1), F32),
        compiler_params=_cparams("arbitrary"),
        name="attn_sample",
    )(page_table, q_idx, w_idx, k_idx_new, q, k_new, v_new, cache_idx_t, cache_k_t, cache_v_t)


def _mix_kernel(x_ref, mod_ref, npre_ref, npost_ref, ys_ref, at_ref, gw_ref, gv_ref, wba_ref, wg_ref, wo_ref,
                o_ref):
    x = x_ref[...]
    d = x.shape[1]
    sh, sc, gt = mod_ref[3], mod_ref[4], mod_ref[5]
    h = (_rms(x, npre_ref[1:2, :]) * (1.0 + sc) + sh).astype(BF16)
    gates = _dot(h, wg_ref[...])
    ys = ys_ref[...].astype(BF16)
    y_a = _dot(ys, gw_ref[...]) * jax.nn.sigmoid(_dot(ys, gv_ref[...]))
    y_b = _dot(at_ref[...].astype(BF16), wba_ref[...])
    mixed = jax.nn.sigmoid(gates[:, 0:d]) * y_a + jax.nn.sigmoid(gates[:, d:2 * d]) * y_b
    y = _dot(mixed.astype(BF16), wo_ref[...])
    o_ref[...] = x + gt * _rms(y, npost_ref[1:2, :])


def _mix(x, mod, npre, npost, y_ssm, attn, glu_w, glu_v, wba, w_gates, w_out, *, bm, blocks_per_batch):
    m, d = x.shape
    nb = mod.shape[2]
    return pl.pallas_call(
        _mix_kernel,
        grid=(m // bm,),
        in_specs=[
            pl.BlockSpec((bm, d), lambda i: (i, 0)),
            _mod_spec(nb, d, blocks_per_batch),
            _const_spec(npre.shape), _const_spec(npost.shape),
            pl.BlockSpec((bm, y_ssm.shape[1]), lambda i: (i, 0)),
            pl.BlockSpec((bm, attn.shape[1]), lambda i: (i, 0)),
            _const_spec(glu_w.shape), _const_spec(glu_v.shape), _const_spec(wba.shape),
            _const_spec(w_gates.shape), _const_spec(w_out.shape),
        ],
        out_specs=pl.BlockSpec((bm, d), lambda i: (i, 0)),
        out_shape=jax.ShapeDtypeStruct((m, d), F32),
        compiler_params=_cparams("parallel"),
        name="mix",
    )(x, mod, npre, npost, y_ssm, attn, glu_w, glu_v, wba, w_gates, w_out)


def _pad_cols(w, n):
    return jnp.pad(w, ((0, 0), (0, n - w.shape[1])))


def kernel(x_prompt, x_sample, cache_k, cache_v, cache_idx_k, state_ssm_re, state_ssm_im, page_table,
           c_prompt, c_sample, mod_w, mod_b, norm_pre, norm_post, ffn1_in, ffn1_out, w_in,
           ssm_log_dt, ssm_a_re, ssm_a_im, ssm_b_re, ssm_b_im, ssm_c_re, ssm_c_im, ssm_d,
           glu_w, glu_v, w_branch_attn, w_out, ffn2_in, ffn2_out):
    batch, t_len, d = x_prompt.shape
    dec_batch, dec_seq, _ = x_sample.shape
    depth = mod_w.shape[0]
    d_ssm = ssm_d.shape[1]
    d_att = N_HEADS * HEAD_DIM
    n_groups = d_ssm // SSM_GROUP
    assert dec_seq == 1 and t_len % QB == 0 and d_ssm % LANES == 0
    m = batch * t_len
    bm = 512 if m % 512 == 0 else QB
    topk_p = min(INDEX_TOPK, t_len // 4)
    n_pages = page_table.shape[1]
    topk_s = min(INDEX_TOPK, (n_pages * PAGE_SIZE + dec_seq) // 4)

    mod = _modulation(jnp.concatenate([c_prompt, c_sample], 0), mod_w, mod_b).reshape(depth, -1, 9, d)

    widths = (d_ssm, d_att, d_att, d_att, IDX_HEADS * IDX_DIM, IDX_DIM, IDX_HEADS, d, d)
    off = np.concatenate([[0], np.cumsum(widths)])
    o_u, o_q, o_k, o_v, o_qi, o_ki, o_wi, o_ga, o_gb, o_end = (int(v) for v in off)

    cache_k_t = cache_k.transpose(0, 1, 3, 4, 2)
    cache_v_t = cache_v.transpose(0, 1, 3, 4, 2)
    cache_idx_t = cache_idx_k.transpose(0, 1, 3, 2)

    xp = x_prompt.reshape(m, d)
    xs = x_sample.reshape(dec_batch, d)
    new_p, new_s = [], []
    for l in range(depth):
        bf = lambda w: w[l].astype(BF16)
        modp = mod[l, :batch].transpose(1, 0, 2).reshape(9, batch, 1, d)
        mods = mod[l, batch:].transpose(1, 0, 2).reshape(9, 1, dec_batch, d)
        npre, npost = norm_pre[l], norm_post[l]
        f1_in, f1_out, f2_in, f2_out = bf(ffn1_in), bf(ffn1_out), bf(ffn2_in), bf(ffn2_out)
        wl = w_in[l]
        w_row = _pad_cols(jnp.concatenate([wl[:, o_u:o_q], wl[:, o_k:o_qi], wl[:, o_ki:o_wi]], 1),
                          13 * LANES).astype(BF16)
        w_t = jnp.concatenate([wl[:, o_q:o_k], wl[:, o_qi:o_ki], wl[:, o_v:o_qi], wl[:, o_wi:o_ga]], 1).T.astype(BF16)
        w_gates = wl[:, o_ga:o_end].astype(BF16)
        w_all = _pad_cols(wl, 37 * LANES).astype(BF16)
        g_w, g_v, wba, wo = bf(glu_w), bf(glu_v), bf(w_branch_attn), bf(w_out)
        prep = _ssm_prep(ssm_log_dt[l], ssm_a_re[l], ssm_a_im[l], ssm_b_re[l], ssm_b_im[l],
                         ssm_c_re[l], ssm_c_im[l])
        ops = _ssm_operators(prep)
        eg = jnp.eye(n_groups, dtype=F32)
        c_step = jnp.einsum("cgjn,gh->cgnhj", jnp.stack([ssm_c_re[l], -ssm_c_im[l]], 0), eg
                            ).reshape(2 * n_groups * SSM_N, d_ssm).astype(BF16)

        xp = _ffn(xp, modp, npre, npost, f1_in, f1_out, k=0, bm=bm, blocks_per_batch=t_len // bm)
        u, k_p, v_p, kidx_p, kh, kidxb, v3, qt, qit, wt = _proj_prompt(
            xp, modp, npre, w_row, w_t, bm=bm, blocks_per_batch=t_len // bm, d_ssm=d_ssm, d_att=d_att)
        y_ssm, sp_re, sp_im = _ssm_prompt(u, ops, ssm_d[l], batch=batch, t_len=t_len)
        attn = _attn_prompt(qt, qit, wt, kh, kidxb, v3, batch=batch, t_len=t_len, topk=topk_p)
        xp = _mix(xp, modp, npre, npost, y_ssm, attn, g_w, g_v, wba, w_gates, wo,
                  bm=bm, blocks_per_batch=t_len // bm)
        xp = _ffn(xp, modp, npre, npost, f2_in, f2_out, k=2, bm=bm, blocks_per_batch=t_len // bm)
        new_p.append((k_p.reshape(batch, t_len, N_HEADS, HEAD_DIM), v_p.reshape(batch, t_len, N_HEADS, HEAD_DIM),
                      kidx_p.reshape(batch, t_len, IDX_DIM), sp_re, sp_im))

        xs = _ffn(xs, mods, npre, npost, f1_in, f1_out, k=0, bm=dec_batch, blocks_per_batch=1)
        pr = _proj_sample(xs, mods, npre, w_all)
        u_s, q_s, k_s, v_s = pr[:, o_u:o_q], pr[:, o_q:o_k], pr[:, o_k:o_v], pr[:, o_v:o_qi]
        qi_s, ki_s, wi_s = pr[:, o_qi:o_ki], pr[:, o_ki:o_wi], pr[:, o_wi:o_ga]
        y_ssm_s, ss_re, ss_im = _ssm_sample(u_s, state_ssm_re[l], state_ssm_im[l], ops, c_step, ssm_d[l])
        hcol = lambda a: a.reshape(dec_batch, N_HEADS, HEAD_DIM, 1)
        attn_s = _attn_sample(page_table, qi_s.reshape(dec_batch, IDX_HEADS, IDX_DIM, 1),
                              wi_s.reshape(dec_batch, IDX_HEADS, 1), ki_s.reshape(dec_batch, IDX_DIM, 1),
                              hcol(q_s), hcol(k_s), hcol(v_s), cache_idx_t, cache_k_t, cache_v_t,
                              layer=l, topk=topk_s)
        xs = _mix(xs, mods, npre, npost, y_ssm_s, attn_s.reshape(dec_batch, d_att), g_w, g_v, wba, w_gates, wo,
                  bm=dec_batch, blocks_per_batch=1)
        xs = _ffn(xs, mods, npre, npost, f2_in, f2_out, k=2, bm=dec_batch, blocks_per_batch=1)
        new_s.append((k_s.reshape(dec_batch, 1, N_HEADS, HEAD_DIM), v_s.reshape(dec_batch, 1, N_HEADS, HEAD_DIM),
                      ki_s.reshape(dec_batch, 1, IDX_DIM), ss_re, ss_im))

    stack = lambda states, i: jnp.stack([s[i] for s in states])
    return (xp.reshape(batch, t_len, d), xs.reshape(dec_batch, 1, d),
            stack(new_p, 0), stack(new_p, 1), stack(new_p, 2), stack(new_p, 3), stack(new_p, 4),
            stack(new_s, 0), stack(new_s, 1), stack(new_s, 2), stack(new_s, 3), stack(new_s, 4))
```

```python
import functools
import math

import jax
import jax.numpy as jnp
import numpy as np
from jax import lax
from jax.experimental import pallas as pl
from jax.experimental.pallas import tpu as pltpu

F32 = jnp.float32
BF16 = jnp.bfloat16
I32 = jnp.int32

EPS = 1e-6
SSM_GROUP = 16
SSM_N = 64
N_HEADS = 8
HEAD_DIM = 64
IDX_HEADS = 8
IDX_DIM = 64
INDEX_TOPK = 256
PAGE_SIZE = 128
IDX_SCALE = IDX_DIM ** -0.5 * IDX_HEADS ** -0.5
ATT_SCALE = HEAD_DIM ** -0.5

LANES = 128
SUBLANES = 8
VMEM_LIMIT_BYTES = 56 * 1024 * 1024
CHUNK = 8
OCTET = LANES // SSM_GROUP
QB = 256
NEG_BIG = float(np.finfo(np.float32).min)
FLOAT_MID_ITERS = 40
SEARCH_PROBES_PER_TEST = 4
MIN_NORMAL_KEY = 0x00800000


def _cparams(*sem):
    return pltpu.CompilerParams(dimension_semantics=sem, vmem_limit_bytes=VMEM_LIMIT_BYTES)


def _const_spec(shape):
    nd = len(shape)
    return pl.BlockSpec(shape, lambda *_: (0,) * nd, pipeline_mode=pl.Buffered(1))


def _rms(x, g):
    ms = jnp.mean(x * x, axis=-1, keepdims=True)
    return x * lax.rsqrt(ms + EPS) * g


def _dot(a, b):
    return jnp.dot(a, b, preferred_element_type=F32)


def _dot_nt(a, b):
    return lax.dot_general(a, b, (((1,), (1,)), ((), ())), preferred_element_type=F32)


def _mod_kernel(c_ref, w_ref, b_ref, o_ref):
    a = jax.nn.silu(c_ref[...]).astype(BF16)
    o_ref[...] = _dot(a, w_ref[...].astype(BF16)) + b_ref[...]


def _modulation(c_all, mod_w, mod_b):
    depth, d, n = mod_w.shape
    r = c_all.shape[0]
    tn = 1024
    return pl.pallas_call(
        _mod_kernel,
        grid=(depth, n // tn),
        in_specs=[
            pl.BlockSpec((r, d), lambda l, j: (0, 0)),
            pl.BlockSpec((None, d, tn), lambda l, j: (l, 0, j)),
            pl.BlockSpec((None, 1, tn), lambda l, j: (l, 0, j)),
        ],
        out_specs=pl.BlockSpec((None, r, tn), lambda l, j: (l, 0, j)),
        out_shape=jax.ShapeDtypeStruct((depth, r, n), F32),
        compiler_params=_cparams("parallel", "parallel"),
        name="modulation",
    )(c_all, mod_w, mod_b.reshape(depth, 1, n))


def _mod_spec(nb, d, rows_per_batch_block):
    return pl.BlockSpec((9, None, nb, d), lambda i: (0, i // rows_per_batch_block, 0, 0))


def _ffn_kernel(x_ref, mod_ref, npre_ref, npost_ref, win_ref, wout_ref, o_ref, h_ref, a_ref, *, k, dff, ck):
    x = x_ref[...]
    sh, sc, gt = mod_ref[3 * k], mod_ref[3 * k + 1], mod_ref[3 * k + 2]
    h = _rms(x, npre_ref[k:k + 1, :]) * (1.0 + sc) + sh
    h_ref[...] = h.astype(BF16)
    for c in range(dff // ck):
        hb = h_ref[...]
        g = _dot(hb, win_ref[:, c * ck:(c + 1) * ck])
        u = _dot(hb, win_ref[:, dff + c * ck:dff + (c + 1) * ck])
        a_ref[:, c * ck:(c + 1) * ck] = (jax.nn.silu(g) * u).astype(BF16)
    y = _dot(a_ref[...], wout_ref[...])
    o_ref[...] = x + 0.5 * gt * _rms(y, npost_ref[k:k + 1, :])


def _ffn(x, mod, npre, npost, w_in, w_out, *, k, bm, blocks_per_batch):
    m, d = x.shape
    dff = w_out.shape[0]
    nb = mod.shape[2]
    kern = functools.partial(_ffn_kernel, k=k, dff=dff, ck=256)
    return pl.pallas_call(
        kern,
        grid=(m // bm,),
        in_specs=[
            pl.BlockSpec((bm, d), lambda i: (i, 0)),
            _mod_spec(nb, d, blocks_per_batch),
            _const_spec(npre.shape),
            _const_spec(npost.shape),
            _const_spec(w_in.shape),
            _const_spec(w_out.shape),
        ],
        out_specs=pl.BlockSpec((bm, d), lambda i: (i, 0)),
        out_shape=jax.ShapeDtypeStruct((m, d), F32),
        scratch_shapes=[pltpu.VMEM((bm, d), BF16), pltpu.VMEM((bm, dff), BF16)],
        compiler_params=_cparams("parallel"),
        name=f"ffn{k}",
    )(x, mod, npre, npost, w_in, w_out)


def _proj_prompt_kernel(x_ref, mod_ref, npre_ref, wrow_ref, wt_ref,
                        u_ref, kh_ref, kidxb_ref, kt_ref, vt_ref, kit_ref, v3_ref, qt_ref, qit_ref, wt_out_ref,
                        *, d_ssm, d_att):
    x = x_ref[...]
    sh, sc = mod_ref[3], mod_ref[4]
    h = (_rms(x, npre_ref[1:2, :]) * (1.0 + sc) + sh).astype(BF16)
    o_k = d_ssm + d_att
    o_ki = d_ssm + 3 * d_att + IDX_HEADS * IDX_DIM
    u_ref[...] = _dot(h, wrow_ref[:, 0:d_ssm])
    k = _dot(h, wrow_ref[:, o_k:o_k + d_att])
    kidxb_ref[...] = _dot(h, wrow_ref[:, o_ki:o_ki + IDX_DIM]).astype(BF16)
    for hh in range(N_HEADS):
        kh_ref[hh] = k[:, hh * HEAD_DIM:(hh + 1) * HEAD_DIM].astype(BF16)
    pt = _dot_nt(wt_ref[...], h)
    o = 0
    qt_ref[...] = (pt[o:o + d_att] * ATT_SCALE).astype(BF16)
    o += d_att
    kt_ref[...] = pt[o:o + d_att]
    o += d_att
    vt = pt[o:o + d_att]
    vt_ref[...] = vt
    vtb = vt.astype(BF16)
    for jj in range(v3_ref.shape[0]):
        v3_ref[jj] = vtb[:, jj * QB:(jj + 1) * QB]
    o += d_att
    qit_ref[...] = pt[o:o + IDX_HEADS * IDX_DIM].astype(BF16)
    o += IDX_HEADS * IDX_DIM
    kit_ref[...] = pt[o:o + IDX_DIM]
    o += IDX_DIM
    wt_out_ref[...] = pt[o:o + IDX_HEADS] * IDX_SCALE


def _proj_prompt(x, mod, npre, w_row, w_t, *, batch, bm, blocks_per_batch, d_ssm, d_att):
    m, d = x.shape
    t_len = m // batch
    nbt = blocks_per_batch
    row = lambda w: pl.BlockSpec((bm, w), lambda i: (i, 0))
    col = lambda r: pl.BlockSpec((r, bm), lambda i: (0, i))
    bcol = lambda r: pl.BlockSpec((None, r, bm), lambda i: (i // nbt, 0, i % nbt))
    kern = functools.partial(_proj_prompt_kernel, d_ssm=d_ssm, d_att=d_att)
    return pl.pallas_call(
        kern,
        grid=(m // bm,),
        in_specs=[
            pl.BlockSpec((bm, d), lambda i: (i, 0)),
            _mod_spec(1, d, blocks_per_batch),
            _const_spec(npre.shape),
            _const_spec(w_row.shape),
            _const_spec(w_t.shape),
        ],
        out_specs=[
            row(d_ssm),
            pl.BlockSpec((N_HEADS, bm, HEAD_DIM), lambda i: (0, i, 0)),
            row(IDX_DIM),
            bcol(d_att), bcol(d_att), bcol(IDX_DIM),
            pl.BlockSpec((bm // QB, d_att, QB), lambda i: (i, 0, 0)),
            col(d_att), col(d_att), col(IDX_HEADS),
        ],
        out_shape=[
            jax.ShapeDtypeStruct((m, d_ssm), F32),
            jax.ShapeDtypeStruct((N_HEADS, m, HEAD_DIM), BF16),
            jax.ShapeDtypeStruct((m, IDX_DIM), BF16),
            jax.ShapeDtypeStruct((batch, d_att, t_len), F32),
            jax.ShapeDtypeStruct((batch, d_att, t_len), F32),
            jax.ShapeDtypeStruct((batch, IDX_DIM, t_len), F32),
            jax.ShapeDtypeStruct((m // QB, d_att, QB), BF16),
            jax.ShapeDtypeStruct((d_att, m), BF16),
            jax.ShapeDtypeStruct((d_att, m), BF16),
            jax.ShapeDtypeStruct((IDX_HEADS, m), F32),
        ],
        compiler_params=_cparams("parallel"),
        name="proj_prompt",
    )(x, mod, npre, w_row, w_t)


def _proj_sample_kernel(x_ref, mod_ref, npre_ref, w_ref, o_ref):
    x = x_ref[...]
    sh, sc = mod_ref[3], mod_ref[4]
    h = (_rms(x, npre_ref[1:2, :]) * (1.0 + sc) + sh).astype(BF16)
    o_ref[...] = _dot(h, w_ref[...])


def _proj_sample(x, mod, npre, w):
    m, d = x.shape
    n = w.shape[1]
    return pl.pallas_call(
        _proj_sample_kernel,
        grid=(1,),
        in_specs=[
            pl.BlockSpec((m, d), lambda i: (0, 0)),
            _mod_spec(m, d, 1),
            _const_spec(npre.shape),
            _const_spec(w.shape),
        ],
        out_specs=pl.BlockSpec((m, n), lambda i: (0, 0)),
        out_shape=jax.ShapeDtypeStruct((m, n), F32),
        compiler_params=_cparams("arbitrary"),
        name="proj_sample",
    )(x, mod, npre, w)


def _ssm_prep_kernel(ldt_ref, ara_ref, aia_ref, arb_ref, aib_ref, btre_ref, btim_ref, cre_ref, cim_ref,
                     ctre_ref, ctim_ref,
                     kk_ref, wre_ref, wim_ref, pre_ref, pim_ref, al_ref, ab_ref, bbre_ref, bbim_ref):
    dt = jnp.exp(ldt_ref[...])

    def powers(ar, ai, k):
        mag = jnp.exp(dt * ar * k)
        ph = dt * ai * k
        return mag * jnp.cos(ph), mag * jnp.sin(ph)

    ar, ai = ara_ref[...], aia_ref[...]
    abr, abi = powers(ar, ai, 1.0)
    den = ar * ar + ai * ai
    z_re = ((abr - 1.0) * ar + abi * ai) / den
    z_im = (abi * ar - (abr - 1.0) * ai) / den
    b_re, b_im = btre_ref[...], btim_ref[...]
    bb_re = z_re * b_re - z_im * b_im
    bb_im = z_re * b_im + z_im * b_re
    bbre_ref[...] = bb_re
    bbim_ref[...] = bb_im
    ab_ref[0] = abr
    ab_ref[1] = abi
    lr, li = powers(ar, ai, float(CHUNK))
    al_ref[0] = lr
    al_ref[1] = li
    c_re, c_im = cre_ref[...], cim_ref[...]
    ck_re, ck_im = [], []
    for k in range(CHUNK):
        pr, pi = powers(ar, ai, float(CHUNK - 1 - k))
        wre_ref[k] = pr * bb_re - pi * bb_im
        wim_ref[k] = pr * bb_im + pi * bb_re
        pr, pi = powers(ar, ai, float(k))
        ck_re.append(c_re * pr - c_im * pi)
        ck_im.append(c_re * pi + c_im * pr)
    bnt = functools.partial(lax.dot_general, dimension_numbers=(((2,), (2,)), ((0,), (0,))),
                            precision=lax.Precision.HIGHEST, preferred_element_type=F32)
    kk_ref[...] = (bnt(bb_re, jnp.concatenate(ck_re, axis=1)) - bnt(bb_im, jnp.concatenate(ck_im, axis=1)))

    ar, ai = arb_ref[...], aib_ref[...]
    c_re, c_im = ctre_ref[...], ctim_ref[...]
    for t in range(CHUNK):
        pr, pi = powers(ar, ai, float(t + 1))
        pre_ref[t] = c_re * pr - c_im * pi
        pim_ref[t] = -(c_re * pi + c_im * pr)


def _ssm_prep(log_dt, a_re, a_im, b_re, b_im, c_re, c_im):
    g, n = a_re.shape
    j = SSM_GROUP
    shp = lambda *s: jax.ShapeDtypeStruct(s, F32)
    swap = lambda a: a.transpose(0, 2, 1)
    return pl.pallas_call(
        _ssm_prep_kernel,
        out_shape=[shp(g, j, CHUNK * j), shp(CHUNK, g, j, n), shp(CHUNK, g, j, n),
                   shp(CHUNK, g, n, j), shp(CHUNK, g, n, j), shp(2, g, 1, n), shp(2, g, 1, n),
                   shp(g, j, n), shp(g, j, n)],
        compiler_params=pltpu.CompilerParams(vmem_limit_bytes=VMEM_LIMIT_BYTES),
        name="ssm_prep",
    )(log_dt.reshape(g, 1, 1), a_re.reshape(g, 1, n), a_im.reshape(g, 1, n),
      a_re.reshape(g, n, 1), a_im.reshape(g, n, 1), swap(b_re), swap(b_im), c_re, c_im, swap(c_re), swap(c_im))


def _block_diag(x, groups):
    rows, c = x.shape[-2:]
    keep = (np.arange(rows)[:, None] // (rows // groups)) == (np.arange(groups * c)[None, :] // c)
    return jnp.where(keep, jnp.tile(x, (1,) * (x.ndim - 1) + (groups,)), 0.0)


def _ssm_operators(prep, c_re, c_im):
    kk, w_re, w_im, p_re, p_im, al, ab, bb_re, bb_im = prep
    g, j, n = bb_re.shape
    no, kw = g // OCTET, CHUNK * LANES
    x = kk.reshape(no, OCTET, j, CHUNK, j).transpose(0, 3, 1, 2, 4).reshape(no, CHUNK, LANES, j)
    tiles = _block_diag(x, OCTET).astype(BF16)
    lag = np.arange(CHUNK)[None, :] - np.arange(CHUNK)[:, None]
    toep = jnp.where((lag >= 0)[None, :, :, None, None], tiles[:, np.clip(lag, 0, None)], 0)
    m_op = toep.transpose(0, 1, 3, 2, 4).reshape(no, kw, kw)
    x = jnp.stack([w_re, w_im], 0).reshape(2, CHUNK, no, LANES, n)
    w_op = _block_diag(x, OCTET).astype(BF16).transpose(2, 1, 3, 0, 4).reshape(no, kw, 2 * OCTET * n)
    x = jnp.stack([p_re, p_im], 0).reshape(2, CHUNK, no, OCTET * n, j)
    p_op = _block_diag(x, OCTET).astype(BF16).transpose(2, 0, 3, 1, 4).reshape(no, 2 * OCTET * n, kw)
    a_chunk = al.reshape(2, no, OCTET * n).transpose(1, 0, 2)
    b_step = _block_diag(jnp.stack([bb_re, bb_im], 0).reshape(2, g * j, n), g)
    b_step = b_step.transpose(1, 0, 2).reshape(g * j, 2 * g * n)
    c_t = jnp.stack([c_re, -c_im], 0).transpose(0, 1, 3, 2).reshape(2, g * n, j)
    c_step = _block_diag(c_t, g).reshape(2 * g * n, g * j).astype(BF16)
    return dict(m=m_op, w=w_op, p=p_op, a_chunk=a_chunk, b_step=b_step, c_step=c_step, abar=ab.reshape(2, g * n))


def _ssm_prompt_kernel(u_ref, m_ref, w_ref, p_ref, al_ref, d_ref, y_ref, sfin_ref,
                       uo_ref, v_ref, sc_ref, *, r):
    half = sc_ref.shape[1] // 2
    for tau in range(CHUNK):
        uo_ref[:, tau * LANES:(tau + 1) * LANES] = u_ref[pl.ds(tau, r, stride=CHUNK), :].astype(BF16)
    uo = uo_ref[...]
    v_ref[...] = _dot(uo, w_ref[...])
    a_r, a_i = al_ref[0:1, :], al_ref[1:2, :]

    def step(c, carry):
        s_r, s_i = carry
        sc_ref[pl.ds(c, 1), 0:half] = s_r
        sc_ref[pl.ds(c, 1), half:2 * half] = s_i
        v = v_ref[pl.ds(c, 1), :]
        return (a_r * s_r - a_i * s_i + v[:, 0:half], a_r * s_i + a_i * s_r + v[:, half:2 * half])

    zero = jnp.zeros((1, half), F32)
    s_r, s_i = lax.fori_loop(0, r, step, (zero, zero))
    sfin_ref[0:1, :] = s_r
    sfin_ref[1:2, :] = s_i
    y = _dot(uo, m_ref[...]) + _dot(sc_ref[...].astype(BF16), p_ref[...])
    d = d_ref[...]
    for t in range(CHUNK):
        rows = pl.ds(t, r, stride=CHUNK)
        y_ref[rows, :] = y[:, t * LANES:(t + 1) * LANES] + d * u_ref[rows, :]


def _ssm_prompt(u, ops, d_skip, *, batch, t_len):
    m, d_ssm = u.shape
    no = d_ssm // LANES
    r = t_len // CHUNK
    kw = CHUNK * LANES
    sw = ops["w"].shape[2]
    op_spec = lambda a, b: pl.BlockSpec((None, a, b), lambda o, bb: (o, 0, 0))
    y, sfin = pl.pallas_call(
        functools.partial(_ssm_prompt_kernel, r=r),
        grid=(no, batch),
        in_specs=[
            pl.BlockSpec((t_len, LANES), lambda o, bb: (bb, o)),
            op_spec(kw, kw), op_spec(kw, sw), op_spec(sw, kw),
            pl.BlockSpec((None, 2, sw // 2), lambda o, bb: (o, 0, 0)),
            pl.BlockSpec((1, LANES), lambda o, bb: (0, o)),
        ],
        out_specs=[
            pl.BlockSpec((t_len, LANES), lambda o, bb: (bb, o)),
            pl.BlockSpec((None, None, 2, sw // 2), lambda o, bb: (bb, o, 0, 0)),
        ],
        out_shape=[jax.ShapeDtypeStruct((m, d_ssm), F32),
                   jax.ShapeDtypeStruct((batch, no, 2, sw // 2), F32)],
        scratch_shapes=[pltpu.VMEM((r, kw), BF16), pltpu.VMEM((r, sw), F32), pltpu.VMEM((r, sw), F32)],
        compiler_params=_cparams("parallel", "parallel"),
        name="ssm_prompt",
    )(u, ops["m"], ops["w"], ops["p"], ops["a_chunk"], d_skip.reshape(1, d_ssm))
    s = sfin.reshape(batch, no, 2, OCTET, SSM_N).transpose(2, 0, 1, 3, 4).reshape(2, batch, no * OCTET, SSM_N)
    return y, s[0], s[1]


def _ssm_sample_kernel(u_ref, h_ref, ab_ref, bstep_ref, cstep_ref, d_ref, y_ref, s_ref):
    u = u_ref[...]
    half = h_ref.shape[2]
    bu = jnp.dot(u, bstep_ref[...], precision=lax.Precision.HIGHEST, preferred_element_type=F32)
    a_r, a_i = ab_ref[0:1, :], ab_ref[1:2, :]
    h_r, h_i = h_ref[0], h_ref[1]
    s_r = a_r * h_r - a_i * h_i + bu[:, 0:half]
    s_i = a_r * h_i + a_i * h_r + bu[:, half:2 * half]
    s_ref[0] = s_r
    s_ref[1] = s_i
    s = jnp.concatenate([s_r, s_i], axis=1).astype(BF16)
    y_ref[...] = _dot(s, cstep_ref[...]) + d_ref[...] * u


def _ssm_sample(u, h_re, h_im, ops, c_step, d_skip):
    bsz, d_ssm = u.shape
    gn = h_re.shape[1] * h_re.shape[2]
    h = jnp.stack([h_re.reshape(bsz, gn), h_im.reshape(bsz, gn)], 0)
    y, s = pl.pallas_call(
        _ssm_sample_kernel,
        out_shape=[jax.ShapeDtypeStruct((bsz, d_ssm), F32), jax.ShapeDtypeStruct((2, bsz, gn), F32)],
        compiler_params=pltpu.CompilerParams(vmem_limit_bytes=VMEM_LIMIT_BYTES),
        name="ssm_sample",
    )(u, h, ops["abar"], ops["b_step"], c_step, d_skip.reshape(1, d_ssm))
    return y, s[0].reshape(h_re.shape), s[1].reshape(h_im.shape)


def _f2key(x):
    b = lax.bitcast_convert_type(x, I32)
    return b ^ ((b >> 31) & 0x7FFFFFFF)


def _key2f(k):
    return lax.bitcast_convert_type(k ^ ((k >> 31) & 0x7FFFFFFF), F32)


def _search_init(mn, mx, n_valid, need):
    z = jnp.zeros_like(n_valid)
    return (_f2key(mn), _f2key(mx) + 1, n_valid, z, jnp.full(mn.shape, NEG_BIG, F32),
            jnp.where(need, 0, 1).astype(I32), z)


def _search_probe(state, it, topk):
    lo, hi, c_lo, c_hi = state[:4]
    lo_f, hi_f = _key2f(lo), _key2f(hi)
    frac = (c_lo.astype(F32) - (topk + 0.5)) / jnp.maximum(c_lo - c_hi, 1).astype(F32)
    frac = jnp.where(it % 3 == 2, 0.5, frac)
    cand = _f2key(lo_f + (hi_f - lo_f) * frac)
    mid_k = (lo >> 1) + (hi >> 1) + (lo & hi & 1)
    inside = (cand > lo) & (cand < hi) & (it < FLOAT_MID_ITERS)
    probe = jnp.where(inside, cand, mid_k)
    zero_k = jnp.where(it == 0, 0, MIN_NORMAL_KEY)
    return jnp.where((it < 2) & (zero_k > lo) & (zero_k < hi), zero_k, probe)


def _search_update(state, mid, cnt, topk):
    lo, hi, c_lo, c_hi, thr, done, tie = state
    hit = cnt == topk
    up, dn = cnt > topk, cnt < topk
    lo_n, hi_n = jnp.where(up, mid, lo), jnp.where(dn, mid, hi)
    adj = ((hi_n == lo_n + 1) | ((lo_n == 0) & (hi_n == MIN_NORMAL_KEY))) & jnp.logical_not(hit)
    fin = hit | adj
    thr_n = jnp.where(hit, _key2f(mid), _key2f(lo_n))
    act = done == 0
    lo = jnp.where(act, lo_n, lo)
    hi = jnp.where(act, hi_n, hi)
    c_lo = jnp.where(act & up, cnt, c_lo)
    c_hi = jnp.where(act & dn, cnt, c_hi)
    thr = jnp.where(act & fin, thr_n, thr)
    tie = jnp.where(act & adj, 1, tie)
    done = jnp.where(act & fin, 1, done)
    return lo, hi, c_lo, c_hi, thr, done, tie


def _search(count_ge, init, topk):
    def pending(state, it):
        return jnp.logical_and(jnp.min(state[5].astype(F32)) == 0.0, it < 96)

    def probe(u, carry):
        it, state = carry
        mid = _search_probe(state, it, topk)
        return it + 1, _search_update(state, mid, count_ge(_key2f(mid)), topk)

    def body(carry):
        it, state = lax.fori_loop(0, SEARCH_PROBES_PER_TEST, probe, carry[1:])
        return pending(state, it), it, state

    _, _, state = lax.while_loop(lambda c: c[0], body, (pending(init, 0), jnp.int32(0), init))
    return state[4], state[6]


def _attn_prompt_kernel(qt_ref, qit_ref, wt_ref, kh_ref, kidx_ref, v3_ref, o_ref, sc_ref, lg_ref, acc_ref, *, topk):
    i = pl.program_id(1)
    nk = i + 1
    qpos = i * QB + lax.broadcasted_iota(I32, (QB, QB), 1)
    krow = lax.broadcasted_iota(I32, (QB, QB), 0)
    fold = lambda x: x.reshape(QB // SUBLANES, SUBLANES, QB)

    tile_rows = lambda j: pl.ds(pl.multiple_of(j * QB, QB), QB)
    rep = lambda x: jnp.broadcast_to(x, (SUBLANES, QB))

    def score_tile(j, carry):
        mn, mx = carry
        kx = kidx_ref[tile_rows(j), :]
        acc = jnp.zeros((QB, QB), F32)
        for h in range(IDX_HEADS):
            s = _dot(kx, qit_ref[h * IDX_DIM:(h + 1) * IDX_DIM, :])
            acc = acc + jnp.maximum(s, 0.0) * wt_ref[h:h + 1, :]
        valid = (j * QB + krow) <= qpos
        sc = jnp.where(valid, acc, -jnp.inf)
        sc_ref[j] = sc
        mx = jnp.maximum(mx, jnp.max(fold(sc), axis=0))
        mn = jnp.minimum(mn, jnp.min(fold(jnp.where(valid, acc, jnp.inf)), axis=0))
        return mn, mx

    mn, mx = lax.fori_loop(0, nk, score_tile,
                           (jnp.full((SUBLANES, QB), jnp.inf, F32), jnp.full((SUBLANES, QB), -jnp.inf, F32)))
    mn = rep(jnp.min(mn, axis=0, keepdims=True))
    mx = rep(jnp.max(mx, axis=0, keepdims=True))

    def count_ge(t):
        t1 = t[0:1, :]

        def body(j, c):
            return c + jnp.sum(fold((sc_ref[j] >= t1).astype(I32)), axis=0)

        c = lax.fori_loop(0, nk, body, jnp.zeros((SUBLANES, QB), I32))
        return rep(jnp.sum(c, axis=0, keepdims=True))

    n_valid = qpos[0:SUBLANES, :] + 1
    thr, tie = _search(count_ge, _search_init(mn, mx, n_valid, n_valid > topk), topk)

    @pl.when(jnp.max(tie) > 0)
    def _():
        tri = (lax.broadcasted_iota(I32, (QB, QB), 0) >= lax.broadcasted_iota(I32, (QB, QB), 1)).astype(BF16)
        thr1, tie1 = thr[0:1, :], tie[0:1, :] > 0

        def gt_tile(j, c):
            return c + jnp.sum((sc_ref[j] > thr1).astype(I32), axis=0, keepdims=True)

        quota = topk - lax.fori_loop(0, nk, gt_tile, jnp.zeros((1, QB), I32))

        def tie_tile(j, before):
            x = sc_ref[j]
            t = (x == thr1) & tie1
            rank = before + _dot(tri, t.astype(BF16)).astype(I32)
            sc_ref[j] = jnp.where(t & (rank > quota), -jnp.inf, x)
            return before + jnp.sum(t.astype(I32), axis=0, keepdims=True)

        lax.fori_loop(0, nk, tie_tile, jnp.zeros((1, QB), I32))

    thr1 = thr[0:1, :]
    heads = range(N_HEADS)
    hslice = lambda h: slice(h * HEAD_DIM, (h + 1) * HEAD_DIM)

    def logit_tile(j, ms):
        bias = jnp.where(sc_ref[j] >= thr1, 0.0, -jnp.inf)
        rows = tile_rows(j)
        out = []
        for h in heads:
            lg = _dot(kh_ref[h, rows, :], qt_ref[hslice(h), :]) + bias
            lg_ref[h, j] = lg
            out.append(jnp.maximum(ms[h], jnp.max(fold(lg), axis=0)))
        return tuple(out)

    ms = lax.fori_loop(0, nk, logit_tile, tuple(jnp.full((SUBLANES, QB), -jnp.inf, F32) for _ in heads))
    m1 = [jnp.max(m, axis=0, keepdims=True) for m in ms]
    acc_ref[...] = jnp.zeros(acc_ref.shape, F32)

    def pv_tile(j, ls):
        out = []
        for h in heads:
            p = jnp.exp(lg_ref[h, j] - m1[h])
            acc_ref[h] += _dot(v3_ref[j, hslice(h), :], p.astype(BF16))
            out.append(ls[h] + jnp.sum(fold(p), axis=0))
        return tuple(out)

    ls = lax.fori_loop(0, nk, pv_tile, tuple(jnp.zeros((SUBLANES, QB), F32) for _ in heads))
    outs = [acc_ref[h] / jnp.sum(ls[h], axis=0, keepdims=True) for h in heads]
    o_ref[...] = jnp.transpose(jnp.concatenate(outs, axis=0)).astype(BF16)


def _attn_prompt(qt, qit, wt, kh, kidxb, v3, *, batch, t_len, topk):
    d_att, m = qt.shape
    nq = t_len // QB
    return pl.pallas_call(
        functools.partial(_attn_prompt_kernel, topk=topk),
        grid=(batch, nq),
        in_specs=[
            pl.BlockSpec((d_att, QB), lambda b, i: (0, b * nq + i)),
            pl.BlockSpec((d_att, QB), lambda b, i: (0, b * nq + i)),
            pl.BlockSpec((IDX_HEADS, QB), lambda b, i: (0, b * nq + i)),
            pl.BlockSpec((N_HEADS, t_len, HEAD_DIM), lambda b, i: (0, b, 0)),
            pl.BlockSpec((t_len, IDX_DIM), lambda b, i: (b, 0)),
            pl.BlockSpec((nq, d_att, QB), lambda b, i: (b, 0, 0)),
        ],
        out_specs=pl.BlockSpec((QB, d_att), lambda b, i: (b * nq + i, 0)),
        out_shape=jax.ShapeDtypeStruct((m, d_att), BF16),
        scratch_shapes=[pltpu.VMEM((nq, QB, QB), F32), pltpu.VMEM((N_HEADS, nq, QB, QB), F32),
                        pltpu.VMEM((N_HEADS, HEAD_DIM, QB), F32)],
        compiler_params=_cparams("parallel", "arbitrary"),
        name="attn_prompt",
    )(qt, qit, wt, kh, kidxb, v3)


def _attn_sample_kernel(pt_ref, qi_ref, w_ref, kin_ref, q_ref, kn_ref, vn_ref, ci_ref, ck_ref, cv_ref, o_ref,
                        ibuf, kbuf, vbuf, qib, qb, sc_ref, lg_ref, sem_i, sem_k, sem_v,
                        *, layer, n_pages, topk):
    b = pl.program_id(0)

    def copies(src_ref, buf, sem, p, page):
        return pltpu.make_async_copy(src_ref.at[layer, page], buf.at[p], sem)

    streams = ((ci_ref, ibuf, sem_i), (ck_ref, kbuf, sem_k), (cv_ref, vbuf, sem_v))

    def start_all(src_ref, buf, sem, sample):
        def start(p, _):
            copies(src_ref, buf, sem, p, pt_ref[sample, p]).start()
            return 0
        lax.fori_loop(0, n_pages, start, 0)

    @pl.when(b == 0)
    def _():
        start_all(*streams[0], b)
        start_all(*streams[1], b)

    start_all(*streams[2], b)

    def wait_all(src_ref, buf, sem):
        def wait(p, _):
            copies(src_ref, buf, sem, p, 0).wait()
            return 0
        lax.fori_loop(0, n_pages, wait, 0)

    for h in range(IDX_HEADS):
        qib[h] = jnp.broadcast_to(qi_ref[h], (IDX_DIM, PAGE_SIZE))
    for h in range(N_HEADS):
        qb[h] = jnp.broadcast_to(q_ref[h] * ATT_SCALE, (HEAD_DIM, PAGE_SIZE))
    w = w_ref[...] * IDX_SCALE
    dsum = lambda x: jnp.sum(x, axis=0, keepdims=True)

    wait_all(*streams[0])

    def idx_page(p, _):
        page = ibuf[p]
        acc = jnp.zeros((1, PAGE_SIZE), F32)
        for h in range(IDX_HEADS):
            acc = acc + jnp.maximum(dsum(page * qib[h]), 0.0) * w[h:h + 1, :]
        sc_ref[pl.ds(p, 1), :] = acc
        return 0

    lax.fori_loop(0, n_pages, idx_page, 0)
    kin = kin_ref[...]
    s_new = jnp.zeros((1, 1), F32)
    for h in range(IDX_HEADS):
        s_new = s_new + jnp.maximum(dsum(qi_ref[h] * kin), 0.0) * w[h:h + 1, :]
    sc = sc_ref[...]

    def total(x, op=jnp.sum):
        return op(op(x, axis=1, keepdims=True), axis=0, keepdims=True)

    mx = jnp.maximum(total(sc, jnp.max), s_new)
    mn = jnp.minimum(total(sc, jnp.min), s_new)

    def count_ge(t):
        return total((sc >= t).astype(I32)) + (s_new >= t).astype(I32)

    n_keys = jnp.full((1, 1), n_pages * PAGE_SIZE + 1, I32)
    thr, tie = _search(count_ge, _search_init(mn, mx, n_keys, n_keys > topk), topk)

    upper = (lax.broadcasted_iota(I32, (PAGE_SIZE, PAGE_SIZE), 0)
             <= lax.broadcasted_iota(I32, (PAGE_SIZE, PAGE_SIZE), 1)).astype(BF16)
    lower = (lax.broadcasted_iota(I32, (n_pages, n_pages), 1)
             < lax.broadcasted_iota(I32, (n_pages, n_pages), 0)).astype(BF16)

    def flat_rank(mask):
        mb = mask.astype(BF16)
        incl = _dot(mb, upper)
        before = jnp.sum(_dot(lower, mb), axis=1, keepdims=True)
        return (incl + before).astype(I32)

    gt = sc > thr
    tied = tie > 0
    is_tie = (sc == thr) & tied
    quota = jnp.where(tied, topk - total(gt.astype(I32)) - (s_new > thr).astype(I32), topk)
    sel = gt | ((sc == thr) & (flat_rank(is_tie) <= quota))
    n_tie_past = total(is_tie.astype(I32))
    new_sel = (s_new > thr) | ((s_new == thr) & (n_tie_past < quota))

    wait_all(*streams[1])
    for h in range(N_HEADS):
        q_h = qb[h]

        def k_page(p, _, h=h, q_h=q_h):
            lg_ref[h, pl.ds(p, 1), :] = dsum(kbuf[p, h] * q_h)
            return 0

        lax.fori_loop(0, n_pages, k_page, 0, unroll=4)

    @pl.when(b + 1 < pl.num_programs(0))
    def _():
        start_all(*streams[0], b + 1)
        start_all(*streams[1], b + 1)

    p_new = []
    for h in range(N_HEADS):
        lg = jnp.where(sel, lg_ref[h], -jnp.inf)
        lg_new = jnp.where(new_sel, dsum(q_ref[h] * ATT_SCALE * kn_ref[h]), -jnp.inf)
        m = jnp.maximum(total(lg, jnp.max), lg_new)
        p = jnp.exp(lg - m)
        e_new = jnp.exp(lg_new - m)
        denom = total(p) + e_new
        lg_ref[h] = p / denom
        p_new.append(e_new / denom)

    wait_all(*streams[2])
    for h in range(N_HEADS):
        def v_page(p, acc, h=h):
            return acc + vbuf[p, h] * lg_ref[h, pl.ds(p, 1), :]

        acc = lax.fori_loop(0, n_pages, v_page, jnp.zeros((HEAD_DIM, PAGE_SIZE), F32), unroll=4)
        o_ref[h] = jnp.sum(acc, axis=1, keepdims=True) + p_new[h] * vn_ref[h]


def _attn_sample(page_table, q_idx, w_idx, k_idx_new, q, k_new, v_new, cache_idx_t, cache_k_t, cache_v_t,
                 *, layer, topk):
    bsz, n_pages = page_table.shape
    col = lambda r: pl.BlockSpec((None, r, 1), lambda b, pt: (b, 0, 0))
    hcol = lambda r: pl.BlockSpec((None, N_HEADS, r, 1), lambda b, pt: (b, 0, 0, 0))
    any_spec = pl.BlockSpec(memory_space=pl.ANY)
    return pl.pallas_call(
        functools.partial(_attn_sample_kernel, layer=layer, n_pages=n_pages, topk=topk),
        grid_spec=pltpu.PrefetchScalarGridSpec(
            num_scalar_prefetch=1,
            grid=(bsz,),
            in_specs=[hcol(IDX_DIM), col(IDX_HEADS), col(IDX_DIM), hcol(HEAD_DIM), hcol(HEAD_DIM), hcol(HEAD_DIM),
                      any_spec, any_spec, any_spec],
            out_specs=hcol(HEAD_DIM),
            scratch_shapes=[pltpu.VMEM((n_pages, IDX_DIM, PAGE_SIZE), F32),
                            pltpu.VMEM((n_pages, N_HEADS, HEAD_DIM, PAGE_SIZE), F32),
                            pltpu.VMEM((n_pages, N_HEADS, HEAD_DIM, PAGE_SIZE), F32),
                            pltpu.VMEM((IDX_HEADS, IDX_DIM, PAGE_SIZE), F32),
                            pltpu.VMEM((N_HEADS, HEAD_DIM, PAGE_SIZE), F32),
                            pltpu.VMEM((n_pages, PAGE_SIZE), F32),
                            pltpu.VMEM((N_HEADS, n_pages, PAGE_SIZE), F32),
                            pltpu.SemaphoreType.DMA(()), pltpu.SemaphoreType.DMA(()), pltpu.SemaphoreType.DMA(())],
        ),
        out_shape=jax.ShapeDtypeStruct((bsz, N_HEADS, HEAD_DIM, 1), F32),
        compiler_params=_cparams("arbitrary"),
        name="attn_sample",
    )(page_table, q_idx, w_idx, k_idx_new, q, k_new, v_new, cache_idx_t, cache_k_t, cache_v_t)


def _mix_kernel(x_ref, mod_ref, npre_ref, npost_ref, ys_ref, at_ref, gw_ref, gv_ref, wba_ref, wg_ref, wo_ref,
                o_ref):
    x = x_ref[...]
    d = x.shape[1]
    sh, sc, gt = mod_ref[3], mod_ref[4], mod_ref[5]
    h = (_rms(x, npre_ref[1:2, :]) * (1.0 + sc) + sh).astype(BF16)
    gates = _dot(h, wg_ref[...])
    ys = ys_ref[...].astype(BF16)
    y_a = _dot(ys, gw_ref[...]) * jax.nn.sigmoid(_dot(ys, gv_ref[...]))
    y_b = _dot(at_ref[...].astype(BF16), wba_ref[...])
    mixed = jax.nn.sigmoid(gates[:, 0:d]) * y_a + jax.nn.sigmoid(gates[:, d:2 * d]) * y_b
    y = _dot(mixed.astype(BF16), wo_ref[...])
    o_ref[...] = x + gt * _rms(y, npost_ref[1:2, :])


def _mix(x, mod, npre, npost, y_ssm, attn, glu_w, glu_v, wba, w_gates, w_out, *, bm, blocks_per_batch):
    m, d = x.shape
    nb = mod.shape[2]
    return pl.pallas_call(
        _mix_kernel,
        grid=(m // bm,),
        in_specs=[
            pl.BlockSpec((bm, d), lambda i: (i, 0)),
            _mod_spec(nb, d, blocks_per_batch),
            _const_spec(npre.shape), _const_spec(npost.shape),
            pl.BlockSpec((bm, y_ssm.shape[1]), lambda i: (i, 0)),
            pl.BlockSpec((bm, attn.shape[1]), lambda i: (i, 0)),
            _const_spec(glu_w.shape), _const_spec(glu_v.shape), _const_spec(wba.shape),
            _const_spec(w_gates.shape), _const_spec(w_out.shape),
        ],
        out_specs=pl.BlockSpec((bm, d), lambda i: (i, 0)),
        out_shape=jax.ShapeDtypeStruct((m, d), F32),
        compiler_params=_cparams("parallel"),
        name="mix",
    )(x, mod, npre, npost, y_ssm, attn, glu_w, glu_v, wba, w_gates, w_out)


def _pad_cols(w, n):
    return jnp.pad(w, ((0, 0), (0, n - w.shape[1])))


def kernel(x_prompt, x_sample, cache_k, cache_v, cache_idx_k, state_ssm_re, state_ssm_im, page_table,
           c_prompt, c_sample, mod_w, mod_b, norm_pre, norm_post, ffn1_in, ffn1_out, w_in,
           ssm_log_dt, ssm_a_re, ssm_a_im, ssm_b_re, ssm_b_im, ssm_c_re, ssm_c_im, ssm_d,
           glu_w, glu_v, w_branch_attn, w_out, ffn2_in, ffn2_out):
    batch, t_len, d = x_prompt.shape
    dec_batch, dec_seq, _ = x_sample.shape
    depth = mod_w.shape[0]
    d_ssm = ssm_d.shape[1]
    d_att = N_HEADS * HEAD_DIM
    n_groups = d_ssm // SSM_GROUP
    assert dec_seq == 1 and t_len % QB == 0 and d_ssm % LANES == 0
    m = batch * t_len
    bm = 512 if m % 512 == 0 else QB
    topk_p = min(INDEX_TOPK, t_len // 4)
    n_pages = page_table.shape[1]
    topk_s = min(INDEX_TOPK, (n_pages * PAGE_SIZE + dec_seq) // 4)

    mod = _modulation(jnp.concatenate([c_prompt, c_sample], 0), mod_w, mod_b).reshape(depth, -1, 9, d)

    widths = (d_ssm, d_att, d_att, d_att, IDX_HEADS * IDX_DIM, IDX_DIM, IDX_HEADS, d, d)
    off = np.concatenate([[0], np.cumsum(widths)])
    o_u, o_q, o_k, o_v, o_qi, o_ki, o_wi, o_ga, o_gb, o_end = (int(v) for v in off)

    cache_k_t = cache_k.transpose(0, 1, 3, 4, 2)
    cache_v_t = cache_v.transpose(0, 1, 3, 4, 2)
    cache_idx_t = cache_idx_k.transpose(0, 1, 3, 2)

    xp = x_prompt.reshape(m, d)
    xs = x_sample.reshape(dec_batch, d)
    new_p, new_s = [], []
    for l in range(depth):
        bf = lambda w: w[l].astype(BF16)
        modp = mod[l, :batch].transpose(1, 0, 2).reshape(9, batch, 1, d)
        mods = mod[l, batch:].transpose(1, 0, 2).reshape(9, 1, dec_batch, d)
        npre, npost = norm_pre[l], norm_post[l]
        f1_in, f1_out, f2_in, f2_out = bf(ffn1_in), bf(ffn1_out), bf(ffn2_in), bf(ffn2_out)
        wl = w_in[l]
        w_t = wl[:, o_q:o_ga].T.astype(BF16)
        w_gates = wl[:, o_ga:o_end].astype(BF16)
        w_all = _pad_cols(wl, 37 * LANES).astype(BF16)
        g_w, g_v, wba, wo = bf(glu_w), bf(glu_v), bf(w_branch_attn), bf(w_out)
        prep = _ssm_prep(ssm_log_dt[l], ssm_a_re[l], ssm_a_im[l], ssm_b_re[l], ssm_b_im[l],
                         ssm_c_re[l], ssm_c_im[l])
        ops = _ssm_operators(prep, ssm_c_re[l], ssm_c_im[l])

        xp = _ffn(xp, modp, npre, npost, f1_in, f1_out, k=0, bm=bm, blocks_per_batch=t_len // bm)
        u, kh, kidxb, k_t, v_t, kidx_t, v3, qt, qit, wt = _proj_prompt(
            xp, modp, npre, w_all, w_t, batch=batch, bm=bm, blocks_per_batch=t_len // bm, d_ssm=d_ssm, d_att=d_att)
        y_ssm, sp_re, sp_im = _ssm_prompt(u, ops, ssm_d[l], batch=batch, t_len=t_len)
        attn = _attn_prompt(qt, qit, wt, kh, kidxb, v3, batch=batch, t_len=t_len, topk=topk_p)
        xp = _mix(xp, modp, npre, npost, y_ssm, attn, g_w, g_v, wba, w_gates, wo,
                  bm=bm, blocks_per_batch=t_len // bm)
        xp = _ffn(xp, modp, npre, npost, f2_in, f2_out, k=2, bm=bm, blocks_per_batch=t_len // bm)
        new_p.append((k_t, v_t, kidx_t, sp_re, sp_im))

        xs = _ffn(xs, mods, npre, npost, f1_in, f1_out, k=0, bm=dec_batch, blocks_per_batch=1)
        pr = _proj_sample(xs, mods, npre, w_all)
        u_s, q_s, k_s, v_s = pr[:, o_u:o_q], pr[:, o_q:o_k], pr[:, o_k:o_v], pr[:, o_v:o_qi]
        qi_s, ki_s, wi_s = pr[:, o_qi:o_ki], pr[:, o_ki:o_wi], pr[:, o_wi:o_ga]
        y_ssm_s, ss_re, ss_im = _ssm_sample(u_s, state_ssm_re[l], state_ssm_im[l], ops, ops["c_step"], ssm_d[l])
        hcol = lambda a: a.reshape(dec_batch, N_HEADS, HEAD_DIM, 1)
        attn_s = _attn_sample(page_table, qi_s.reshape(dec_batch, IDX_HEADS, IDX_DIM, 1),
                              wi_s.reshape(dec_batch, IDX_HEADS, 1), ki_s.reshape(dec_batch, IDX_DIM, 1),
                              hcol(q_s), hcol(k_s), hcol(v_s), cache_idx_t, cache_k_t, cache_v_t,
                              layer=l, topk=topk_s)
        xs = _mix(xs, mods, npre, npost, y_ssm_s, attn_s.reshape(dec_batch, d_att), g_w, g_v, wba, w_gates, wo,
                  bm=dec_batch, blocks_per_batch=1)
        xs = _ffn(xs, mods, npre, npost, f2_in, f2_out, k=2, bm=dec_batch, blocks_per_batch=1)
        new_s.append((k_s.reshape(dec_batch, 1, N_HEADS, HEAD_DIM), v_s.reshape(dec_batch, 1, N_HEADS, HEAD_DIM),
                      ki_s.reshape(dec_batch, 1, IDX_DIM), ss_re, ss_im))

    stack = lambda states, i: jnp.stack([s[i] for s in states])
    heads_last = lambda a: a.reshape(depth, batch, N_HEADS, HEAD_DIM, t_len).transpose(0, 1, 4, 2, 3)
    return (xp.reshape(batch, t_len, d), xs.reshape(dec_batch, 1, d),
            heads_last(stack(new_p, 0)), heads_last(stack(new_p, 1)), stack(new_p, 2).transpose(0, 1, 3, 2),
            stack(new_p, 3), stack(new_p, 4),
            stack(new_s, 0), stack(new_s, 1), stack(new_s, 2), stack(new_s, 3), stack(new_s, 4))
```

```python
import functools
import math

import jax
import jax.numpy as jnp
import numpy as np
from jax import lax
from jax.experimental import pallas as pl
from jax.experimental.pallas import tpu as pltpu

F32 = jnp.float32
BF16 = jnp.bfloat16
I32 = jnp.int32

EPS = 1e-6
SSM_GROUP = 16
SSM_N = 64
N_HEADS = 8
HEAD_DIM = 64
IDX_HEADS = 8
IDX_DIM = 64
INDEX_TOPK = 256
PAGE_SIZE = 128
IDX_SCALE = IDX_DIM ** -0.5 * IDX_HEADS ** -0.5
ATT_SCALE = HEAD_DIM ** -0.5

LANES = 128
SUBLANES = 8
VMEM_LIMIT_BYTES = 56 * 1024 * 1024
CHUNK = 8
OCTET = LANES // SSM_GROUP
QB = 256
NEG_BIG = float(np.finfo(np.float32).min)
FLOAT_MID_ITERS = 40
SEARCH_PROBES_PER_TEST = 4
MIN_NORMAL_KEY = 0x00800000


def _cparams(*sem):
    return pltpu.CompilerParams(dimension_semantics=sem, vmem_limit_bytes=VMEM_LIMIT_BYTES)


def _const_spec(shape):
    nd = len(shape)
    return pl.BlockSpec(shape, lambda *_: (0,) * nd, pipeline_mode=pl.Buffered(1))


def _rms(x, g):
    ms = jnp.mean(x * x, axis=-1, keepdims=True)
    return x * lax.rsqrt(ms + EPS) * g


def _dot(a, b):
    return jnp.dot(a, b, preferred_element_type=F32)


def _dot_nt(a, b):
    return lax.dot_general(a, b, (((1,), (1,)), ((), ())), preferred_element_type=F32)


def _mod_kernel(c_ref, w_ref, b_ref, o_ref):
    a = jax.nn.silu(c_ref[...]).astype(BF16)
    o_ref[...] = _dot(a, w_ref[...].astype(BF16)) + b_ref[...]


def _modulation(c_all, mod_w, mod_b):
    depth, d, n = mod_w.shape
    r = c_all.shape[0]
    tn = 1024
    return pl.pallas_call(
        _mod_kernel,
        grid=(depth, n // tn),
        in_specs=[
            pl.BlockSpec((r, d), lambda l, j: (0, 0)),
            pl.BlockSpec((None, d, tn), lambda l, j: (l, 0, j)),
            pl.BlockSpec((None, 1, tn), lambda l, j: (l, 0, j)),
        ],
        out_specs=pl.BlockSpec((None, r, tn), lambda l, j: (l, 0, j)),
        out_shape=jax.ShapeDtypeStruct((depth, r, n), F32),
        compiler_params=_cparams("parallel", "parallel"),
        name="modulation",
    )(c_all, mod_w, mod_b.reshape(depth, 1, n))


def _mod_spec(nb, d, rows_per_batch_block):
    return pl.BlockSpec((9, None, nb, d), lambda i: (0, i // rows_per_batch_block, 0, 0))


def _ffn_kernel(x_ref, mod_ref, npre_ref, npost_ref, win_ref, wout_ref, o_ref, h_ref, a_ref, *, k, dff, ck):
    x = x_ref[...]
    sh, sc, gt = mod_ref[3 * k], mod_ref[3 * k + 1], mod_ref[3 * k + 2]
    h = _rms(x, npre_ref[k:k + 1, :]) * (1.0 + sc) + sh
    h_ref[...] = h.astype(BF16)
    for c in range(dff // ck):
        hb = h_ref[...]
        g = _dot(hb, win_ref[:, c * ck:(c + 1) * ck])
        u = _dot(hb, win_ref[:, dff + c * ck:dff + (c + 1) * ck])
        a_ref[:, c * ck:(c + 1) * ck] = (jax.nn.silu(g) * u).astype(BF16)
    y = _dot(a_ref[...], wout_ref[...])
    o_ref[...] = x + 0.5 * gt * _rms(y, npost_ref[k:k + 1, :])


def _ffn(x, mod, npre, npost, w_in, w_out, *, k, bm, blocks_per_batch):
    m, d = x.shape
    dff = w_out.shape[0]
    nb = mod.shape[2]
    kern = functools.partial(_ffn_kernel, k=k, dff=dff, ck=256)
    return pl.pallas_call(
        kern,
        grid=(m // bm,),
        in_specs=[
            pl.BlockSpec((bm, d), lambda i: (i, 0)),
            _mod_spec(nb, d, blocks_per_batch),
            _const_spec(npre.shape),
            _const_spec(npost.shape),
            _const_spec(w_in.shape),
            _const_spec(w_out.shape),
        ],
        out_specs=pl.BlockSpec((bm, d), lambda i: (i, 0)),
        out_shape=jax.ShapeDtypeStruct((m, d), F32),
        scratch_shapes=[pltpu.VMEM((bm, d), BF16), pltpu.VMEM((bm, dff), BF16)],
        compiler_params=_cparams("parallel"),
        name=f"ffn{k}",
    )(x, mod, npre, npost, w_in, w_out)


def _proj_prompt_kernel(x_ref, mod_ref, npre_ref, wrow_ref, wt_ref,
                        u_ref, kh_ref, kidxb_ref, kt_ref, vt_ref, kit_ref, v3_ref, qt_ref, qit_ref, wt_out_ref,
                        *, d_ssm, d_att):
    x = x_ref[...]
    sh, sc = mod_ref[3], mod_ref[4]
    h = (_rms(x, npre_ref[1:2, :]) * (1.0 + sc) + sh).astype(BF16)
    o_k = d_ssm + d_att
    o_ki = d_ssm + 3 * d_att + IDX_HEADS * IDX_DIM
    u_ref[...] = _dot(h, wrow_ref[:, 0:d_ssm])
    k = _dot(h, wrow_ref[:, o_k:o_k + d_att])
    kidxb_ref[...] = _dot(h, wrow_ref[:, o_ki:o_ki + IDX_DIM]).astype(BF16)
    for hh in range(N_HEADS):
        kh_ref[hh] = k[:, hh * HEAD_DIM:(hh + 1) * HEAD_DIM].astype(BF16)
    pt = _dot_nt(wt_ref[...], h)
    o = 0
    qt_ref[...] = (pt[o:o + d_att] * ATT_SCALE).astype(BF16)
    o += d_att
    kt_ref[...] = pt[o:o + d_att]
    o += d_att
    vt = pt[o:o + d_att]
    vt_ref[...] = vt
    vtb = vt.astype(BF16)
    for jj in range(v3_ref.shape[0]):
        v3_ref[jj] = vtb[:, jj * QB:(jj + 1) * QB]
    o += d_att
    qit_ref[...] = pt[o:o + IDX_HEADS * IDX_DIM].astype(BF16)
    o += IDX_HEADS * IDX_DIM
    kit_ref[...] = pt[o:o + IDX_DIM]
    o += IDX_DIM
    wt_out_ref[...] = pt[o:o + IDX_HEADS] * IDX_SCALE


def _proj_prompt(x, mod, npre, w_row, w_t, *, batch, bm, blocks_per_batch, d_ssm, d_att):
    m, d = x.shape
    t_len = m // batch
    nbt = blocks_per_batch
    row = lambda w: pl.BlockSpec((bm, w), lambda i: (i, 0))
    col = lambda r: pl.BlockSpec((r, bm), lambda i: (0, i))
    bcol = lambda r: pl.BlockSpec((None, r, bm), lambda i: (i // nbt, 0, i % nbt))
    kern = functools.partial(_proj_prompt_kernel, d_ssm=d_ssm, d_att=d_att)
    return pl.pallas_call(
        kern,
        grid=(m // bm,),
        in_specs=[
            pl.BlockSpec((bm, d), lambda i: (i, 0)),
            _mod_spec(1, d, blocks_per_batch),
            _const_spec(npre.shape),
            _const_spec(w_row.shape),
            _const_spec(w_t.shape),
        ],
        out_specs=[
            row(d_ssm),
            pl.BlockSpec((N_HEADS, bm, HEAD_DIM), lambda i: (0, i, 0)),
            row(IDX_DIM),
            bcol(d_att), bcol(d_att), bcol(IDX_DIM),
            pl.BlockSpec((bm // QB, d_att, QB), lambda i: (i, 0, 0)),
            col(d_att), col(d_att), col(IDX_HEADS),
        ],
        out_shape=[
            jax.ShapeDtypeStruct((m, d_ssm), F32),
            jax.ShapeDtypeStruct((N_HEADS, m, HEAD_DIM), BF16),
            jax.ShapeDtypeStruct((m, IDX_DIM), BF16),
            jax.ShapeDtypeStruct((batch, d_att, t_len), F32),
            jax.ShapeDtypeStruct((batch, d_att, t_len), F32),
            jax.ShapeDtypeStruct((batch, IDX_DIM, t_len), F32),
            jax.ShapeDtypeStruct((m // QB, d_att, QB), BF16),
            jax.ShapeDtypeStruct((d_att, m), BF16),
            jax.ShapeDtypeStruct((d_att, m), BF16),
            jax.ShapeDtypeStruct((IDX_HEADS, m), F32),
        ],
        compiler_params=_cparams("parallel"),
        name="proj_prompt",
    )(x, mod, npre, w_row, w_t)


def _proj_sample_kernel(x_ref, mod_ref, npre_ref, w_ref, o_ref):
    x = x_ref[...]
    sh, sc = mod_ref[3], mod_ref[4]
    h = (_rms(x, npre_ref[1:2, :]) * (1.0 + sc) + sh).astype(BF16)
    o_ref[...] = _dot(h, w_ref[...])


def _proj_sample(x, mod, npre, w):
    m, d = x.shape
    n = w.shape[1]
    return pl.pallas_call(
        _proj_sample_kernel,
        grid=(1,),
        in_specs=[
            pl.BlockSpec((m, d), lambda i: (0, 0)),
            _mod_spec(m, d, 1),
            _const_spec(npre.shape),
            _const_spec(w.shape),
        ],
        out_specs=pl.BlockSpec((m, n), lambda i: (0, 0)),
        out_shape=jax.ShapeDtypeStruct((m, n), F32),
        compiler_params=_cparams("arbitrary"),
        name="proj_sample",
    )(x, mod, npre, w)


def _ssm_prep_kernel(ldt_ref, ara_ref, aia_ref, arb_ref, aib_ref, btre_ref, btim_ref, cre_ref, cim_ref,
                     ctre_ref, ctim_ref,
                     kk_ref, wre_ref, wim_ref, pre_ref, pim_ref, al_ref, ab_ref, bbre_ref, bbim_ref):
    dt = jnp.exp(ldt_ref[...])

    def powers(ar, ai, k):
        mag = jnp.exp(dt * ar * k)
        ph = dt * ai * k
        return mag * jnp.cos(ph), mag * jnp.sin(ph)

    ar, ai = ara_ref[...], aia_ref[...]
    abr, abi = powers(ar, ai, 1.0)
    den = ar * ar + ai * ai
    z_re = ((abr - 1.0) * ar + abi * ai) / den
    z_im = (abi * ar - (abr - 1.0) * ai) / den
    b_re, b_im = btre_ref[...], btim_ref[...]
    bb_re = z_re * b_re - z_im * b_im
    bb_im = z_re * b_im + z_im * b_re
    bbre_ref[...] = bb_re
    bbim_ref[...] = bb_im
    ab_ref[0] = abr
    ab_ref[1] = abi
    lr, li = powers(ar, ai, float(CHUNK))
    al_ref[0] = lr
    al_ref[1] = li
    c_re, c_im = cre_ref[...], cim_ref[...]
    ck_re, ck_im = [], []
    for k in range(CHUNK):
        pr, pi = powers(ar, ai, float(CHUNK - 1 - k))
        wre_ref[k] = pr * bb_re - pi * bb_im
        wim_ref[k] = pr * bb_im + pi * bb_re
        pr, pi = powers(ar, ai, float(k))
        ck_re.append(c_re * pr - c_im * pi)
        ck_im.append(c_re * pi + c_im * pr)
    bnt = functools.partial(lax.dot_general, dimension_numbers=(((2,), (2,)), ((0,), (0,))),
                            precision=lax.Precision.HIGHEST, preferred_element_type=F32)
    kk_ref[...] = (bnt(bb_re, jnp.concatenate(ck_re, axis=1)) - bnt(bb_im, jnp.concatenate(ck_im, axis=1)))

    ar, ai = arb_ref[...], aib_ref[...]
    c_re, c_im = ctre_ref[...], ctim_ref[...]
    for t in range(CHUNK):
        pr, pi = powers(ar, ai, float(t + 1))
        pre_ref[t] = c_re * pr - c_im * pi
        pim_ref[t] = -(c_re * pi + c_im * pr)


def _ssm_prep(log_dt, a_re, a_im, b_re, b_im, c_re, c_im):
    g, n = a_re.shape
    j = SSM_GROUP
    shp = lambda *s: jax.ShapeDtypeStruct(s, F32)
    swap = lambda a: a.transpose(0, 2, 1)
    return pl.pallas_call(
        _ssm_prep_kernel,
        out_shape=[shp(g, j, CHUNK * j), shp(CHUNK, g, j, n), shp(CHUNK, g, j, n),
                   shp(CHUNK, g, n, j), shp(CHUNK, g, n, j), shp(2, g, 1, n), shp(2, g, 1, n),
                   shp(g, j, n), shp(g, j, n)],
        compiler_params=pltpu.CompilerParams(vmem_limit_bytes=VMEM_LIMIT_BYTES),
        name="ssm_prep",
    )(log_dt.reshape(g, 1, 1), a_re.reshape(g, 1, n), a_im.reshape(g, 1, n),
      a_re.reshape(g, n, 1), a_im.reshape(g, n, 1), swap(b_re), swap(b_im), c_re, c_im, swap(c_re), swap(c_im))


def _block_diag(x, groups):
    rows, c = x.shape[-2:]
    keep = (np.arange(rows)[:, None] // (rows // groups)) == (np.arange(groups * c)[None, :] // c)
    return jnp.where(keep, jnp.tile(x, (1,) * (x.ndim - 1) + (groups,)), 0.0)


def _ssm_operators(prep, c_re, c_im):
    kk, w_re, w_im, p_re, p_im, al, ab, bb_re, bb_im = prep
    g, j, n = bb_re.shape
    no, kw = g // OCTET, CHUNK * LANES
    x = kk.reshape(no, OCTET, j, CHUNK, j).transpose(0, 3, 1, 2, 4).reshape(no, CHUNK, LANES, j)
    tiles = _block_diag(x, OCTET).astype(BF16)
    lag = np.arange(CHUNK)[None, :] - np.arange(CHUNK)[:, None]
    toep = jnp.where((lag >= 0)[None, :, :, None, None], tiles[:, np.clip(lag, 0, None)], 0)
    m_op = toep.transpose(0, 1, 3, 2, 4).reshape(no, kw, kw)
    x = jnp.stack([w_re, w_im], 0).reshape(2, CHUNK, no, LANES, n)
    w_op = _block_diag(x, OCTET).astype(BF16).transpose(2, 1, 3, 0, 4).reshape(no, kw, 2 * OCTET * n)
    x = jnp.stack([p_re, p_im], 0).reshape(2, CHUNK, no, OCTET * n, j)
    p_op = _block_diag(x, OCTET).astype(BF16).transpose(2, 0, 3, 1, 4).reshape(no, 2 * OCTET * n, kw)
    a_chunk = al.reshape(2, no, OCTET * n).transpose(1, 0, 2)
    b_step = _block_diag(jnp.stack([bb_re, bb_im], 0).reshape(2, g * j, n), g)
    b_step = b_step.transpose(1, 0, 2).reshape(g * j, 2 * g * n)
    c_t = jnp.stack([c_re, -c_im], 0).transpose(0, 1, 3, 2).reshape(2, g * n, j)
    c_step = _block_diag(c_t, g).reshape(2 * g * n, g * j).astype(BF16)
    return dict(m=m_op, w=w_op, p=p_op, a_chunk=a_chunk, b_step=b_step, c_step=c_step, abar=ab.reshape(2, g * n))


def _ssm_prompt_kernel(u_ref, m_ref, w_ref, p_ref, al_ref, d_ref, y_ref, sfin_ref,
                       uo_ref, v_ref, sc_ref, *, r):
    half = sc_ref.shape[1] // 2
    for tau in range(CHUNK):
        uo_ref[:, tau * LANES:(tau + 1) * LANES] = u_ref[pl.ds(tau, r, stride=CHUNK), :].astype(BF16)
    uo = uo_ref[...]
    v_ref[...] = _dot(uo, w_ref[...])
    a_r, a_i = al_ref[0:1, :], al_ref[1:2, :]

    def step(c, carry):
        s_r, s_i = carry
        sc_ref[pl.ds(c, 1), 0:half] = s_r
        sc_ref[pl.ds(c, 1), half:2 * half] = s_i
        v = v_ref[pl.ds(c, 1), :]
        return (a_r * s_r - a_i * s_i + v[:, 0:half], a_r * s_i + a_i * s_r + v[:, half:2 * half])

    zero = jnp.zeros((1, half), F32)
    s_r, s_i = lax.fori_loop(0, r, step, (zero, zero))
    sfin_ref[0:1, :] = s_r
    sfin_ref[1:2, :] = s_i
    y = _dot(uo, m_ref[...]) + _dot(sc_ref[...].astype(BF16), p_ref[...])
    d = d_ref[...]
    for t in range(CHUNK):
        rows = pl.ds(t, r, stride=CHUNK)
        y_ref[rows, :] = y[:, t * LANES:(t + 1) * LANES] + d * u_ref[rows, :]


def _ssm_prompt(u, ops, d_skip, *, batch, t_len):
    m, d_ssm = u.shape
    no = d_ssm // LANES
    r = t_len // CHUNK
    kw = CHUNK * LANES
    sw = ops["w"].shape[2]
    op_spec = lambda a, b: pl.BlockSpec((None, a, b), lambda o, bb: (o, 0, 0))
    y, sfin = pl.pallas_call(
        functools.partial(_ssm_prompt_kernel, r=r),
        grid=(no, batch),
        in_specs=[
            pl.BlockSpec((t_len, LANES), lambda o, bb: (bb, o)),
            op_spec(kw, kw), op_spec(kw, sw), op_spec(sw, kw),
            pl.BlockSpec((None, 2, sw // 2), lambda o, bb: (o, 0, 0)),
            pl.BlockSpec((1, LANES), lambda o, bb: (0, o)),
        ],
        out_specs=[
            pl.BlockSpec((t_len, LANES), lambda o, bb: (bb, o)),
            pl.BlockSpec((None, None, 2, sw // 2), lambda o, bb: (bb, o, 0, 0)),
        ],
        out_shape=[jax.ShapeDtypeStruct((m, d_ssm), F32),
                   jax.ShapeDtypeStruct((batch, no, 2, sw // 2), F32)],
        scratch_shapes=[pltpu.VMEM((r, kw), BF16), pltpu.VMEM((r, sw), F32), pltpu.VMEM((r, sw), F32)],
        compiler_params=_cparams("parallel", "parallel"),
        name="ssm_prompt",
    )(u, ops["m"], ops["w"], ops["p"], ops["a_chunk"], d_skip.reshape(1, d_ssm))
    s = sfin.reshape(batch, no, 2, OCTET, SSM_N).transpose(2, 0, 1, 3, 4).reshape(2, batch, no * OCTET, SSM_N)
    return y, s[0], s[1]


def _ssm_sample_kernel(u_ref, h_ref, ab_ref, bstep_ref, cstep_ref, d_ref, y_ref, s_ref):
    u = u_ref[...]
    half = h_ref.shape[2]
    bu = jnp.dot(u, bstep_ref[...], precision=lax.Precision.HIGHEST, preferred_element_type=F32)
    a_r, a_i = ab_ref[0:1, :], ab_ref[1:2, :]
    h_r, h_i = h_ref[0], h_ref[1]
    s_r = a_r * h_r - a_i * h_i + bu[:, 0:half]
    s_i = a_r * h_i + a_i * h_r + bu[:, half:2 * half]
    s_ref[0] = s_r
    s_ref[1] = s_i
    s = jnp.concatenate([s_r, s_i], axis=1).astype(BF16)
    y_ref[...] = _dot(s, cstep_ref[...]) + d_ref[...] * u


def _ssm_sample(u, h_re, h_im, ops, c_step, d_skip):
    bsz, d_ssm = u.shape
    gn = h_re.shape[1] * h_re.shape[2]
    h = jnp.stack([h_re.reshape(bsz, gn), h_im.reshape(bsz, gn)], 0)
    y, s = pl.pallas_call(
        _ssm_sample_kernel,
        out_shape=[jax.ShapeDtypeStruct((bsz, d_ssm), F32), jax.ShapeDtypeStruct((2, bsz, gn), F32)],
        compiler_params=pltpu.CompilerParams(vmem_limit_bytes=VMEM_LIMIT_BYTES),
        name="ssm_sample",
    )(u, h, ops["abar"], ops["b_step"], c_step, d_skip.reshape(1, d_ssm))
    return y, s[0].reshape(h_re.shape), s[1].reshape(h_im.shape)


def _f2key(x):
    b = lax.bitcast_convert_type(x, I32)
    return b ^ ((b >> 31) & 0x7FFFFFFF)


def _key2f(k):
    return lax.bitcast_convert_type(k ^ ((k >> 31) & 0x7FFFFFFF), F32)


def _search_init(mn, mx, n_valid, topk):
    z = jnp.zeros_like(n_valid)
    half = topk + 0.5
    return (_f2key(mn), _f2key(mx) + 1, n_valid.astype(F32) - half, jnp.full(mn.shape, half, F32),
            jnp.full(mn.shape, NEG_BIG, F32), jnp.where(n_valid > topk, 0, 1).astype(I32), z, z)


def _search_probe(state, it, topk):
    lo, hi, f_lo, f_hi = state[:4]
    lo_f, hi_f = _key2f(lo), _key2f(hi)
    frac = jnp.where(it % 4 == 3, 0.5, f_lo / (f_lo + f_hi))
    cand = _f2key(lo_f + (hi_f - lo_f) * frac)
    mid_k = (lo >> 1) + (hi >> 1) + (lo & hi & 1)
    inside = (cand > lo) & (cand < hi) & (it < FLOAT_MID_ITERS)
    probe = jnp.where(inside, cand, mid_k)
    zero_k = jnp.where(it == 0, 0, MIN_NORMAL_KEY)
    return jnp.where((it < 2) & (zero_k > lo) & (zero_k < hi), zero_k, probe)


def _search_update(state, mid, cnt, topk):
    lo, hi, f_lo, f_hi, thr, done, tie, last = state
    hit = cnt == topk
    up, dn = cnt > topk, cnt < topk
    lo_n, hi_n = jnp.where(up, mid, lo), jnp.where(dn, mid, hi)
    adj = ((hi_n == lo_n + 1) | ((lo_n == 0) & (hi_n == MIN_NORMAL_KEY))) & jnp.logical_not(hit)
    fin = hit | adj
    thr_n = jnp.where(hit, _key2f(mid), _key2f(lo_n))
    act = done == 0
    lo = jnp.where(act, lo_n, lo)
    hi = jnp.where(act, hi_n, hi)
    miss = cnt.astype(F32) - (topk + 0.5)
    f_lo = jnp.where(act, jnp.where(up, miss, jnp.where(dn & (last == -1), 0.5 * f_lo, f_lo)), f_lo)
    f_hi = jnp.where(act, jnp.where(dn, -miss, jnp.where(up & (last == 1), 0.5 * f_hi, f_hi)), f_hi)
    last = jnp.where(act, jnp.where(up, 1, jnp.where(dn, -1, last)), last)
    thr = jnp.where(act & fin, thr_n, thr)
    tie = jnp.where(act & adj, 1, tie)
    done = jnp.where(act & fin, 1, done)
    return lo, hi, f_lo, f_hi, thr, done, tie, last


def _search(count_ge, init, topk):
    def pending(state, it):
        return jnp.logical_and(jnp.min(state[5].astype(F32)) == 0.0, it < 96)

    def probe(u, carry):
        it, state = carry
        mid = _search_probe(state, it, topk)
        return it + 1, _search_update(state, mid, count_ge(_key2f(mid)), topk)

    def body(carry):
        it, state = lax.fori_loop(0, SEARCH_PROBES_PER_TEST, probe, carry[1:])
        return pending(state, it), it, state

    _, _, state = lax.while_loop(lambda c: c[0], body, (pending(init, 0), jnp.int32(0), init))
    return state[4], state[6]


def _attn_prompt_kernel(qt_ref, qit_ref, wt_ref, kh_ref, kidx_ref, v3_ref, o_ref, sc_ref, lg_ref, acc_ref, *, topk):
    i = pl.program_id(1)
    nk = i + 1
    qpos = i * QB + lax.broadcasted_iota(I32, (QB, QB), 1)
    krow = lax.broadcasted_iota(I32, (QB, QB), 0)
    fold = lambda x: x.reshape(QB // SUBLANES, SUBLANES, QB)

    tile_rows = lambda j: pl.ds(pl.multiple_of(j * QB, QB), QB)
    rep = lambda x: jnp.broadcast_to(x, (SUBLANES, QB))

    def score_tile(j, carry):
        mn, mx = carry
        kx = kidx_ref[tile_rows(j), :]
        acc = jnp.zeros((QB, QB), F32)
        for h in range(IDX_HEADS):
            s = _dot(kx, qit_ref[h * IDX_DIM:(h + 1) * IDX_DIM, :])
            acc = acc + jnp.maximum(s, 0.0) * wt_ref[h:h + 1, :]
        valid = (j * QB + krow) <= qpos
        sc = jnp.where(valid, acc, -jnp.inf)
        sc_ref[j] = sc
        mx = jnp.maximum(mx, jnp.max(fold(sc), axis=0))
        mn = jnp.minimum(mn, jnp.min(fold(jnp.where(valid, acc, jnp.inf)), axis=0))
        return mn, mx

    mn, mx = lax.fori_loop(0, nk, score_tile,
                           (jnp.full((SUBLANES, QB), jnp.inf, F32), jnp.full((SUBLANES, QB), -jnp.inf, F32)))
    mn = rep(jnp.min(mn, axis=0, keepdims=True))
    mx = rep(jnp.max(mx, axis=0, keepdims=True))

    def count_ge(t):
        t1 = t[0:1, :]

        def body(j, c):
            return c + jnp.sum(fold((sc_ref[j] >= t1).astype(I32)), axis=0)

        c = lax.fori_loop(0, nk, body, jnp.zeros((SUBLANES, QB), I32))
        return rep(jnp.sum(c, axis=0, keepdims=True))

    n_valid = qpos[0:SUBLANES, :] + 1
    thr, tie = _search(count_ge, _search_init(mn, mx, n_valid, topk), topk)

    @pl.when(jnp.max(tie) > 0)
    def _():
        tri = (lax.broadcasted_iota(I32, (QB, QB), 0) >= lax.broadcasted_iota(I32, (QB, QB), 1)).astype(BF16)
        thr1, tie1 = thr[0:1, :], tie[0:1, :] > 0

        def gt_tile(j, c):
            return c + jnp.sum((sc_ref[j] > thr1).astype(I32), axis=0, keepdims=True)

        quota = topk - lax.fori_loop(0, nk, gt_tile, jnp.zeros((1, QB), I32))

        def tie_tile(j, before):
            x = sc_ref[j]
            t = (x == thr1) & tie1
            rank = before + _dot(tri, t.astype(BF16)).astype(I32)
            sc_ref[j] = jnp.where(t & (rank > quota), -jnp.inf, x)
            return before + jnp.sum(t.astype(I32), axis=0, keepdims=True)

        lax.fori_loop(0, nk, tie_tile, jnp.zeros((1, QB), I32))

    thr1 = thr[0:1, :]
    heads = range(N_HEADS)
    hslice = lambda h: slice(h * HEAD_DIM, (h + 1) * HEAD_DIM)

    def logit_tile(j, ms):
        bias = jnp.where(sc_ref[j] >= thr1, 0.0, -jnp.inf)
        rows = tile_rows(j)
        out = []
        for h in heads:
            lg = _dot(kh_ref[h, rows, :], qt_ref[hslice(h), :]) + bias
            lg_ref[h, j] = lg
            out.append(jnp.maximum(ms[h], jnp.max(fold(lg), axis=0)))
        return tuple(out)

    ms = lax.fori_loop(0, nk, logit_tile, tuple(jnp.full((SUBLANES, QB), -jnp.inf, F32) for _ in heads))
    m1 = [jnp.max(m, axis=0, keepdims=True) for m in ms]
    acc_ref[...] = jnp.zeros(acc_ref.shape, F32)

    def pv_tile(j, ls):
        out = []
        for h in heads:
            p = jnp.exp(lg_ref[h, j] - m1[h])
            acc_ref[h] += _dot(v3_ref[j, hslice(h), :], p.astype(BF16))
            out.append(ls[h] + jnp.sum(fold(p), axis=0))
        return tuple(out)

    ls = lax.fori_loop(0, nk, pv_tile, tuple(jnp.zeros((SUBLANES, QB), F32) for _ in heads))
    outs = [acc_ref[h] / jnp.sum(ls[h], axis=0, keepdims=True) for h in heads]
    o_ref[...] = jnp.transpose(jnp.concatenate(outs, axis=0)).astype(BF16)


def _attn_prompt(qt, qit, wt, kh, kidxb, v3, *, batch, t_len, topk):
    d_att, m = qt.shape
    nq = t_len // QB
    return pl.pallas_call(
        functools.partial(_attn_prompt_kernel, topk=topk),
        grid=(batch, nq),
        in_specs=[
            pl.BlockSpec((d_att, QB), lambda b, i: (0, b * nq + i)),
            pl.BlockSpec((d_att, QB), lambda b, i: (0, b * nq + i)),
            pl.BlockSpec((IDX_HEADS, QB), lambda b, i: (0, b * nq + i)),
            pl.BlockSpec((N_HEADS, t_len, HEAD_DIM), lambda b, i: (0, b, 0)),
            pl.BlockSpec((t_len, IDX_DIM), lambda b, i: (b, 0)),
            pl.BlockSpec((nq, d_att, QB), lambda b, i: (b, 0, 0)),
        ],
        out_specs=pl.BlockSpec((QB, d_att), lambda b, i: (b * nq + i, 0)),
        out_shape=jax.ShapeDtypeStruct((m, d_att), BF16),
        scratch_shapes=[pltpu.VMEM((nq, QB, QB), F32), pltpu.VMEM((N_HEADS, nq, QB, QB), F32),
                        pltpu.VMEM((N_HEADS, HEAD_DIM, QB), F32)],
        compiler_params=_cparams("parallel", "arbitrary"),
        name="attn_prompt",
    )(qt, qit, wt, kh, kidxb, v3)


def _idx_sample_kernel(pt_ref, qi_ref, w_ref, kin_ref, ci_ref, o_ref, ibuf, sem, *, layer, n_pages):
    b = pl.program_id(0)
    slot = b % 2

    def page_copy(buf_slot, p, page):
        lanes = pl.ds(pl.multiple_of(p * PAGE_SIZE, PAGE_SIZE), PAGE_SIZE)
        return pltpu.make_async_copy(ci_ref.at[layer, page], ibuf.at[buf_slot, :, lanes], sem.at[buf_slot])

    def start_all(sample, buf_slot):
        def start(p, _):
            page_copy(buf_slot, p, pt_ref[sample, p]).start()
            return 0
        lax.fori_loop(0, n_pages, start, 0)

    @pl.when(b == 0)
    def _():
        start_all(b, slot)

    @pl.when(b + 1 < pl.num_programs(0))
    def _():
        start_all(b + 1, 1 - slot)

    def wait(p, _):
        page_copy(slot, p, 0).wait()
        return 0

    lax.fori_loop(0, n_pages, wait, 0)

    qi = qi_ref[...].astype(BF16)
    w = w_ref[...] * IDX_SCALE
    n_past = n_pages * PAGE_SIZE
    step = SUBLANES * PAGE_SIZE
    for c in range(n_past // step):
        cols = slice(c * step, (c + 1) * step)
        s = _dot(qi, ibuf[slot, :, cols].astype(BF16))
        o_ref[:, cols] = jnp.sum(jnp.maximum(s, 0.0) * w, axis=0, keepdims=True)
    s_new = jnp.sum(qi.astype(F32) * kin_ref[...].astype(BF16).astype(F32), axis=1, keepdims=True)
    s_new = jnp.sum(jnp.maximum(s_new, 0.0) * w, axis=0, keepdims=True)
    lane = lax.broadcasted_iota(I32, (1, PAGE_SIZE), 1)
    o_ref[:, n_past:n_past + PAGE_SIZE] = jnp.where(lane == 0, s_new, -jnp.inf)


def _idx_sample(page_table, q_idx, w_idx, k_idx_new, cache_idx_t, *, layer):
    bsz, n_pages = page_table.shape
    width = (n_pages + 1) * PAGE_SIZE
    blk = lambda r, c: pl.BlockSpec((None, r, c), lambda b, pt: (b, 0, 0))
    return pl.pallas_call(
        functools.partial(_idx_sample_kernel, layer=layer, n_pages=n_pages),
        grid_spec=pltpu.PrefetchScalarGridSpec(
            num_scalar_prefetch=1,
            grid=(bsz,),
            in_specs=[blk(IDX_HEADS, IDX_DIM), blk(IDX_HEADS, 1), blk(1, IDX_DIM), pl.BlockSpec(memory_space=pl.ANY)],
            out_specs=blk(1, width),
            scratch_shapes=[pltpu.VMEM((2, IDX_DIM, n_pages * PAGE_SIZE), F32), pltpu.SemaphoreType.DMA((2,))],
        ),
        out_shape=jax.ShapeDtypeStruct((bsz, 1, width), F32),
        compiler_params=_cparams("arbitrary"),
        name="idx_sample",
    )(page_table, q_idx, w_idx, k_idx_new, cache_idx_t)


def _threshold_sample_kernel(sc_ref, thr_ref, tie_ref, *, n_keys, topk):
    sc = sc_ref[...]
    mx = jnp.max(sc, axis=1, keepdims=True)
    mn = jnp.min(jnp.where(sc == -jnp.inf, jnp.inf, sc), axis=1, keepdims=True)

    def count_ge(t):
        return jnp.sum((sc_ref[...] >= t).astype(I32), axis=1, keepdims=True)

    n = jnp.full(mn.shape, n_keys, I32)
    thr, tie = _search(count_ge, _search_init(mn, mx, n, topk), topk)
    thr_ref[...] = jnp.broadcast_to(thr, thr_ref.shape)
    tie_ref[...] = jnp.broadcast_to(tie, tie_ref.shape)


def _threshold_sample(scores, *, n_keys, topk):
    bsz = scores.shape[0]
    return pl.pallas_call(
        functools.partial(_threshold_sample_kernel, n_keys=n_keys, topk=topk),
        out_shape=[jax.ShapeDtypeStruct((bsz, LANES), F32), jax.ShapeDtypeStruct((bsz, LANES), I32)],
        compiler_params=pltpu.CompilerParams(vmem_limit_bytes=VMEM_LIMIT_BYTES),
        name="threshold_sample",
    )(scores)


def _attend_sample_kernel(pt_ref, sc_ref, snew_ref, thr_ref, tie_ref, q_ref, kn_ref, vn_ref, ck_ref, cv_ref, o_ref,
                          kbuf, vbuf, qb, lg_ref, sem_k, sem_v, *, layer, n_pages, topk):
    b = pl.program_id(0)

    def copies(src_ref, buf, sem, p, page):
        return pltpu.make_async_copy(src_ref.at[layer, page], buf.at[p], sem)

    streams = ((ck_ref, kbuf, sem_k), (cv_ref, vbuf, sem_v))

    def start_all(src_ref, buf, sem, sample):
        def start(p, _):
            copies(src_ref, buf, sem, p, pt_ref[sample, p]).start()
            return 0
        lax.fori_loop(0, n_pages, start, 0)

    @pl.when(b == 0)
    def _():
        start_all(*streams[0], b)

    start_all(*streams[1], b)

    def wait_all(src_ref, buf, sem):
        def wait(p, _):
            copies(src_ref, buf, sem, p, 0).wait()
            return 0
        lax.fori_loop(0, n_pages, wait, 0)

    for h in range(N_HEADS):
        qb[h] = jnp.broadcast_to(q_ref[h] * ATT_SCALE, (HEAD_DIM, PAGE_SIZE))
    dsum = lambda x: jnp.sum(x, axis=0, keepdims=True)
    sc = sc_ref[...]
    s_new, thr, tie = snew_ref[...], thr_ref[...], tie_ref[...]

    def total(x, op=jnp.sum):
        return op(op(x, axis=1, keepdims=True), axis=0, keepdims=True)

    upper = (lax.broadcasted_iota(I32, (PAGE_SIZE, PAGE_SIZE), 0)
             <= lax.broadcasted_iota(I32, (PAGE_SIZE, PAGE_SIZE), 1)).astype(BF16)
    lower = (lax.broadcasted_iota(I32, (n_pages, n_pages), 1)
             < lax.broadcasted_iota(I32, (n_pages, n_pages), 0)).astype(BF16)

    def flat_rank(mask):
        mb = mask.astype(BF16)
        incl = _dot(mb, upper)
        before = jnp.sum(_dot(lower, mb), axis=1, keepdims=True)
        return (incl + before).astype(I32)

    gt = sc > thr
    tied = tie > 0
    is_tie = (sc == thr) & tied
    quota = jnp.where(tied, topk - total(gt.astype(I32)) - (s_new > thr).astype(I32), topk)
    sel = gt | ((sc == thr) & (flat_rank(is_tie) <= quota))
    n_tie_past = total(is_tie.astype(I32))
    new_sel = (s_new > thr) | ((s_new == thr) & (n_tie_past < quota))

    wait_all(*streams[0])
    for h in range(N_HEADS):
        q_h = qb[h]

        def k_page(p, _, h=h, q_h=q_h):
            lg_ref[h, pl.ds(p, 1), :] = dsum(kbuf[p, h] * q_h)
            return 0

        lax.fori_loop(0, n_pages, k_page, 0, unroll=4)

    @pl.when(b + 1 < pl.num_programs(0))
    def _():
        start_all(*streams[0], b + 1)

    red = lambda x, op: op(x, axis=(1, 2), keepdims=True)
    lg = jnp.where(sel[None], lg_ref[...], -jnp.inf)
    lg_new = jnp.sum(q_ref[...] * ATT_SCALE * kn_ref[...], axis=1, keepdims=True)
    lg_new = jnp.where(new_sel[None], lg_new, -jnp.inf)
    m = jnp.maximum(red(lg, jnp.max), lg_new)
    p = jnp.exp(lg - m)
    e_new = jnp.exp(lg_new - m)
    denom = red(p, jnp.sum) + e_new
    lg_ref[...] = p / denom
    p_new = e_new / denom

    wait_all(*streams[1])
    for h in range(N_HEADS):
        def v_page(p, acc, h=h):
            return acc + vbuf[p, h] * lg_ref[h, pl.ds(p, 1), :]

        acc = lax.fori_loop(0, n_pages, v_page, jnp.zeros((HEAD_DIM, PAGE_SIZE), F32), unroll=4)
        o_ref[h] = jnp.sum(acc, axis=1, keepdims=True) + p_new[h] * vn_ref[h]


def _attend_sample(page_table, scores, s_new, thr, tie, q, k_new, v_new, cache_k_t, cache_v_t, *, layer, topk):
    bsz, n_pages = page_table.shape
    one = pl.BlockSpec((None, 1, 1), lambda b, pt: (b, 0, 0))
    hcol = pl.BlockSpec((None, N_HEADS, HEAD_DIM, 1), lambda b, pt: (b, 0, 0, 0))
    any_spec = pl.BlockSpec(memory_space=pl.ANY)
    return pl.pallas_call(
        functools.partial(_attend_sample_kernel, layer=layer, n_pages=n_pages, topk=topk),
        grid_spec=pltpu.PrefetchScalarGridSpec(
            num_scalar_prefetch=1,
            grid=(bsz,),
            in_specs=[pl.BlockSpec((None, n_pages, PAGE_SIZE), lambda b, pt: (b, 0, 0)), one, one, one,
                      hcol, hcol, hcol, any_spec, any_spec],
            out_specs=hcol,
            scratch_shapes=[pltpu.VMEM((n_pages, N_HEADS, HEAD_DIM, PAGE_SIZE), F32),
                            pltpu.VMEM((n_pages, N_HEADS, HEAD_DIM, PAGE_SIZE), F32),
                            pltpu.VMEM((N_HEADS, HEAD_DIM, PAGE_SIZE), F32),
                            pltpu.VMEM((N_HEADS, n_pages, PAGE_SIZE), F32),
                            pltpu.SemaphoreType.DMA(()), pltpu.SemaphoreType.DMA(())],
        ),
        out_shape=jax.ShapeDtypeStruct((bsz, N_HEADS, HEAD_DIM, 1), F32),
        compiler_params=_cparams("arbitrary"),
        name="attend_sample",
    )(page_table, scores, s_new, thr, tie, q, k_new, v_new, cache_k_t, cache_v_t)


def _mix_kernel(x_ref, mod_ref, npre_ref, npost_ref, ys_ref, at_ref, gw_ref, gv_ref, wba_ref, wg_ref, wo_ref,
                o_ref):
    x = x_ref[...]
    d = x.shape[1]
    sh, sc, gt = mod_ref[3], mod_ref[4], mod_ref[5]
    h = (_rms(x, npre_ref[1:2, :]) * (1.0 + sc) + sh).astype(BF16)
    gates = _dot(h, wg_ref[...])
    ys = ys_ref[...].astype(BF16)
    y_a = _dot(ys, gw_ref[...]) * jax.nn.sigmoid(_dot(ys, gv_ref[...]))
    y_b = _dot(at_ref[...].astype(BF16), wba_ref[...])
    mixed = jax.nn.sigmoid(gates[:, 0:d]) * y_a + jax.nn.sigmoid(gates[:, d:2 * d]) * y_b
    y = _dot(mixed.astype(BF16), wo_ref[...])
    o_ref[...] = x + gt * _rms(y, npost_ref[1:2, :])


def _mix(x, mod, npre, npost, y_ssm, attn, glu_w, glu_v, wba, w_gates, w_out, *, bm, blocks_per_batch):
    m, d = x.shape
    nb = mod.shape[2]
    return pl.pallas_call(
        _mix_kernel,
        grid=(m // bm,),
        in_specs=[
            pl.BlockSpec((bm, d), lambda i: (i, 0)),
            _mod_spec(nb, d, blocks_per_batch),
            _const_spec(npre.shape), _const_spec(npost.shape),
            pl.BlockSpec((bm, y_ssm.shape[1]), lambda i: (i, 0)),
            pl.BlockSpec((bm, attn.shape[1]), lambda i: (i, 0)),
            _const_spec(glu_w.shape), _const_spec(glu_v.shape), _const_spec(wba.shape),
            _const_spec(w_gates.shape), _const_spec(w_out.shape),
        ],
        out_specs=pl.BlockSpec((bm, d), lambda i: (i, 0)),
        out_shape=jax.ShapeDtypeStruct((m, d), F32),
        compiler_params=_cparams("parallel"),
        name="mix",
    )(x, mod, npre, npost, y_ssm, attn, glu_w, glu_v, wba, w_gates, w_out)


def _pad_cols(w, n):
    return jnp.pad(w, ((0, 0), (0, n - w.shape[1])))


def kernel(x_prompt, x_sample, cache_k, cache_v, cache_idx_k, state_ssm_re, state_ssm_im, page_table,
           c_prompt, c_sample, mod_w, mod_b, norm_pre, norm_post, ffn1_in, ffn1_out, w_in,
           ssm_log_dt, ssm_a_re, ssm_a_im, ssm_b_re, ssm_b_im, ssm_c_re, ssm_c_im, ssm_d,
           glu_w, glu_v, w_branch_attn, w_out, ffn2_in, ffn2_out):
    batch, t_len, d = x_prompt.shape
    dec_batch, dec_seq, _ = x_sample.shape
    depth = mod_w.shape[0]
    d_ssm = ssm_d.shape[1]
    d_att = N_HEADS * HEAD_DIM
    n_groups = d_ssm // SSM_GROUP
    assert dec_seq == 1 and t_len % QB == 0 and d_ssm % LANES == 0
    m = batch * t_len
    bm = 512 if m % 512 == 0 else QB
    topk_p = min(INDEX_TOPK, t_len // 4)
    n_pages = page_table.shape[1]
    n_past = n_pages * PAGE_SIZE
    assert n_pages % SUBLANES == 0
    topk_s = min(INDEX_TOPK, (n_past + dec_seq) // 4)

    mod = _modulation(jnp.concatenate([c_prompt, c_sample], 0), mod_w, mod_b).reshape(depth, -1, 9, d)

    widths = (d_ssm, d_att, d_att, d_att, IDX_HEADS * IDX_DIM, IDX_DIM, IDX_HEADS, d, d)
    off = np.concatenate([[0], np.cumsum(widths)])
    o_u, o_q, o_k, o_v, o_qi, o_ki, o_wi, o_ga, o_gb, o_end = (int(v) for v in off)

    cache_k_t = cache_k.transpose(0, 1, 3, 4, 2)
    cache_v_t = cache_v.transpose(0, 1, 3, 4, 2)
    cache_idx_t = cache_idx_k.transpose(0, 1, 3, 2)

    xp = x_prompt.reshape(m, d)
    xs = x_sample.reshape(dec_batch, d)
    new_p, new_s = [], []
    for l in range(depth):
        bf = lambda w: w[l].astype(BF16)
        modp = mod[l, :batch].transpose(1, 0, 2).reshape(9, batch, 1, d)
        mods = mod[l, batch:].transpose(1, 0, 2).reshape(9, 1, dec_batch, d)
        npre, npost = norm_pre[l], norm_post[l]
        f1_in, f1_out, f2_in, f2_out = bf(ffn1_in), bf(ffn1_out), bf(ffn2_in), bf(ffn2_out)
        wl = w_in[l]
        w_t = wl[:, o_q:o_ga].T.astype(BF16)
        w_gates = wl[:, o_ga:o_end].astype(BF16)
        w_all = _pad_cols(wl, 37 * LANES).astype(BF16)
        g_w, g_v, wba, wo = bf(glu_w), bf(glu_v), bf(w_branch_attn), bf(w_out)
        prep = _ssm_prep(ssm_log_dt[l], ssm_a_re[l], ssm_a_im[l], ssm_b_re[l], ssm_b_im[l],
                         ssm_c_re[l], ssm_c_im[l])
        ops = _ssm_operators(prep, ssm_c_re[l], ssm_c_im[l])

        xp = _ffn(xp, modp, npre, npost, f1_in, f1_out, k=0, bm=bm, blocks_per_batch=t_len // bm)
        u, kh, kidxb, k_t, v_t, kidx_t, v3, qt, qit, wt = _proj_prompt(
            xp, modp, npre, w_all, w_t, batch=batch, bm=bm, blocks_per_batch=t_len // bm, d_ssm=d_ssm, d_att=d_att)
        y_ssm, sp_re, sp_im = _ssm_prompt(u, ops, ssm_d[l], batch=batch, t_len=t_len)
        attn = _attn_prompt(qt, qit, wt, kh, kidxb, v3, batch=batch, t_len=t_len, topk=topk_p)
        xp = _mix(xp, modp, npre, npost, y_ssm, attn, g_w, g_v, wba, w_gates, wo,
                  bm=bm, blocks_per_batch=t_len // bm)
        xp = _ffn(xp, modp, npre, npost, f2_in, f2_out, k=2, bm=bm, blocks_per_batch=t_len // bm)
        new_p.append((k_t, v_t, kidx_t, sp_re, sp_im))

        xs = _ffn(xs, mods, npre, npost, f1_in, f1_out, k=0, bm=dec_batch, blocks_per_batch=1)
        pr = _proj_sample(xs, mods, npre, w_all)
        u_s, q_s, k_s, v_s = pr[:, o_u:o_q], pr[:, o_q:o_k], pr[:, o_k:o_v], pr[:, o_v:o_qi]
        qi_s, ki_s, wi_s = pr[:, o_qi:o_ki], pr[:, o_ki:o_wi], pr[:, o_wi:o_ga]
        y_ssm_s, ss_re, ss_im = _ssm_sample(u_s, state_ssm_re[l], state_ssm_im[l], ops, ops["c_step"], ssm_d[l])
        hcol = lambda a: a.reshape(dec_batch, N_HEADS, HEAD_DIM, 1)
        sc_s = _idx_sample(page_table, qi_s.reshape(dec_batch, IDX_HEADS, IDX_DIM),
                           wi_s.reshape(dec_batch, IDX_HEADS, 1), ki_s.reshape(dec_batch, 1, IDX_DIM),
                           cache_idx_t, layer=l)
        thr_s, tie_s = _threshold_sample(sc_s.reshape(dec_batch, -1), n_keys=n_past + 1, topk=topk_s)
        attn_s = _attend_sample(page_table, sc_s[:, 0, :n_past].reshape(dec_batch, n_pages, PAGE_SIZE),
                                sc_s[:, :, n_past:n_past + 1], thr_s[:, :1].reshape(dec_batch, 1, 1),
                                tie_s[:, :1].reshape(dec_batch, 1, 1), hcol(q_s), hcol(k_s), hcol(v_s),
                                cache_k_t, cache_v_t, layer=l, topk=topk_s)
        xs = _mix(xs, mods, npre, npost, y_ssm_s, attn_s.reshape(dec_batch, d_att), g_w, g_v, wba, w_gates, wo,
                  bm=dec_batch, blocks_per_batch=1)
        xs = _ffn(xs, mods, npre, npost, f2_in, f2_out, k=2, bm=dec_batch, blocks_per_batch=1)
        new_s.append((k_s.reshape(dec_batch, 1, N_HEADS, HEAD_DIM), v_s.reshape(dec_batch, 1, N_HEADS, HEAD_DIM),
                      ki_s.reshape(dec_batch, 1, IDX_DIM), ss_re, ss_im))

    stack = lambda states, i: jnp.stack([s[i] for s in states])
    heads_last = lambda a: a.reshape(depth, batch, N_HEADS, HEAD_DIM, t_len).transpose(0, 1, 4, 2, 3)
    return (xp.reshape(batch, t_len, d), xs.reshape(dec_batch, 1, d),
            heads_last(stack(new_p, 0)), heads_last(stack(new_p, 1)), stack(new_p, 2).transpose(0, 1, 3, 2),
            stack(new_p, 3), stack(new_p, 4),
            stack(new_s, 0), stack(new_s, 1), stack(new_s, 2), stack(new_s, 3), stack(new_s, 4))
```

```python
import functools
import math

import jax
import jax.numpy as jnp
import numpy as np
from jax import lax
from jax.experimental import pallas as pl
from jax.experimental.pallas import tpu as pltpu

F32 = jnp.float32
BF16 = jnp.bfloat16
I32 = jnp.int32

EPS = 1e-6
SSM_GROUP = 16
SSM_N = 64
N_HEADS = 8
HEAD_DIM = 64
IDX_HEADS = 8
IDX_DIM = 64
INDEX_TOPK = 256
PAGE_SIZE = 128
IDX_SCALE = IDX_DIM ** -0.5 * IDX_HEADS ** -0.5
ATT_SCALE = HEAD_DIM ** -0.5

LANES = 128
SUBLANES = 8
VMEM_LIMIT_BYTES = 56 * 1024 * 1024
CHUNK = 8
OCTET = LANES // SSM_GROUP
QB = 256
NEG_BIG = float(np.finfo(np.float32).min)
FLOAT_MID_ITERS = 40
SEARCH_PROBES_PER_TEST = 4
MIN_NORMAL_KEY = 0x00800000


def _cparams(*sem):
    return pltpu.CompilerParams(dimension_semantics=sem, vmem_limit_bytes=VMEM_LIMIT_BYTES)


def _const_spec(shape):
    nd = len(shape)
    return pl.BlockSpec(shape, lambda *_: (0,) * nd, pipeline_mode=pl.Buffered(1))


def _rms(x, g):
    ms = jnp.mean(x * x, axis=-1, keepdims=True)
    return x * lax.rsqrt(ms + EPS) * g


def _dot(a, b):
    return jnp.dot(a, b, preferred_element_type=F32)


def _dot_nt(a, b):
    return lax.dot_general(a, b, (((1,), (1,)), ((), ())), preferred_element_type=F32)


def _mod_kernel(c_ref, w_ref, b_ref, o_ref):
    a = jax.nn.silu(c_ref[...]).astype(BF16)
    o_ref[...] = _dot(a, w_ref[...].astype(BF16)) + b_ref[...]


def _modulation(c_all, mod_w, mod_b):
    depth, d, n = mod_w.shape
    r = c_all.shape[0]
    tn = 1024
    return pl.pallas_call(
        _mod_kernel,
        grid=(depth, n // tn),
        in_specs=[
            pl.BlockSpec((r, d), lambda l, j: (0, 0)),
            pl.BlockSpec((None, d, tn), lambda l, j: (l, 0, j)),
            pl.BlockSpec((None, 1, tn), lambda l, j: (l, 0, j)),
        ],
        out_specs=pl.BlockSpec((None, r, tn), lambda l, j: (l, 0, j)),
        out_shape=jax.ShapeDtypeStruct((depth, r, n), F32),
        compiler_params=_cparams("parallel", "parallel"),
        name="modulation",
    )(c_all, mod_w, mod_b.reshape(depth, 1, n))


def _mod_spec(nb, d, rows_per_batch_block):
    return pl.BlockSpec((9, None, nb, d), lambda i: (0, i // rows_per_batch_block, 0, 0))


def _ffn_kernel(x_ref, mod_ref, npre_ref, npost_ref, win_ref, wout_ref, o_ref, h_ref, a_ref, *, k, dff, ck):
    x = x_ref[...]
    sh, sc, gt = mod_ref[3 * k], mod_ref[3 * k + 1], mod_ref[3 * k + 2]
    h = _rms(x, npre_ref[k:k + 1, :]) * (1.0 + sc) + sh
    h_ref[...] = h.astype(BF16)
    for c in range(dff // ck):
        hb = h_ref[...]
        g = _dot(hb, win_ref[:, c * ck:(c + 1) * ck])
        u = _dot(hb, win_ref[:, dff + c * ck:dff + (c + 1) * ck])
        a_ref[:, c * ck:(c + 1) * ck] = (jax.nn.silu(g) * u).astype(BF16)
    y = _dot(a_ref[...], wout_ref[...])
    o_ref[...] = x + 0.5 * gt * _rms(y, npost_ref[k:k + 1, :])


def _ffn(x, mod, npre, npost, w_in, w_out, *, k, bm, blocks_per_batch):
    m, d = x.shape
    dff = w_out.shape[0]
    nb = mod.shape[2]
    kern = functools.partial(_ffn_kernel, k=k, dff=dff, ck=256)
    return pl.pallas_call(
        kern,
        grid=(m // bm,),
        in_specs=[
            pl.BlockSpec((bm, d), lambda i: (i, 0)),
            _mod_spec(nb, d, blocks_per_batch),
            _const_spec(npre.shape),
            _const_spec(npost.shape),
            _const_spec(w_in.shape),
            _const_spec(w_out.shape),
        ],
        out_specs=pl.BlockSpec((bm, d), lambda i: (i, 0)),
        out_shape=jax.ShapeDtypeStruct((m, d), F32),
        scratch_shapes=[pltpu.VMEM((bm, d), BF16), pltpu.VMEM((bm, dff), BF16)],
        compiler_params=_cparams("parallel"),
        name=f"ffn{k}",
    )(x, mod, npre, npost, w_in, w_out)


def _proj_prompt_kernel(x_ref, mod_ref, npre_ref, wrow_ref, wt_ref, *refs, d_ssm, d_att, layer):
    earlier = refs[:3] if layer else ()
    u_ref, kh_ref, kidxb_ref, kt_all, vt_all, kit_all, v3_ref, qt_ref, qit_ref, wt_out_ref = refs[len(earlier):]
    for src, dst in zip(earlier, (kt_all, vt_all, kit_all)):
        dst[0:layer] = src[...]
    kt_ref, vt_ref, kit_ref = kt_all.at[layer], vt_all.at[layer], kit_all.at[layer]
    x = x_ref[...]
    sh, sc = mod_ref[3], mod_ref[4]
    h = (_rms(x, npre_ref[1:2, :]) * (1.0 + sc) + sh).astype(BF16)
    o_k = d_ssm + d_att
    o_ki = d_ssm + 3 * d_att + IDX_HEADS * IDX_DIM
    u_ref[...] = _dot(h, wrow_ref[:, 0:d_ssm])
    k = _dot(h, wrow_ref[:, o_k:o_k + d_att])
    kidxb_ref[...] = _dot(h, wrow_ref[:, o_ki:o_ki + IDX_DIM]).astype(BF16)
    for hh in range(N_HEADS):
        kh_ref[hh] = k[:, hh * HEAD_DIM:(hh + 1) * HEAD_DIM].astype(BF16)
    pt = _dot_nt(wt_ref[...], h)
    o = 0
    qt_ref[...] = (pt[o:o + d_att] * ATT_SCALE).astype(BF16)
    o += d_att
    kt_ref[...] = pt[o:o + d_att]
    o += d_att
    vt = pt[o:o + d_att]
    vt_ref[...] = vt
    vtb = vt.astype(BF16)
    for jj in range(v3_ref.shape[0]):
        v3_ref[jj] = vtb[:, jj * QB:(jj + 1) * QB]
    o += d_att
    qit_ref[...] = pt[o:o + IDX_HEADS * IDX_DIM].astype(BF16)
    o += IDX_HEADS * IDX_DIM
    kit_ref[...] = pt[o:o + IDX_DIM]
    o += IDX_DIM
    wt_out_ref[...] = pt[o:o + IDX_HEADS] * IDX_SCALE


def _proj_prompt(x, mod, npre, w_row, w_t, earlier, *, batch, bm, blocks_per_batch, d_ssm, d_att):
    m, d = x.shape
    t_len = m // batch
    nbt = blocks_per_batch
    layer = earlier[0].shape[0] if earlier else 0
    row = lambda w: pl.BlockSpec((bm, w), lambda i: (i, 0))
    col = lambda r: pl.BlockSpec((r, bm), lambda i: (0, i))
    lcol = lambda n, r: pl.BlockSpec((n, None, r, bm), lambda i: (0, i // nbt, 0, i % nbt))
    bcol = lambda r: lcol(layer + 1, r)
    kern = functools.partial(_proj_prompt_kernel, d_ssm=d_ssm, d_att=d_att, layer=layer)
    return pl.pallas_call(
        kern,
        grid=(m // bm,),
        in_specs=[
            pl.BlockSpec((bm, d), lambda i: (i, 0)),
            _mod_spec(1, d, blocks_per_batch),
            _const_spec(npre.shape),
            _const_spec(w_row.shape),
            _const_spec(w_t.shape),
        ] + [lcol(layer, a.shape[2]) for a in earlier],
        out_specs=[
            row(d_ssm),
            pl.BlockSpec((N_HEADS, bm, HEAD_DIM), lambda i: (0, i, 0)),
            row(IDX_DIM),
            bcol(d_att), bcol(d_att), bcol(IDX_DIM),
            pl.BlockSpec((bm // QB, d_att, QB), lambda i: (i, 0, 0)),
            col(d_att), col(d_att), col(IDX_HEADS),
        ],
        out_shape=[
            jax.ShapeDtypeStruct((m, d_ssm), F32),
            jax.ShapeDtypeStruct((N_HEADS, m, HEAD_DIM), BF16),
            jax.ShapeDtypeStruct((m, IDX_DIM), BF16),
            jax.ShapeDtypeStruct((layer + 1, batch, d_att, t_len), F32),
            jax.ShapeDtypeStruct((layer + 1, batch, d_att, t_len), F32),
            jax.ShapeDtypeStruct((layer + 1, batch, IDX_DIM, t_len), F32),
            jax.ShapeDtypeStruct((m // QB, d_att, QB), BF16),
            jax.ShapeDtypeStruct((d_att, m), BF16),
            jax.ShapeDtypeStruct((d_att, m), BF16),
            jax.ShapeDtypeStruct((IDX_HEADS, m), F32),
        ],
        compiler_params=_cparams("parallel"),
        name="proj_prompt",
    )(x, mod, npre, w_row, w_t, *earlier)


def _proj_sample_kernel(x_ref, mod_ref, npre_ref, w_ref, o_ref):
    x = x_ref[...]
    sh, sc = mod_ref[3], mod_ref[4]
    h = (_rms(x, npre_ref[1:2, :]) * (1.0 + sc) + sh).astype(BF16)
    o_ref[...] = _dot(h, w_ref[...])


def _proj_sample(x, mod, npre, w):
    m, d = x.shape
    n = w.shape[1]
    return pl.pallas_call(
        _proj_sample_kernel,
        grid=(1,),
        in_specs=[
            pl.BlockSpec((m, d), lambda i: (0, 0)),
            _mod_spec(m, d, 1),
            _const_spec(npre.shape),
            _const_spec(w.shape),
        ],
        out_specs=pl.BlockSpec((m, n), lambda i: (0, 0)),
        out_shape=jax.ShapeDtypeStruct((m, n), F32),
        compiler_params=_cparams("arbitrary"),
        name="proj_sample",
    )(x, mod, npre, w)


def _ssm_prep_kernel(ldt_ref, ara_ref, aia_ref, arb_ref, aib_ref, btre_ref, btim_ref, cre_ref, cim_ref,
                     ctre_ref, ctim_ref,
                     kk_ref, wre_ref, wim_ref, pre_ref, pim_ref, al_ref, ab_ref, bbre_ref, bbim_ref):
    dt = jnp.exp(ldt_ref[...])

    def powers(ar, ai, k):
        mag = jnp.exp(dt * ar * k)
        ph = dt * ai * k
        return mag * jnp.cos(ph), mag * jnp.sin(ph)

    ar, ai = ara_ref[...], aia_ref[...]
    abr, abi = powers(ar, ai, 1.0)
    den = ar * ar + ai * ai
    z_re = ((abr - 1.0) * ar + abi * ai) / den
    z_im = (abi * ar - (abr - 1.0) * ai) / den
    b_re, b_im = btre_ref[...], btim_ref[...]
    bb_re = z_re * b_re - z_im * b_im
    bb_im = z_re * b_im + z_im * b_re
    bbre_ref[...] = bb_re
    bbim_ref[...] = bb_im
    ab_ref[0] = abr
    ab_ref[1] = abi
    lr, li = powers(ar, ai, float(CHUNK))
    al_ref[0] = lr
    al_ref[1] = li
    c_re, c_im = cre_ref[...], cim_ref[...]
    bnt = functools.partial(lax.dot_general, dimension_numbers=(((2,), (2,)), ((0,), (0,))),
                            precision=lax.Precision.HIGHEST, preferred_element_type=F32)
    for k in range(CHUNK):
        pr, pi = powers(ar, ai, float(CHUNK - 1 - k))
        wre_ref[k] = pr * bb_re - pi * bb_im
        wim_ref[k] = pr * bb_im + pi * bb_re
        pr, pi = powers(ar, ai, float(k))
        kk_ref[k] = bnt(bb_re, c_re * pr - c_im * pi) - bnt(bb_im, c_re * pi + c_im * pr)

    ar, ai = arb_ref[...], aib_ref[...]
    c_re, c_im = ctre_ref[...], ctim_ref[...]
    for t in range(CHUNK):
        pr, pi = powers(ar, ai, float(t + 1))
        pre_ref[t] = c_re * pr - c_im * pi
        pim_ref[t] = -(c_re * pi + c_im * pr)


def _ssm_prep(log_dt, a_re, a_im, b_re, b_im, c_re, c_im):
    g, n = a_re.shape
    j = SSM_GROUP
    shp = lambda *s: jax.ShapeDtypeStruct(s, F32)
    swap = lambda a: a.transpose(0, 2, 1)
    return pl.pallas_call(
        _ssm_prep_kernel,
        out_shape=[shp(CHUNK, g, j, j), shp(CHUNK, g, j, n), shp(CHUNK, g, j, n),
                   shp(CHUNK, g, n, j), shp(CHUNK, g, n, j), shp(2, g, 1, n), shp(2, g, 1, n),
                   shp(g, j, n), shp(g, j, n)],
        compiler_params=pltpu.CompilerParams(vmem_limit_bytes=VMEM_LIMIT_BYTES),
        name="ssm_prep",
    )(log_dt.reshape(g, 1, 1), a_re.reshape(g, 1, n), a_im.reshape(g, 1, n),
      a_re.reshape(g, n, 1), a_im.reshape(g, n, 1), swap(b_re), swap(b_im), c_re, c_im, swap(c_re), swap(c_im))


def _block_diag(x, groups):
    rows, c = x.shape[-2:]
    keep = (np.arange(rows)[:, None] // (rows // groups)) == (np.arange(groups * c)[None, :] // c)
    return jnp.where(keep, jnp.tile(x, (1,) * (x.ndim - 1) + (groups,)), 0.0)


def _ssm_assemble_kernel(kk_ref, wre_ref, wim_ref, pre_ref, pim_ref, m_ref, w_ref, p_ref):
    j, n = SSM_GROUP, SSM_N

    def block_diag(x, r, c):
        rows, cols = OCTET * r, OCTET * c
        rep = ((lax.broadcasted_iota(I32, (c, cols), 1) & (c - 1)) == lax.broadcasted_iota(I32, (c, cols), 0))
        keep = ((lax.broadcasted_iota(I32, (rows, cols), 0) >> int(math.log2(r)))
                == (lax.broadcasted_iota(I32, (rows, cols), 1) >> int(math.log2(c))))
        return jnp.where(keep, _dot(x.astype(BF16), rep.astype(BF16)), 0.0).astype(BF16)

    m_ref[...] = jnp.zeros(m_ref.shape, BF16)
    blk = lambda i, w: slice(i * w, (i + 1) * w)
    for k in range(CHUNK):
        tile = block_diag(kk_ref[k].reshape(OCTET * j, j), j, j)
        for tau in range(CHUNK - k):
            m_ref[blk(tau, LANES), blk(tau + k, LANES)] = tile
        for part, (w_src, p_src) in enumerate(((wre_ref, pre_ref), (wim_ref, pim_ref))):
            w_ref[blk(k, LANES), blk(part, OCTET * n)] = block_diag(w_src[k].reshape(OCTET * j, n), j, n)
            p_ref[blk(part, OCTET * n), blk(k, LANES)] = block_diag(p_src[k].reshape(OCTET * n, j), n, j)


def _ssm_assemble(kk, w_re, w_im, p_re, p_im):
    chunk, g, j, n = w_re.shape
    no, kw, sw = g // OCTET, chunk * LANES, 2 * OCTET * n
    grp = lambda a, b: pl.BlockSpec((chunk, OCTET, a, b), lambda o: (0, o, 0, 0))
    out = lambda a, b: pl.BlockSpec((None, a, b), lambda o: (o, 0, 0))
    return pl.pallas_call(
        _ssm_assemble_kernel,
        grid=(no,),
        in_specs=[grp(j, j), grp(j, n), grp(j, n), grp(n, j), grp(n, j)],
        out_specs=[out(kw, kw), out(kw, sw), out(sw, kw)],
        out_shape=[jax.ShapeDtypeStruct((no, kw, kw), BF16), jax.ShapeDtypeStruct((no, kw, sw), BF16),
                   jax.ShapeDtypeStruct((no, sw, kw), BF16)],
        compiler_params=_cparams("parallel"),
        name="ssm_assemble",
    )(kk, w_re, w_im, p_re, p_im)


def _ssm_operators(prep, c_re, c_im):
    kk, w_re, w_im, p_re, p_im, al, ab, bb_re, bb_im = prep
    g, j, n = bb_re.shape
    no = g // OCTET
    m_op, w_op, p_op = _ssm_assemble(kk, w_re, w_im, p_re, p_im)
    a_chunk = al.reshape(2, no, OCTET * n).transpose(1, 0, 2)
    b_step = _block_diag(jnp.stack([bb_re, bb_im], 0).reshape(2, g * j, n), g)
    b_step = b_step.transpose(1, 0, 2).reshape(g * j, 2 * g * n)
    c_t = jnp.stack([c_re, -c_im], 0).transpose(0, 1, 3, 2).reshape(2, g * n, j)
    c_step = _block_diag(c_t, g).reshape(2 * g * n, g * j).astype(BF16)
    return dict(m=m_op, w=w_op, p=p_op, a_chunk=a_chunk, b_step=b_step, c_step=c_step, abar=ab.reshape(2, g * n))


def _ssm_prompt_kernel(u_ref, m_ref, w_ref, p_ref, al_ref, d_ref, y_ref, sfin_ref,
                       uo_ref, v_ref, sc_ref, *, r):
    half = sc_ref.shape[1] // 2
    for tau in range(CHUNK):
        uo_ref[:, tau * LANES:(tau + 1) * LANES] = u_ref[pl.ds(tau, r, stride=CHUNK), :].astype(BF16)
    uo = uo_ref[...]
    v_ref[...] = _dot(uo, w_ref[...])
    a_r, a_i = al_ref[0:1, :], al_ref[1:2, :]

    def step(c, carry):
        s_r, s_i = carry
        sc_ref[pl.ds(c, 1), 0:half] = s_r
        sc_ref[pl.ds(c, 1), half:2 * half] = s_i
        v = v_ref[pl.ds(c, 1), :]
        return (a_r * s_r - a_i * s_i + v[:, 0:half], a_r * s_i + a_i * s_r + v[:, half:2 * half])

    zero = jnp.zeros((1, half), F32)
    s_r, s_i = lax.fori_loop(0, r, step, (zero, zero))
    sfin_ref[0:1, :] = s_r
    sfin_ref[1:2, :] = s_i
    y = _dot(uo, m_ref[...]) + _dot(sc_ref[...].astype(BF16), p_ref[...])
    d = d_ref[...]
    for t in range(CHUNK):
        rows = pl.ds(t, r, stride=CHUNK)
        y_ref[rows, :] = y[:, t * LANES:(t + 1) * LANES] + d * u_ref[rows, :]


def _ssm_prompt(u, ops, d_skip, *, batch, t_len):
    m, d_ssm = u.shape
    no = d_ssm // LANES
    r = t_len // CHUNK
    kw = CHUNK * LANES
    sw = ops["w"].shape[2]
    op_spec = lambda a, b: pl.BlockSpec((None, a, b), lambda o, bb: (o, 0, 0))
    y, sfin = pl.pallas_call(
        functools.partial(_ssm_prompt_kernel, r=r),
        grid=(no, batch),
        in_specs=[
            pl.BlockSpec((t_len, LANES), lambda o, bb: (bb, o)),
            op_spec(kw, kw), op_spec(kw, sw), op_spec(sw, kw),
            pl.BlockSpec((None, 2, sw // 2), lambda o, bb: (o, 0, 0)),
            pl.BlockSpec((1, LANES), lambda o, bb: (0, o)),
        ],
        out_specs=[
            pl.BlockSpec((t_len, LANES), lambda o, bb: (bb, o)),
            pl.BlockSpec((None, None, 2, sw // 2), lambda o, bb: (bb, o, 0, 0)),
        ],
        out_shape=[jax.ShapeDtypeStruct((m, d_ssm), F32),
                   jax.ShapeDtypeStruct((batch, no, 2, sw // 2), F32)],
        scratch_shapes=[pltpu.VMEM((r, kw), BF16), pltpu.VMEM((r, sw), F32), pltpu.VMEM((r, sw), F32)],
        compiler_params=_cparams("parallel", "parallel"),
        name="ssm_prompt",
    )(u, ops["m"], ops["w"], ops["p"], ops["a_chunk"], d_skip.reshape(1, d_ssm))
    s = sfin.reshape(batch, no, 2, OCTET, SSM_N).transpose(2, 0, 1, 3, 4).reshape(2, batch, no * OCTET, SSM_N)
    return y, s[0], s[1]


def _ssm_sample_kernel(u_ref, h_ref, ab_ref, bstep_ref, cstep_ref, d_ref, y_ref, s_ref):
    u = u_ref[...]
    half = h_ref.shape[2]
    bu = jnp.dot(u, bstep_ref[...], precision=lax.Precision.HIGHEST, preferred_element_type=F32)
    a_r, a_i = ab_ref[0:1, :], ab_ref[1:2, :]
    h_r, h_i = h_ref[0], h_ref[1]
    s_r = a_r * h_r - a_i * h_i + bu[:, 0:half]
    s_i = a_r * h_i + a_i * h_r + bu[:, half:2 * half]
    s_ref[0] = s_r
    s_ref[1] = s_i
    s = jnp.concatenate([s_r, s_i], axis=1).astype(BF16)
    y_ref[...] = _dot(s, cstep_ref[...]) + d_ref[...] * u


def _ssm_sample(u, h_re, h_im, ops, c_step, d_skip):
    bsz, d_ssm = u.shape
    gn = h_re.shape[1] * h_re.shape[2]
    h = jnp.stack([h_re.reshape(bsz, gn), h_im.reshape(bsz, gn)], 0)
    y, s = pl.pallas_call(
        _ssm_sample_kernel,
        out_shape=[jax.ShapeDtypeStruct((bsz, d_ssm), F32), jax.ShapeDtypeStruct((2, bsz, gn), F32)],
        compiler_params=pltpu.CompilerParams(vmem_limit_bytes=VMEM_LIMIT_BYTES),
        name="ssm_sample",
    )(u, h, ops["abar"], ops["b_step"], c_step, d_skip.reshape(1, d_ssm))
    return y, s[0].reshape(h_re.shape), s[1].reshape(h_im.shape)


def _f2key(x):
    b = lax.bitcast_convert_type(x, I32)
    return b ^ ((b >> 31) & 0x7FFFFFFF)


def _key2f(k):
    return lax.bitcast_convert_type(k ^ ((k >> 31) & 0x7FFFFFFF), F32)


def _search_init(mn, mx, n_valid, topk):
    z = jnp.zeros_like(n_valid)
    half = topk + 0.5
    return (_f2key(mn), _f2key(mx) + 1, n_valid.astype(F32) - half, jnp.full(mn.shape, half, F32),
            jnp.full(mn.shape, NEG_BIG, F32), jnp.where(n_valid > topk, 0, 1).astype(I32), z, z)


def _search_probe(state, it, topk):
    lo, hi, f_lo, f_hi = state[:4]
    lo_f, hi_f = _key2f(lo), _key2f(hi)
    frac = jnp.where(it % 4 == 3, 0.5, f_lo / (f_lo + f_hi))
    cand = _f2key(lo_f + (hi_f - lo_f) * frac)
    mid_k = (lo >> 1) + (hi >> 1) + (lo & hi & 1)
    inside = (cand > lo) & (cand < hi) & (it < FLOAT_MID_ITERS)
    probe = jnp.where(inside, cand, mid_k)
    zero_k = jnp.where(it == 0, 0, MIN_NORMAL_KEY)
    return jnp.where((it < 2) & (zero_k > lo) & (zero_k < hi), zero_k, probe)


def _search_update(state, mid, cnt, topk):
    lo, hi, f_lo, f_hi, thr, done, tie, last = state
    hit = cnt == topk
    up, dn = cnt > topk, cnt < topk
    lo_n, hi_n = jnp.where(up, mid, lo), jnp.where(dn, mid, hi)
    adj = ((hi_n == lo_n + 1) | ((lo_n == 0) & (hi_n == MIN_NORMAL_KEY))) & jnp.logical_not(hit)
    fin = hit | adj
    thr_n = jnp.where(hit, _key2f(mid), _key2f(lo_n))
    act = done == 0
    lo = jnp.where(act, lo_n, lo)
    hi = jnp.where(act, hi_n, hi)
    miss = cnt.astype(F32) - (topk + 0.5)
    f_lo = jnp.where(act, jnp.where(up, miss, jnp.where(dn & (last == -1), 0.5 * f_lo, f_lo)), f_lo)
    f_hi = jnp.where(act, jnp.where(dn, -miss, jnp.where(up & (last == 1), 0.5 * f_hi, f_hi)), f_hi)
    last = jnp.where(act, jnp.where(up, 1, jnp.where(dn, -1, last)), last)
    thr = jnp.where(act & fin, thr_n, thr)
    tie = jnp.where(act & adj, 1, tie)
    done = jnp.where(act & fin, 1, done)
    return lo, hi, f_lo, f_hi, thr, done, tie, last


def _search(count_ge, init, topk):
    def pending(state, it):
        return jnp.logical_and(jnp.min(state[5].astype(F32)) == 0.0, it < 96)

    def probe(u, carry):
        it, state = carry
        mid = _search_probe(state, it, topk)
        return it + 1, _search_update(state, mid, count_ge(_key2f(mid)), topk)

    def body(carry):
        it, state = lax.fori_loop(0, SEARCH_PROBES_PER_TEST, probe, carry[1:])
        return pending(state, it), it, state

    _, _, state = lax.while_loop(lambda c: c[0], body, (pending(init, 0), jnp.int32(0), init))
    return state[4], state[6]


def _attn_prompt_kernel(qt_ref, qit_ref, wt_ref, kh_ref, kidx_ref, v3_ref, o_ref, sc_ref, lg_ref, acc_ref, *, topk):
    i = pl.program_id(1)
    nk = i + 1
    qpos = i * QB + lax.broadcasted_iota(I32, (QB, QB), 1)
    krow = lax.broadcasted_iota(I32, (QB, QB), 0)
    fold = lambda x: x.reshape(QB // SUBLANES, SUBLANES, QB)

    tile_rows = lambda j: pl.ds(pl.multiple_of(j * QB, QB), QB)
    rep = lambda x: jnp.broadcast_to(x, (SUBLANES, QB))

    def score_tile(j, carry):
        mn, mx = carry
        kx = kidx_ref[tile_rows(j), :]
        acc = jnp.zeros((QB, QB), F32)
        for h in range(IDX_HEADS):
            s = _dot(kx, qit_ref[h * IDX_DIM:(h + 1) * IDX_DIM, :])
            acc = acc + jnp.maximum(s, 0.0) * wt_ref[h:h + 1, :]
        valid = (j * QB + krow) <= qpos
        sc = jnp.where(valid, acc, -jnp.inf)
        sc_ref[j] = sc
        mx = jnp.maximum(mx, jnp.max(fold(sc), axis=0))
        mn = jnp.minimum(mn, jnp.min(fold(jnp.where(valid, acc, jnp.inf)), axis=0))
        return mn, mx

    mn, mx = lax.fori_loop(0, nk, score_tile,
                           (jnp.full((SUBLANES, QB), jnp.inf, F32), jnp.full((SUBLANES, QB), -jnp.inf, F32)))
    mn = rep(jnp.min(mn, axis=0, keepdims=True))
    mx = rep(jnp.max(mx, axis=0, keepdims=True))

    def count_ge(t):
        t1 = t[0:1, :]

        def body(j, c):
            return c + jnp.sum(fold((sc_ref[j] >= t1).astype(I32)), axis=0)

        c = lax.fori_loop(0, nk, body, jnp.zeros((SUBLANES, QB), I32))
        return rep(jnp.sum(c, axis=0, keepdims=True))

    n_valid = qpos[0:SUBLANES, :] + 1
    thr, tie = _search(count_ge, _search_init(mn, mx, n_valid, topk), topk)

    @pl.when(jnp.max(tie) > 0)
    def _():
        tri = (lax.broadcasted_iota(I32, (QB, QB), 0) >= lax.broadcasted_iota(I32, (QB, QB), 1)).astype(BF16)
        thr1, tie1 = thr[0:1, :], tie[0:1, :] > 0

        def gt_tile(j, c):
            return c + jnp.sum((sc_ref[j] > thr1).astype(I32), axis=0, keepdims=True)

        quota = topk - lax.fori_loop(0, nk, gt_tile, jnp.zeros((1, QB), I32))

        def tie_tile(j, before):
            x = sc_ref[j]
            t = (x == thr1) & tie1
            rank = before + _dot(tri, t.astype(BF16)).astype(I32)
            sc_ref[j] = jnp.where(t & (rank > quota), -jnp.inf, x)
            return before + jnp.sum(t.astype(I32), axis=0, keepdims=True)

        lax.fori_loop(0, nk, tie_tile, jnp.zeros((1, QB), I32))

    thr1 = thr[0:1, :]
    heads = range(N_HEADS)
    hslice = lambda h: slice(h * HEAD_DIM, (h + 1) * HEAD_DIM)

    colmax = lambda x: jnp.max(jnp.max(fold(x), axis=0), axis=0, keepdims=True)
    colsum = lambda x: jnp.sum(jnp.sum(fold(x), axis=0), axis=0, keepdims=True)
    rows_of = lambda rows: jnp.concatenate(rows, axis=0)

    def logits(j, slot, m):
        bias = jnp.where(sc_ref[j] >= thr1, 0.0, -jnp.inf)
        rows = tile_rows(j)
        out = []
        for h in heads:
            lg = _dot(kh_ref[h, rows, :], qt_ref[hslice(h), :]) + bias
            lg_ref[h, slot] = lg
            out.append(jnp.maximum(m[h:h + 1, :], colmax(lg)))
        return rows_of(out)

    def values(j, slot, m_old, m_new, l):
        m_safe = jnp.where(m_new == -jnp.inf, 0.0, m_new)
        alpha = jnp.exp(m_old - m_safe)
        out = []
        for h in heads:
            p = jnp.exp(lg_ref[h, slot] - m_safe[h:h + 1, :])
            acc_ref[h] = acc_ref[h] * alpha[h:h + 1, :] + _dot(v3_ref[j, hslice(h), :], p.astype(BF16))
            out.append(colsum(p))
        return l * alpha + rows_of(out)

    acc_ref[...] = jnp.zeros(acc_ref.shape, F32)
    neg = jnp.full((N_HEADS, QB), -jnp.inf, F32)

    def step(j, slot, carry):
        m_before, m_upto, l = carry
        m_next = logits(j + 1, 1 - slot, m_upto)
        return m_upto, m_next, values(j, slot, m_before, m_upto, l)

    def two_steps(jj, carry):
        return step(2 * jj + 1, 1, step(2 * jj, 0, carry))

    carry = (neg, logits(0, 0, neg), jnp.zeros((N_HEADS, QB), F32))
    carry = lax.fori_loop(0, (nk - 1) // 2, two_steps, carry)
    odd = (nk - 1) % 2 == 1
    carry = lax.cond(odd, lambda c: step(nk - 2, 0, c), lambda c: c, carry)
    l = lax.cond(odd, lambda c: values(nk - 1, 1, *c), lambda c: values(nk - 1, 0, *c), carry)
    outs = [acc_ref[h] / l[h:h + 1, :] for h in heads]
    o_ref[...] = jnp.transpose(jnp.concatenate(outs, axis=0)).astype(BF16)


def _attn_prompt(qt, qit, wt, kh, kidxb, v3, *, batch, t_len, topk):
    d_att, m = qt.shape
    nq = t_len // QB
    return pl.pallas_call(
        functools.partial(_attn_prompt_kernel, topk=topk),
        grid=(batch, nq),
        in_specs=[
            pl.BlockSpec((d_att, QB), lambda b, i: (0, b * nq + i)),
            pl.BlockSpec((d_att, QB), lambda b, i: (0, b * nq + i)),
            pl.BlockSpec((IDX_HEADS, QB), lambda b, i: (0, b * nq + i)),
            pl.BlockSpec((N_HEADS, t_len, HEAD_DIM), lambda b, i: (0, b, 0)),
            pl.BlockSpec((t_len, IDX_DIM), lambda b, i: (b, 0)),
            pl.BlockSpec((nq, d_att, QB), lambda b, i: (b, 0, 0)),
        ],
        out_specs=pl.BlockSpec((QB, d_att), lambda b, i: (b * nq + i, 0)),
        out_shape=jax.ShapeDtypeStruct((m, d_att), BF16),
        scratch_shapes=[pltpu.VMEM((nq, QB, QB), F32), pltpu.VMEM((N_HEADS, 2, QB, QB), F32),
                        pltpu.VMEM((N_HEADS, HEAD_DIM, QB), F32)],
        compiler_params=_cparams("parallel", "arbitrary"),
        name="attn_prompt",
    )(qt, qit, wt, kh, kidxb, v3)


def _idx_sample_kernel(pt_ref, qi_ref, w_ref, kin_ref, ci_ref, o_ref, ibuf, sem, *, layer, n_pages):
    b = pl.program_id(0)
    slot = b % 2

    def page_copy(buf_slot, p, page):
        lanes = pl.ds(pl.multiple_of(p * PAGE_SIZE, PAGE_SIZE), PAGE_SIZE)
        return pltpu.make_async_copy(ci_ref.at[layer, page], ibuf.at[buf_slot, :, lanes], sem.at[buf_slot])

    def start_all(sample, buf_slot):
        def start(p, _):
            page_copy(buf_slot, p, pt_ref[sample, p]).start()
            return 0
        lax.fori_loop(0, n_pages, start, 0)

    @pl.when(b == 0)
    def _():
        start_all(b, slot)

    @pl.when(b + 1 < pl.num_programs(0))
    def _():
        start_all(b + 1, 1 - slot)

    def wait(p, _):
        page_copy(slot, p, 0).wait()
        return 0

    lax.fori_loop(0, n_pages, wait, 0)

    qi = qi_ref[...].astype(BF16)
    w = w_ref[...] * IDX_SCALE
    n_past = n_pages * PAGE_SIZE
    step = SUBLANES * PAGE_SIZE
    for c in range(n_past // step):
        cols = slice(c * step, (c + 1) * step)
        s = _dot(qi, ibuf[slot, :, cols].astype(BF16))
        o_ref[:, cols] = jnp.sum(jnp.maximum(s, 0.0) * w, axis=0, keepdims=True)
    s_new = jnp.sum(qi.astype(F32) * kin_ref[...].astype(BF16).astype(F32), axis=1, keepdims=True)
    s_new = jnp.sum(jnp.maximum(s_new, 0.0) * w, axis=0, keepdims=True)
    lane = lax.broadcasted_iota(I32, (1, PAGE_SIZE), 1)
    o_ref[:, n_past:n_past + PAGE_SIZE] = jnp.where(lane == 0, s_new, -jnp.inf)


def _idx_sample(page_table, q_idx, w_idx, k_idx_new, cache_idx_t, *, layer):
    bsz, n_pages = page_table.shape
    width = (n_pages + 1) * PAGE_SIZE
    blk = lambda r, c: pl.BlockSpec((None, r, c), lambda b, pt: (b, 0, 0))
    return pl.pallas_call(
        functools.partial(_idx_sample_kernel, layer=layer, n_pages=n_pages),
        grid_spec=pltpu.PrefetchScalarGridSpec(
            num_scalar_prefetch=1,
            grid=(bsz,),
            in_specs=[blk(IDX_HEADS, IDX_DIM), blk(IDX_HEADS, 1), blk(1, IDX_DIM), pl.BlockSpec(memory_space=pl.ANY)],
            out_specs=blk(1, width),
            scratch_shapes=[pltpu.VMEM((2, IDX_DIM, n_pages * PAGE_SIZE), F32), pltpu.SemaphoreType.DMA((2,))],
        ),
        out_shape=jax.ShapeDtypeStruct((bsz, 1, width), F32),
        compiler_params=_cparams("arbitrary"),
        name="idx_sample",
    )(page_table, q_idx, w_idx, k_idx_new, cache_idx_t)


def _threshold_sample_kernel(sc_ref, thr_ref, tie_ref, *, n_keys, topk):
    sc = sc_ref[...]
    mx = jnp.max(sc, axis=1, keepdims=True)
    mn = jnp.min(jnp.where(sc == -jnp.inf, jnp.inf, sc), axis=1, keepdims=True)

    def count_ge(t):
        return jnp.sum((sc_ref[...] >= t).astype(I32), axis=1, keepdims=True)

    n = jnp.full(mn.shape, n_keys, I32)
    thr, tie = _search(count_ge, _search_init(mn, mx, n, topk), topk)
    thr_ref[...] = jnp.broadcast_to(thr, thr_ref.shape)
    tie_ref[...] = jnp.broadcast_to(tie, tie_ref.shape)


def _threshold_sample(scores, *, n_keys, topk):
    bsz = scores.shape[0]
    return pl.pallas_call(
        functools.partial(_threshold_sample_kernel, n_keys=n_keys, topk=topk),
        out_shape=[jax.ShapeDtypeStruct((bsz, LANES), F32), jax.ShapeDtypeStruct((bsz, LANES), I32)],
        compiler_params=pltpu.CompilerParams(vmem_limit_bytes=VMEM_LIMIT_BYTES),
        name="threshold_sample",
    )(scores)


def _attend_sample_kernel(pt_ref, sc_ref, snew_ref, thr_ref, tie_ref, q_ref, kn_ref, vn_ref, ck_ref, cv_ref, o_ref,
                          kbuf, vbuf, qb, lg_ref, sem_k, sem_v, *, layer, n_pages, topk):
    b = pl.program_id(0)

    def copies(src_ref, buf, sem, p, page):
        return pltpu.make_async_copy(src_ref.at[layer, page], buf.at[p], sem)

    streams = ((ck_ref, kbuf, sem_k), (cv_ref, vbuf, sem_v))

    def start_all(src_ref, buf, sem, sample):
        def start(p, _):
            copies(src_ref, buf, sem, p, pt_ref[sample, p]).start()
            return 0
        lax.fori_loop(0, n_pages, start, 0)

    @pl.when(b == 0)
    def _():
        start_all(*streams[0], b)

    start_all(*streams[1], b)

    def wait_all(src_ref, buf, sem):
        def wait(p, _):
            copies(src_ref, buf, sem, p, 0).wait()
            return 0
        lax.fori_loop(0, n_pages, wait, 0)

    for h in range(N_HEADS):
        qb[h] = jnp.broadcast_to(q_ref[h] * ATT_SCALE, (HEAD_DIM, PAGE_SIZE))
    dsum = lambda x: jnp.sum(x, axis=0, keepdims=True)
    sc = sc_ref[...]
    s_new, thr, tie = snew_ref[...], thr_ref[...], tie_ref[...]

    def total(x, op=jnp.sum):
        return op(op(x, axis=1, keepdims=True), axis=0, keepdims=True)

    upper = (lax.broadcasted_iota(I32, (PAGE_SIZE, PAGE_SIZE), 0)
             <= lax.broadcasted_iota(I32, (PAGE_SIZE, PAGE_SIZE), 1)).astype(BF16)
    lower = (lax.broadcasted_iota(I32, (n_pages, n_pages), 1)
             < lax.broadcasted_iota(I32, (n_pages, n_pages), 0)).astype(BF16)

    def flat_rank(mask):
        mb = mask.astype(BF16)
        incl = _dot(mb, upper)
        before = jnp.sum(_dot(lower, mb), axis=1, keepdims=True)
        return (incl + before).astype(I32)

    gt = sc > thr
    tied = tie > 0
    is_tie = (sc == thr) & tied
    quota = jnp.where(tied, topk - total(gt.astype(I32)) - (s_new > thr).astype(I32), topk)
    sel = gt | ((sc == thr) & (flat_rank(is_tie) <= quota))
    n_tie_past = total(is_tie.astype(I32))
    new_sel = (s_new > thr) | ((s_new == thr) & (n_tie_past < quota))

    wait_all(*streams[0])
    for h in range(N_HEADS):
        q_h = qb[h]

        def k_page(p, _, h=h, q_h=q_h):
            lg_ref[h, pl.ds(p, 1), :] = dsum(kbuf[p, h] * q_h)
            return 0

        lax.fori_loop(0, n_pages, k_page, 0, unroll=4)

    @pl.when(b + 1 < pl.num_programs(0))
    def _():
        start_all(*streams[0], b + 1)

    red = lambda x, op: op(x, axis=(1, 2), keepdims=True)
    lg = jnp.where(sel[None], lg_ref[...], -jnp.inf)
    lg_new = jnp.sum(q_ref[...] * ATT_SCALE * kn_ref[...], axis=1, keepdims=True)
    lg_new = jnp.where(new_sel[None], lg_new, -jnp.inf)
    m = jnp.maximum(red(lg, jnp.max), lg_new)
    p = jnp.exp(lg - m)
    e_new = jnp.exp(lg_new - m)
    denom = red(p, jnp.sum) + e_new
    lg_ref[...] = p / denom
    p_new = e_new / denom

    wait_all(*streams[1])
    for h in range(N_HEADS):
        def v_page(p, acc, h=h):
            return acc + vbuf[p, h] * lg_ref[h, pl.ds(p, 1), :]

        acc = lax.fori_loop(0, n_pages, v_page, jnp.zeros((HEAD_DIM, PAGE_SIZE), F32), unroll=4)
        o_ref[h] = jnp.sum(acc, axis=1, keepdims=True) + p_new[h] * vn_ref[h]


def _attend_sample(page_table, scores, s_new, thr, tie, q, k_new, v_new, cache_k_t, cache_v_t, *, layer, topk):
    bsz, n_pages = page_table.shape
    one = pl.BlockSpec((None, 1, 1), lambda b, pt: (b, 0, 0))
    hcol = pl.BlockSpec((None, N_HEADS, HEAD_DIM, 1), lambda b, pt: (b, 0, 0, 0))
    any_spec = pl.BlockSpec(memory_space=pl.ANY)
    return pl.pallas_call(
        functools.partial(_attend_sample_kernel, layer=layer, n_pages=n_pages, topk=topk),
        grid_spec=pltpu.PrefetchScalarGridSpec(
            num_scalar_prefetch=1,
            grid=(bsz,),
            in_specs=[pl.BlockSpec((None, n_pages, PAGE_SIZE), lambda b, pt: (b, 0, 0)), one, one, one,
                      hcol, hcol, hcol, any_spec, any_spec],
            out_specs=hcol,
            scratch_shapes=[pltpu.VMEM((n_pages, N_HEADS, HEAD_DIM, PAGE_SIZE), F32),
                            pltpu.VMEM((n_pages, N_HEADS, HEAD_DIM, PAGE_SIZE), F32),
                            pltpu.VMEM((N_HEADS, HEAD_DIM, PAGE_SIZE), F32),
                            pltpu.VMEM((N_HEADS, n_pages, PAGE_SIZE), F32),
                            pltpu.SemaphoreType.DMA(()), pltpu.SemaphoreType.DMA(())],
        ),
        out_shape=jax.ShapeDtypeStruct((bsz, N_HEADS, HEAD_DIM, 1), F32),
        compiler_params=_cparams("arbitrary"),
        name="attend_sample",
    )(page_table, scores, s_new, thr, tie, q, k_new, v_new, cache_k_t, cache_v_t)


def _mix_kernel(x_ref, mod_ref, npre_ref, npost_ref, ys_ref, at_ref, gw_ref, gv_ref, wba_ref, wg_ref, wo_ref,
                o_ref):
    x = x_ref[...]
    d = x.shape[1]
    sh, sc, gt = mod_ref[3], mod_ref[4], mod_ref[5]
    h = (_rms(x, npre_ref[1:2, :]) * (1.0 + sc) + sh).astype(BF16)
    gates = _dot(h, wg_ref[...])
    ys = ys_ref[...].astype(BF16)
    y_a = _dot(ys, gw_ref[...]) * jax.nn.sigmoid(_dot(ys, gv_ref[...]))
    y_b = _dot(at_ref[...].astype(BF16), wba_ref[...])
    mixed = jax.nn.sigmoid(gates[:, 0:d]) * y_a + jax.nn.sigmoid(gates[:, d:2 * d]) * y_b
    y = _dot(mixed.astype(BF16), wo_ref[...])
    o_ref[...] = x + gt * _rms(y, npost_ref[1:2, :])


def _mix(x, mod, npre, npost, y_ssm, attn, glu_w, glu_v, wba, w_gates, w_out, *, bm, blocks_per_batch):
    m, d = x.shape
    nb = mod.shape[2]
    return pl.pallas_call(
        _mix_kernel,
        grid=(m // bm,),
        in_specs=[
            pl.BlockSpec((bm, d), lambda i: (i, 0)),
            _mod_spec(nb, d, blocks_per_batch),
            _const_spec(npre.shape), _const_spec(npost.shape),
            pl.BlockSpec((bm, y_ssm.shape[1]), lambda i: (i, 0)),
            pl.BlockSpec((bm, attn.shape[1]), lambda i: (i, 0)),
            _const_spec(glu_w.shape), _const_spec(glu_v.shape), _const_spec(wba.shape),
            _const_spec(w_gates.shape), _const_spec(w_out.shape),
        ],
        out_specs=pl.BlockSpec((bm, d), lambda i: (i, 0)),
        out_shape=jax.ShapeDtypeStruct((m, d), F32),
        compiler_params=_cparams("parallel"),
        name="mix",
    )(x, mod, npre, npost, y_ssm, attn, glu_w, glu_v, wba, w_gates, w_out)


def _pad_cols(w, n):
    return jnp.pad(w, ((0, 0), (0, n - w.shape[1])))


def kernel(x_prompt, x_sample, cache_k, cache_v, cache_idx_k, state_ssm_re, state_ssm_im, page_table,
           c_prompt, c_sample, mod_w, mod_b, norm_pre, norm_post, ffn1_in, ffn1_out, w_in,
           ssm_log_dt, ssm_a_re, ssm_a_im, ssm_b_re, ssm_b_im, ssm_c_re, ssm_c_im, ssm_d,
           glu_w, glu_v, w_branch_attn, w_out, ffn2_in, ffn2_out):
    batch, t_len, d = x_prompt.shape
    dec_batch, dec_seq, _ = x_sample.shape
    depth = mod_w.shape[0]
    d_ssm = ssm_d.shape[1]
    d_att = N_HEADS * HEAD_DIM
    n_groups = d_ssm // SSM_GROUP
    assert dec_seq == 1 and t_len % QB == 0 and d_ssm % LANES == 0
    m = batch * t_len
    bm = 512 if m % 512 == 0 else QB
    topk_p = min(INDEX_TOPK, t_len // 4)
    n_pages = page_table.shape[1]
    n_past = n_pages * PAGE_SIZE
    assert n_pages % SUBLANES == 0
    topk_s = min(INDEX_TOPK, (n_past + dec_seq) // 4)

    mod = _modulation(jnp.concatenate([c_prompt, c_sample], 0), mod_w, mod_b).reshape(depth, -1, 9, d)

    widths = (d_ssm, d_att, d_att, d_att, IDX_HEADS * IDX_DIM, IDX_DIM, IDX_HEADS, d, d)
    off = np.concatenate([[0], np.cumsum(widths)])
    o_u, o_q, o_k, o_v, o_qi, o_ki, o_wi, o_ga, o_gb, o_end = (int(v) for v in off)

    cache_k_t = cache_k.transpose(0, 1, 3, 4, 2)
    cache_v_t = cache_v.transpose(0, 1, 3, 4, 2)
    cache_idx_t = cache_idx_k.transpose(0, 1, 3, 2)

    xp = x_prompt.reshape(m, d)
    xs = x_sample.reshape(dec_batch, d)
    new_p, new_s = [], []
    kv_t = ()
    for l in range(depth):
        bf = lambda w: w[l].astype(BF16)
        modp = mod[l, :batch].transpose(1, 0, 2).reshape(9, batch, 1, d)
        mods = mod[l, batch:].transpose(1, 0, 2).reshape(9, 1, dec_batch, d)
        npre, npost = norm_pre[l], norm_post[l]
        f1_in, f1_out, f2_in, f2_out = bf(ffn1_in), bf(ffn1_out), bf(ffn2_in), bf(ffn2_out)
        wl = w_in[l]
        w_t = wl[:, o_q:o_ga].T.astype(BF16)
        w_gates = wl[:, o_ga:o_end].astype(BF16)
        w_all = _pad_cols(wl, 37 * LANES).astype(BF16)
        g_w, g_v, wba, wo = bf(glu_w), bf(glu_v), bf(w_branch_attn), bf(w_out)
        prep = _ssm_prep(ssm_log_dt[l], ssm_a_re[l], ssm_a_im[l], ssm_b_re[l], ssm_b_im[l],
                         ssm_c_re[l], ssm_c_im[l])
        ops = _ssm_operators(prep, ssm_c_re[l], ssm_c_im[l])

        xp = _ffn(xp, modp, npre, npost, f1_in, f1_out, k=0, bm=bm, blocks_per_batch=t_len // bm)
        u, kh, kidxb, *kv_t, v3, qt, qit, wt = _proj_prompt(
            xp, modp, npre, w_all, w_t, tuple(kv_t), batch=batch, bm=bm, blocks_per_batch=t_len // bm,
            d_ssm=d_ssm, d_att=d_att)
        y_ssm, sp_re, sp_im = _ssm_prompt(u, ops, ssm_d[l], batch=batch, t_len=t_len)
        attn = _attn_prompt(qt, qit, wt, kh, kidxb, v3, batch=batch, t_len=t_len, topk=topk_p)
        xp = _mix(xp, modp, npre, npost, y_ssm, attn, g_w, g_v, wba, w_gates, wo,
                  bm=bm, blocks_per_batch=t_len // bm)
        xp = _ffn(xp, modp, npre, npost, f2_in, f2_out, k=2, bm=bm, blocks_per_batch=t_len // bm)
        new_p.append((sp_re, sp_im))

        xs = _ffn(xs, mods, npre, npost, f1_in, f1_out, k=0, bm=dec_batch, blocks_per_batch=1)
        pr = _proj_sample(xs, mods, npre, w_all)
        u_s, q_s, k_s, v_s = pr[:, o_u:o_q], pr[:, o_q:o_k], pr[:, o_k:o_v], pr[:, o_v:o_qi]
        qi_s, ki_s, wi_s = pr[:, o_qi:o_ki], pr[:, o_ki:o_wi], pr[:, o_wi:o_ga]
        y_ssm_s, ss_re, ss_im = _ssm_sample(u_s, state_ssm_re[l], state_ssm_im[l], ops, ops["c_step"], ssm_d[l])
        hcol = lambda a: a.reshape(dec_batch, N_HEADS, HEAD_DIM, 1)
        sc_s = _idx_sample(page_table, qi_s.reshape(dec_batch, IDX_HEADS, IDX_DIM),
                           wi_s.reshape(dec_batch, IDX_HEADS, 1), ki_s.reshape(dec_batch, 1, IDX_DIM),
                           cache_idx_t, layer=l)
        thr_s, tie_s = _threshold_sample(sc_s.reshape(dec_batch, -1), n_keys=n_past + 1, topk=topk_s)
        attn_s = _attend_sample(page_table, sc_s[:, 0, :n_past].reshape(dec_batch, n_pages, PAGE_SIZE),
                                sc_s[:, :, n_past:n_past + 1], thr_s[:, :1].reshape(dec_batch, 1, 1),
                                tie_s[:, :1].reshape(dec_batch, 1, 1), hcol(q_s), hcol(k_s), hcol(v_s),
                                cache_k_t, cache_v_t, layer=l, topk=topk_s)
        xs = _mix(xs, mods, npre, npost, y_ssm_s, attn_s.reshape(dec_batch, d_att), g_w, g_v, wba, w_gates, wo,
                  bm=dec_batch, blocks_per_batch=1)
        xs = _ffn(xs, mods, npre, npost, f2_in, f2_out, k=2, bm=dec_batch, blocks_per_batch=1)
        new_s.append((k_s.reshape(dec_batch, 1, N_HEADS, HEAD_DIM), v_s.reshape(dec_batch, 1, N_HEADS, HEAD_DIM),
                      ki_s.reshape(dec_batch, 1, IDX_DIM), ss_re, ss_im))

    stack = lambda states, i: jnp.stack([s[i] for s in states])
    k_t, v_t, kidx_t = kv_t
    heads_last = lambda a: a.reshape(depth, batch, N_HEADS, HEAD_DIM, t_len).transpose(0, 1, 4, 2, 3)
    return (xp.reshape(batch, t_len, d), xs.reshape(dec_batch, 1, d),
            heads_last(k_t), heads_last(v_t), kidx_t.transpose(0, 1, 3, 2), stack(new_p, 0), stack(new_p, 1),
            stack(new_s, 0), stack(new_s, 1), stack(new_s, 2), stack(new_s, 3), stack(new_s, 4))
```

```python
import functools
import math

import jax
import jax.numpy as jnp
import numpy as np
from jax import lax
from jax.experimental import pallas as pl
from jax.experimental.pallas import tpu as pltpu

F32 = jnp.float32
BF16 = jnp.bfloat16
I32 = jnp.int32

EPS = 1e-6
SSM_GROUP = 16
SSM_N = 64
N_HEADS = 8
HEAD_DIM = 64
IDX_HEADS = 8
IDX_DIM = 64
INDEX_TOPK = 256
PAGE_SIZE = 128
IDX_SCALE = IDX_DIM ** -0.5 * IDX_HEADS ** -0.5
ATT_SCALE = HEAD_DIM ** -0.5

LANES = 128
SUBLANES = 8
VMEM_LIMIT_BYTES = 56 * 1024 * 1024
CHUNK = 8
OCTET = LANES // SSM_GROUP
QB = 256
NEG_BIG = float(np.finfo(np.float32).min)
FLOAT_MID_ITERS = 40
SEARCH_PROBES_PER_TEST = 4
MIN_NORMAL_KEY = 0x00800000


def _cparams(*sem):
    return pltpu.CompilerParams(dimension_semantics=sem, vmem_limit_bytes=VMEM_LIMIT_BYTES)


def _const_spec(shape):
    nd = len(shape)
    return pl.BlockSpec(shape, lambda *_: (0,) * nd, pipeline_mode=pl.Buffered(1))


def _rms(x, g):
    ms = jnp.mean(x * x, axis=-1, keepdims=True)
    return x * lax.rsqrt(ms + EPS) * g


def _dot(a, b):
    return jnp.dot(a, b, preferred_element_type=F32)


def _dot_nt(a, b):
    return lax.dot_general(a, b, (((1,), (1,)), ((), ())), preferred_element_type=F32)


def _mod_kernel(c_ref, w_ref, b_ref, o_ref):
    a = jax.nn.silu(c_ref[...]).astype(BF16)
    o_ref[...] = _dot(a, w_ref[...].astype(BF16)) + b_ref[...]


def _modulation(c_all, mod_w, mod_b):
    depth, d, n = mod_w.shape
    r = c_all.shape[0]
    tn = 1024
    return pl.pallas_call(
        _mod_kernel,
        grid=(depth, n // tn),
        in_specs=[
            pl.BlockSpec((r, d), lambda l, j: (0, 0)),
            pl.BlockSpec((None, d, tn), lambda l, j: (l, 0, j)),
            pl.BlockSpec((None, 1, tn), lambda l, j: (l, 0, j)),
        ],
        out_specs=pl.BlockSpec((None, r, tn), lambda l, j: (l, 0, j)),
        out_shape=jax.ShapeDtypeStruct((depth, r, n), F32),
        compiler_params=_cparams("parallel", "parallel"),
        name="modulation",
    )(c_all, mod_w, mod_b.reshape(depth, 1, n))


def _mod_spec(nb, d, rows_per_batch_block):
    return pl.BlockSpec((9, None, nb, d), lambda i: (0, i // rows_per_batch_block, 0, 0))


def _ffn_kernel(x_ref, mod_ref, npre_ref, npost_ref, win_ref, wout_ref, o_ref, h_ref, a_ref, *, k, dff, ck):
    x = x_ref[...]
    sh, sc, gt = mod_ref[3 * k], mod_ref[3 * k + 1], mod_ref[3 * k + 2]
    h = _rms(x, npre_ref[k:k + 1, :]) * (1.0 + sc) + sh
    h_ref[...] = h.astype(BF16)
    for c in range(dff // ck):
        hb = h_ref[...]
        g = _dot(hb, win_ref[:, c * ck:(c + 1) * ck])
        u = _dot(hb, win_ref[:, dff + c * ck:dff + (c + 1) * ck])
        a_ref[:, c * ck:(c + 1) * ck] = (jax.nn.silu(g) * u).astype(BF16)
    y = _dot(a_ref[...], wout_ref[...])
    o_ref[...] = x + 0.5 * gt * _rms(y, npost_ref[k:k + 1, :])


def _ffn(x, mod, npre, npost, w_in, w_out, *, k, bm, blocks_per_batch):
    m, d = x.shape
    dff = w_out.shape[0]
    nb = mod.shape[2]
    kern = functools.partial(_ffn_kernel, k=k, dff=dff, ck=256)
    return pl.pallas_call(
        kern,
        grid=(m // bm,),
        in_specs=[
            pl.BlockSpec((bm, d), lambda i: (i, 0)),
            _mod_spec(nb, d, blocks_per_batch),
            _const_spec(npre.shape),
            _const_spec(npost.shape),
            _const_spec(w_in.shape),
            _const_spec(w_out.shape),
        ],
        out_specs=pl.BlockSpec((bm, d), lambda i: (i, 0)),
        out_shape=jax.ShapeDtypeStruct((m, d), F32),
        scratch_shapes=[pltpu.VMEM((bm, d), BF16), pltpu.VMEM((bm, dff), BF16)],
        compiler_params=_cparams("parallel"),
        name=f"ffn{k}",
    )(x, mod, npre, npost, w_in, w_out)


def _proj_prompt_kernel(x_ref, mod_ref, npre_ref, wrow_ref, wt_ref, *refs, d_ssm, d_att, layer):
    earlier = refs[:3] if layer else ()
    u_ref, kh_ref, kidxb_ref, kt_all, vt_all, kit_all, v3_ref, qt_ref, qit_ref, wt_out_ref = refs[len(earlier):]
    for src, dst in zip(earlier, (kt_all, vt_all, kit_all)):
        dst[0:layer] = src[...]
    kt_ref, vt_ref, kit_ref = kt_all.at[layer], vt_all.at[layer], kit_all.at[layer]
    x = x_ref[...]
    sh, sc = mod_ref[3], mod_ref[4]
    h = (_rms(x, npre_ref[1:2, :]) * (1.0 + sc) + sh).astype(BF16)
    o_k = d_ssm + d_att
    o_ki = d_ssm + 3 * d_att + IDX_HEADS * IDX_DIM
    u_ref[...] = _dot(h, wrow_ref[:, 0:d_ssm])
    k = _dot(h, wrow_ref[:, o_k:o_k + d_att])
    kidxb_ref[...] = _dot(h, wrow_ref[:, o_ki:o_ki + IDX_DIM]).astype(BF16)
    for hh in range(N_HEADS):
        kh_ref[hh] = k[:, hh * HEAD_DIM:(hh + 1) * HEAD_DIM].astype(BF16)
    pt = _dot_nt(wt_ref[...], h)
    o = 0
    qt_ref[...] = (pt[o:o + d_att] * ATT_SCALE).astype(BF16)
    o += d_att
    kt_ref[...] = pt[o:o + d_att]
    o += d_att
    vt = pt[o:o + d_att]
    vt_ref[...] = vt
    vtb = vt.astype(BF16)
    for jj in range(v3_ref.shape[0]):
        v3_ref[jj] = vtb[:, jj * QB:(jj + 1) * QB]
    o += d_att
    qit_ref[...] = pt[o:o + IDX_HEADS * IDX_DIM].astype(BF16)
    o += IDX_HEADS * IDX_DIM
    kit_ref[...] = pt[o:o + IDX_DIM]
    o += IDX_DIM
    wt_out_ref[...] = pt[o:o + IDX_HEADS] * IDX_SCALE


def _proj_prompt(x, mod, npre, w_row, w_t, earlier, *, batch, bm, blocks_per_batch, d_ssm, d_att):
    m, d = x.shape
    t_len = m // batch
    nbt = blocks_per_batch
    layer = earlier[0].shape[0] if earlier else 0
    row = lambda w: pl.BlockSpec((bm, w), lambda i: (i, 0))
    col = lambda r: pl.BlockSpec((r, bm), lambda i: (0, i))
    lcol = lambda n, r: pl.BlockSpec((n, None, r, bm), lambda i: (0, i // nbt, 0, i % nbt))
    bcol = lambda r: lcol(layer + 1, r)
    kern = functools.partial(_proj_prompt_kernel, d_ssm=d_ssm, d_att=d_att, layer=layer)
    return pl.pallas_call(
        kern,
        grid=(m // bm,),
        in_specs=[
            pl.BlockSpec((bm, d), lambda i: (i, 0)),
            _mod_spec(1, d, blocks_per_batch),
            _const_spec(npre.shape),
            _const_spec(w_row.shape),
            _const_spec(w_t.shape),
        ] + [lcol(layer, a.shape[2]) for a in earlier],
        out_specs=[
            row(d_ssm),
            pl.BlockSpec((N_HEADS, bm, HEAD_DIM), lambda i: (0, i, 0)),
            row(IDX_DIM),
            bcol(d_att), bcol(d_att), bcol(IDX_DIM),
            pl.BlockSpec((bm // QB, d_att, QB), lambda i: (i, 0, 0)),
            col(d_att), col(d_att), col(IDX_HEADS),
        ],
        out_shape=[
            jax.ShapeDtypeStruct((m, d_ssm), F32),
            jax.ShapeDtypeStruct((N_HEADS, m, HEAD_DIM), BF16),
            jax.ShapeDtypeStruct((m, IDX_DIM), BF16),
            jax.ShapeDtypeStruct((layer + 1, batch, d_att, t_len), F32),
            jax.ShapeDtypeStruct((layer + 1, batch, d_att, t_len), F32),
            jax.ShapeDtypeStruct((layer + 1, batch, IDX_DIM, t_len), F32),
            jax.ShapeDtypeStruct((m // QB, d_att, QB), BF16),
            jax.ShapeDtypeStruct((d_att, m), BF16),
            jax.ShapeDtypeStruct((d_att, m), BF16),
            jax.ShapeDtypeStruct((IDX_HEADS, m), F32),
        ],
        compiler_params=_cparams("parallel"),
        name="proj_prompt",
    )(x, mod, npre, w_row, w_t, *earlier)


def _proj_sample_kernel(x_ref, mod_ref, npre_ref, w_ref, o_ref):
    x = x_ref[...]
    sh, sc = mod_ref[3], mod_ref[4]
    h = (_rms(x, npre_ref[1:2, :]) * (1.0 + sc) + sh).astype(BF16)
    o_ref[...] = _dot(h, w_ref[...])


def _proj_sample(x, mod, npre, w):
    m, d = x.shape
    n = w.shape[1]
    return pl.pallas_call(
        _proj_sample_kernel,
        grid=(1,),
        in_specs=[
            pl.BlockSpec((m, d), lambda i: (0, 0)),
            _mod_spec(m, d, 1),
            _const_spec(npre.shape),
            _const_spec(w.shape),
        ],
        out_specs=pl.BlockSpec((m, n), lambda i: (0, 0)),
        out_shape=jax.ShapeDtypeStruct((m, n), F32),
        compiler_params=_cparams("arbitrary"),
        name="proj_sample",
    )(x, mod, npre, w)


def _ssm_prep_kernel(ldt_ref, ar_ref, ai_ref, btre_ref, btim_ref, cre_ref, cim_ref,
                     kk_ref, wre_ref, wim_ref, pre_ref, pim_ref, al_ref, ab_ref, bbre_ref, bbim_ref):
    dt = jnp.exp(ldt_ref[...])
    ar, ai = ar_ref[...], ai_ref[...]
    power = []
    for k in range(CHUNK + 1):
        mag = jnp.exp(dt * ar * float(k))
        ph = dt * ai * float(k)
        power.append((mag * jnp.cos(ph), mag * jnp.sin(ph)))

    abr, abi = power[1]
    den = ar * ar + ai * ai
    z_re = ((abr - 1.0) * ar + abi * ai) / den
    z_im = (abi * ar - (abr - 1.0) * ai) / den
    b_re, b_im = btre_ref[...], btim_ref[...]
    bb_re = z_re * b_re - z_im * b_im
    bb_im = z_re * b_im + z_im * b_re
    bbre_ref[...] = bb_re
    bbim_ref[...] = bb_im
    ab_ref[0] = abr
    ab_ref[1] = abi
    al_ref[0], al_ref[1] = power[CHUNK]
    c_re, c_im = cre_ref[...], cim_ref[...]
    bnt = functools.partial(lax.dot_general, dimension_numbers=(((2,), (2,)), ((0,), (0,))),
                            precision=lax.Precision.HIGHEST, preferred_element_type=F32)
    for k in range(CHUNK):
        pr, pi = power[CHUNK - 1 - k]
        wre_ref[k] = pr * bb_re - pi * bb_im
        wim_ref[k] = pr * bb_im + pi * bb_re
        pr, pi = power[k]
        kk_ref[k] = bnt(bb_re, c_re * pr - c_im * pi) - bnt(bb_im, c_re * pi + c_im * pr)
        pr, pi = power[k + 1]
        pre_ref[k] = c_re * pr - c_im * pi
        pim_ref[k] = -(c_re * pi + c_im * pr)


def _ssm_prep(log_dt, a_re, a_im, b_re, b_im, c_re, c_im):
    g, n = a_re.shape
    j = SSM_GROUP
    shp = lambda *s: jax.ShapeDtypeStruct(s, F32)
    swap = lambda a: a.transpose(0, 2, 1)
    return pl.pallas_call(
        _ssm_prep_kernel,
        out_shape=[shp(CHUNK, g, j, j), shp(CHUNK, g, j, n), shp(CHUNK, g, j, n),
                   shp(CHUNK, g, j, n), shp(CHUNK, g, j, n), shp(2, g, 1, n), shp(2, g, 1, n),
                   shp(g, j, n), shp(g, j, n)],
        compiler_params=pltpu.CompilerParams(vmem_limit_bytes=VMEM_LIMIT_BYTES),
        name="ssm_prep",
    )(log_dt.reshape(g, 1, 1), a_re.reshape(g, 1, n), a_im.reshape(g, 1, n), swap(b_re), swap(b_im), c_re, c_im)


def _block_diag(x, groups):
    rows, c = x.shape[-2:]
    keep = (np.arange(rows)[:, None] // (rows // groups)) == (np.arange(groups * c)[None, :] // c)
    return jnp.where(keep, jnp.tile(x, (1,) * (x.ndim - 1) + (groups,)), 0.0)


def _ssm_assemble_kernel(kk_ref, wre_ref, wim_ref, pre_ref, pim_ref, m_ref, w_ref, p_ref):
    j, n = SSM_GROUP, SSM_N

    def block_diag(x, r, c):
        rows, cols = OCTET * r, OCTET * c
        rep = ((lax.broadcasted_iota(I32, (c, cols), 1) & (c - 1)) == lax.broadcasted_iota(I32, (c, cols), 0))
        keep = ((lax.broadcasted_iota(I32, (rows, cols), 0) >> int(math.log2(r)))
                == (lax.broadcasted_iota(I32, (rows, cols), 1) >> int(math.log2(c))))
        return jnp.where(keep, _dot(x.astype(BF16), rep.astype(BF16)), 0.0)

    m_ref[...] = jnp.zeros(m_ref.shape, BF16)
    blk = lambda i, w: slice(i * w, (i + 1) * w)
    for k in range(CHUNK):
        tile = block_diag(kk_ref[k].reshape(OCTET * j, j), j, j).astype(BF16)
        for tau in range(CHUNK - k):
            m_ref[blk(tau, LANES), blk(tau + k, LANES)] = tile
        for part, (w_src, p_src) in enumerate(((wre_ref, pre_ref), (wim_ref, pim_ref))):
            w_ref[blk(k, LANES), blk(part, OCTET * n)] = block_diag(w_src[k].reshape(OCTET * j, n), j, n).astype(BF16)
            p_t = block_diag(p_src[k].reshape(OCTET * j, n), j, n)
            p_ref[blk(part, OCTET * n), blk(k, LANES)] = jnp.transpose(p_t).astype(BF16)


def _ssm_assemble(kk, w_re, w_im, p_re, p_im):
    chunk, g, j, n = w_re.shape
    no, kw, sw = g // OCTET, chunk * LANES, 2 * OCTET * n
    grp = lambda a, b: pl.BlockSpec((chunk, OCTET, a, b), lambda o: (0, o, 0, 0))
    out = lambda a, b: pl.BlockSpec((None, a, b), lambda o: (o, 0, 0))
    return pl.pallas_call(
        _ssm_assemble_kernel,
        grid=(no,),
        in_specs=[grp(j, j), grp(j, n), grp(j, n), grp(j, n), grp(j, n)],
        out_specs=[out(kw, kw), out(kw, sw), out(sw, kw)],
        out_shape=[jax.ShapeDtypeStruct((no, kw, kw), BF16), jax.ShapeDtypeStruct((no, kw, sw), BF16),
                   jax.ShapeDtypeStruct((no, sw, kw), BF16)],
        compiler_params=_cparams("parallel"),
        name="ssm_assemble",
    )(kk, w_re, w_im, p_re, p_im)


def _ssm_operators(prep, c_re, c_im):
    kk, w_re, w_im, p_re, p_im, al, ab, bb_re, bb_im = prep
    g, j, n = bb_re.shape
    no = g // OCTET
    m_op, w_op, p_op = _ssm_assemble(kk, w_re, w_im, p_re, p_im)
    a_chunk = al.reshape(2, no, OCTET * n).transpose(1, 0, 2)
    b_step = _block_diag(jnp.stack([bb_re, bb_im], 0).reshape(2, g * j, n), g)
    b_step = b_step.transpose(1, 0, 2).reshape(g * j, 2 * g * n)
    c_t = jnp.stack([c_re, -c_im], 0).transpose(0, 1, 3, 2).reshape(2, g * n, j)
    c_step = _block_diag(c_t, g).reshape(2 * g * n, g * j).astype(BF16)
    return dict(m=m_op, w=w_op, p=p_op, a_chunk=a_chunk, b_step=b_step, c_step=c_step, abar=ab.reshape(2, g * n))


def _ssm_prompt_kernel(u_ref, m_ref, w_ref, p_ref, al_ref, d_ref, y_ref, sfin_ref,
                       uo_ref, v_ref, sc_ref, *, r, nb):
    half = sc_ref.shape[1] // 2
    rows_all = nb * r
    for tau in range(CHUNK):
        uo_ref[:, tau * LANES:(tau + 1) * LANES] = u_ref[pl.ds(tau, rows_all, stride=CHUNK), :].astype(BF16)
    uo = uo_ref[...]
    v_ref[...] = _dot(uo, w_ref[...])
    a_r, a_i = al_ref[0:1, :], al_ref[1:2, :]

    def step(c, carry):
        out = []
        for s, (s_r, s_i) in enumerate(carry):
            row = pl.ds(s * r + c, 1)
            sc_ref[row, 0:half] = s_r
            sc_ref[row, half:2 * half] = s_i
            v = v_ref[row, :]
            out.append((a_r * s_r - a_i * s_i + v[:, 0:half], a_r * s_i + a_i * s_r + v[:, half:2 * half]))
        return tuple(out)

    zero = jnp.zeros((1, half), F32)
    final = lax.fori_loop(0, r, step, tuple((zero, zero) for _ in range(nb)))
    for s, (s_r, s_i) in enumerate(final):
        sfin_ref[s, 0:1, :] = s_r
        sfin_ref[s, 1:2, :] = s_i
    y = _dot(uo, m_ref[...]) + _dot(sc_ref[...].astype(BF16), p_ref[...])
    d = d_ref[...]
    for t in range(CHUNK):
        rows = pl.ds(t, rows_all, stride=CHUNK)
        y_ref[rows, :] = y[:, t * LANES:(t + 1) * LANES] + d * u_ref[rows, :]


def _ssm_prompt(u, ops, d_skip, *, batch, t_len):
    m, d_ssm = u.shape
    no = d_ssm // LANES
    r = t_len // CHUNK
    kw = CHUNK * LANES
    sw = ops["w"].shape[2]
    nb = math.gcd(batch, 4)
    op_spec = lambda a, b: pl.BlockSpec((None, a, b), lambda o, bb: (o, 0, 0))
    y, sfin = pl.pallas_call(
        functools.partial(_ssm_prompt_kernel, r=r, nb=nb),
        grid=(no, batch // nb),
        in_specs=[
            pl.BlockSpec((nb * t_len, LANES), lambda o, bb: (bb, o)),
            op_spec(kw, kw), op_spec(kw, sw), op_spec(sw, kw),
            pl.BlockSpec((None, 2, sw // 2), lambda o, bb: (o, 0, 0)),
            pl.BlockSpec((1, LANES), lambda o, bb: (0, o)),
        ],
        out_specs=[
            pl.BlockSpec((nb * t_len, LANES), lambda o, bb: (bb, o)),
            pl.BlockSpec((nb, None, 2, sw // 2), lambda o, bb: (bb, o, 0, 0)),
        ],
        out_shape=[jax.ShapeDtypeStruct((m, d_ssm), F32),
                   jax.ShapeDtypeStruct((batch, no, 2, sw // 2), F32)],
        scratch_shapes=[pltpu.VMEM((nb * r, kw), BF16), pltpu.VMEM((nb * r, sw), F32), pltpu.VMEM((nb * r, sw), F32)],
        compiler_params=_cparams("parallel", "parallel"),
        name="ssm_prompt",
    )(u, ops["m"], ops["w"], ops["p"], ops["a_chunk"], d_skip.reshape(1, d_ssm))
    s = sfin.reshape(batch, no, 2, OCTET, SSM_N).transpose(2, 0, 1, 3, 4).reshape(2, batch, no * OCTET, SSM_N)
    return y, s[0], s[1]


def _ssm_sample_kernel(u_ref, h_ref, ab_ref, bstep_ref, cstep_ref, d_ref, y_ref, s_ref):
    u = u_ref[...]
    half = h_ref.shape[2]
    bu = jnp.dot(u, bstep_ref[...], precision=lax.Precision.HIGHEST, preferred_element_type=F32)
    a_r, a_i = ab_ref[0:1, :], ab_ref[1:2, :]
    h_r, h_i = h_ref[0], h_ref[1]
    s_r = a_r * h_r - a_i * h_i + bu[:, 0:half]
    s_i = a_r * h_i + a_i * h_r + bu[:, half:2 * half]
    s_ref[0] = s_r
    s_ref[1] = s_i
    s = jnp.concatenate([s_r, s_i], axis=1).astype(BF16)
    y_ref[...] = _dot(s, cstep_ref[...]) + d_ref[...] * u


def _ssm_sample(u, h_re, h_im, ops, c_step, d_skip):
    bsz, d_ssm = u.shape
    gn = h_re.shape[1] * h_re.shape[2]
    h = jnp.stack([h_re.reshape(bsz, gn), h_im.reshape(bsz, gn)], 0)
    y, s = pl.pallas_call(
        _ssm_sample_kernel,
        out_shape=[jax.ShapeDtypeStruct((bsz, d_ssm), F32), jax.ShapeDtypeStruct((2, bsz, gn), F32)],
        compiler_params=pltpu.CompilerParams(vmem_limit_bytes=VMEM_LIMIT_BYTES),
        name="ssm_sample",
    )(u, h, ops["abar"], ops["b_step"], c_step, d_skip.reshape(1, d_ssm))
    return y, s[0].reshape(h_re.shape), s[1].reshape(h_im.shape)


def _f2key(x):
    b = lax.bitcast_convert_type(x, I32)
    return b ^ ((b >> 31) & 0x7FFFFFFF)


def _key2f(k):
    return lax.bitcast_convert_type(k ^ ((k >> 31) & 0x7FFFFFFF), F32)


def _search_init(mn, mx, n_valid, topk):
    z = jnp.zeros_like(n_valid)
    half = topk + 0.5
    return (_f2key(mn), _f2key(mx) + 1, n_valid.astype(F32) - half, jnp.full(mn.shape, half, F32),
            jnp.full(mn.shape, NEG_BIG, F32), jnp.where(n_valid > topk, 0, 1).astype(I32), z, z)


def _search_probe(state, it, topk):
    lo, hi, f_lo, f_hi = state[:4]
    lo_f, hi_f = _key2f(lo), _key2f(hi)
    frac = jnp.where(it % 4 == 3, 0.5, f_lo / (f_lo + f_hi))
    cand = _f2key(lo_f + (hi_f - lo_f) * frac)
    mid_k = (lo >> 1) + (hi >> 1) + (lo & hi & 1)
    inside = (cand > lo) & (cand < hi) & (it < FLOAT_MID_ITERS)
    probe = jnp.where(inside, cand, mid_k)
    zero_k = jnp.where(it == 0, 0, MIN_NORMAL_KEY)
    return jnp.where((it < 2) & (zero_k > lo) & (zero_k < hi), zero_k, probe)


def _search_update(state, mid, cnt, topk):
    lo, hi, f_lo, f_hi, thr, done, tie, last = state
    hit = cnt == topk
    up, dn = cnt > topk, cnt < topk
    lo_n, hi_n = jnp.where(up, mid, lo), jnp.where(dn, mid, hi)
    adj = ((hi_n == lo_n + 1) | ((lo_n == 0) & (hi_n == MIN_NORMAL_KEY))) & jnp.logical_not(hit)
    fin = hit | adj
    thr_n = jnp.where(hit, _key2f(mid), _key2f(lo_n))
    act = done == 0
    lo = jnp.where(act, lo_n, lo)
    hi = jnp.where(act, hi_n, hi)
    miss = cnt.astype(F32) - (topk + 0.5)
    f_lo = jnp.where(act, jnp.where(up, miss, jnp.where(dn & (last == -1), 0.5 * f_lo, f_lo)), f_lo)
    f_hi = jnp.where(act, jnp.where(dn, -miss, jnp.where(up & (last == 1), 0.5 * f_hi, f_hi)), f_hi)
    last = jnp.where(act, jnp.where(up, 1, jnp.where(dn, -1, last)), last)
    thr = jnp.where(act & fin, thr_n, thr)
    tie = jnp.where(act & adj, 1, tie)
    done = jnp.where(act & fin, 1, done)
    return lo, hi, f_lo, f_hi, thr, done, tie, last


def _search(count_ge, init, topk):
    def pending(state, it):
        return jnp.logical_and(jnp.min(state[5].astype(F32)) == 0.0, it < 96)

    def probe(u, carry):
        it, state = carry
        mid = _search_probe(state, it, topk)
        return it + 1, _search_update(state, mid, count_ge(_key2f(mid)), topk)

    def body(carry):
        it, state = lax.fori_loop(0, SEARCH_PROBES_PER_TEST, probe, carry[1:])
        return pending(state, it), it, state

    _, _, state = lax.while_loop(lambda c: c[0], body, (pending(init, 0), jnp.int32(0), init))
    return state[4], state[6]


def _attn_prompt_kernel(qt_ref, qit_ref, wt_ref, kh_ref, kidx_ref, v3_ref, o_ref, sc_ref, lg_ref, acc_ref, *, topk):
    i = pl.program_id(1)
    nk = i + 1
    qpos = i * QB + lax.broadcasted_iota(I32, (QB, QB), 1)
    krow = lax.broadcasted_iota(I32, (QB, QB), 0)
    fold = lambda x: x.reshape(QB // SUBLANES, SUBLANES, QB)

    tile_rows = lambda j: pl.ds(pl.multiple_of(j * QB, QB), QB)
    rep = lambda x: jnp.broadcast_to(x, (SUBLANES, QB))

    def score_tile(j, carry):
        mn, mx = carry
        kx = kidx_ref[tile_rows(j), :]
        acc = jnp.zeros((QB, QB), F32)
        for h in range(IDX_HEADS):
            s = _dot(kx, qit_ref[h * IDX_DIM:(h + 1) * IDX_DIM, :])
            acc = acc + jnp.maximum(s, 0.0) * wt_ref[h:h + 1, :]
        valid = (j * QB + krow) <= qpos
        sc = jnp.where(valid, acc, -jnp.inf)
        sc_ref[j] = sc
        mx = jnp.maximum(mx, jnp.max(fold(sc), axis=0))
        mn = jnp.minimum(mn, jnp.min(fold(jnp.where(valid, acc, jnp.inf)), axis=0))
        return mn, mx

    mn, mx = lax.fori_loop(0, nk, score_tile,
                           (jnp.full((SUBLANES, QB), jnp.inf, F32), jnp.full((SUBLANES, QB), -jnp.inf, F32)))
    mn = rep(jnp.min(mn, axis=0, keepdims=True))
    mx = rep(jnp.max(mx, axis=0, keepdims=True))

    def count_ge(t):
        t1 = t[0:1, :]

        def body(j, c):
            return c + jnp.sum(fold((sc_ref[j] >= t1).astype(I32)), axis=0)

        c = lax.fori_loop(0, nk, body, jnp.zeros((SUBLANES, QB), I32))
        return rep(jnp.sum(c, axis=0, keepdims=True))

    n_valid = qpos[0:SUBLANES, :] + 1
    thr, tie = _search(count_ge, _search_init(mn, mx, n_valid, topk), topk)

    @pl.when(jnp.max(tie) > 0)
    def _():
        tri = (lax.broadcasted_iota(I32, (QB, QB), 0) >= lax.broadcasted_iota(I32, (QB, QB), 1)).astype(BF16)
        thr1, tie1 = thr[0:1, :], tie[0:1, :] > 0

        def gt_tile(j, c):
            return c + jnp.sum((sc_ref[j] > thr1).astype(I32), axis=0, keepdims=True)

        quota = topk - lax.fori_loop(0, nk, gt_tile, jnp.zeros((1, QB), I32))

        def tie_tile(j, before):
            x = sc_ref[j]
            t = (x == thr1) & tie1
            rank = before + _dot(tri, t.astype(BF16)).astype(I32)
            sc_ref[j] = jnp.where(t & (rank > quota), -jnp.inf, x)
            return before + jnp.sum(t.astype(I32), axis=0, keepdims=True)

        lax.fori_loop(0, nk, tie_tile, jnp.zeros((1, QB), I32))

    thr1 = thr[0:1, :]
    heads = range(N_HEADS)
    hslice = lambda h: slice(h * HEAD_DIM, (h + 1) * HEAD_DIM)

    colmax = lambda x: jnp.max(jnp.max(fold(x), axis=0), axis=0, keepdims=True)
    colsum = lambda x: jnp.sum(jnp.sum(fold(x), axis=0), axis=0, keepdims=True)
    rows_of = lambda rows: jnp.concatenate(rows, axis=0)

    def logits(j, slot, m):
        bias = jnp.where(sc_ref[j] >= thr1, 0.0, -jnp.inf)
        rows = tile_rows(j)
        out = []
        for h in heads:
            lg = _dot(kh_ref[h, rows, :], qt_ref[hslice(h), :]) + bias
            lg_ref[h, slot] = lg
            out.append(jnp.maximum(m[h:h + 1, :], colmax(lg)))
        return rows_of(out)

    def values(j, slot, m_old, m_new, l):
        m_safe = jnp.where(m_new == -jnp.inf, 0.0, m_new)
        alpha = jnp.exp(m_old - m_safe)
        out = []
        for h in heads:
            p = jnp.exp(lg_ref[h, slot] - m_safe[h:h + 1, :])
            acc_ref[h] = acc_ref[h] * alpha[h:h + 1, :] + _dot(v3_ref[j, hslice(h), :], p.astype(BF16))
            out.append(colsum(p))
        return l * alpha + rows_of(out)

    acc_ref[...] = jnp.zeros(acc_ref.shape, F32)
    neg = jnp.full((N_HEADS, QB), -jnp.inf, F32)

    def step(j, slot, carry):
        m_before, m_upto, l = carry
        m_next = logits(j + 1, 1 - slot, m_upto)
        return m_upto, m_next, values(j, slot, m_before, m_upto, l)

    def two_steps(jj, carry):
        return step(2 * jj + 1, 1, step(2 * jj, 0, carry))

    carry = (neg, logits(0, 0, neg), jnp.zeros((N_HEADS, QB), F32))
    carry = lax.fori_loop(0, (nk - 1) // 2, two_steps, carry)
    odd = (nk - 1) % 2 == 1
    carry = lax.cond(odd, lambda c: step(nk - 2, 0, c), lambda c: c, carry)
    l = lax.cond(odd, lambda c: values(nk - 1, 1, *c), lambda c: values(nk - 1, 0, *c), carry)
    outs = [acc_ref[h] / l[h:h + 1, :] for h in heads]
    o_ref[...] = jnp.transpose(jnp.concatenate(outs, axis=0)).astype(BF16)


def _attn_prompt(qt, qit, wt, kh, kidxb, v3, *, batch, t_len, topk):
    d_att, m = qt.shape
    nq = t_len // QB
    return pl.pallas_call(
        functools.partial(_attn_prompt_kernel, topk=topk),
        grid=(batch, nq),
        in_specs=[
            pl.BlockSpec((d_att, QB), lambda b, i: (0, b * nq + i)),
            pl.BlockSpec((d_att, QB), lambda b, i: (0, b * nq + i)),
            pl.BlockSpec((IDX_HEADS, QB), lambda b, i: (0, b * nq + i)),
            pl.BlockSpec((N_HEADS, t_len, HEAD_DIM), lambda b, i: (0, b, 0)),
            pl.BlockSpec((t_len, IDX_DIM), lambda b, i: (b, 0)),
            pl.BlockSpec((nq, d_att, QB), lambda b, i: (b, 0, 0)),
        ],
        out_specs=pl.BlockSpec((QB, d_att), lambda b, i: (b * nq + i, 0)),
        out_shape=jax.ShapeDtypeStruct((m, d_att), BF16),
        scratch_shapes=[pltpu.VMEM((nq, QB, QB), F32), pltpu.VMEM((N_HEADS, 2, QB, QB), F32),
                        pltpu.VMEM((N_HEADS, HEAD_DIM, QB), F32)],
        compiler_params=_cparams("parallel", "arbitrary"),
        name="attn_prompt",
    )(qt, qit, wt, kh, kidxb, v3)


def _idx_sample_kernel(pt_ref, qi_ref, w_ref, kin_ref, ci_ref, o_ref, ibuf, sem, *, layer, n_pages):
    b = pl.program_id(0)
    slot = b % 2

    def page_copy(buf_slot, p, page):
        lanes = pl.ds(pl.multiple_of(p * PAGE_SIZE, PAGE_SIZE), PAGE_SIZE)
        return pltpu.make_async_copy(ci_ref.at[layer, page], ibuf.at[buf_slot, :, lanes], sem.at[buf_slot])

    def start_all(sample, buf_slot):
        def start(p, _):
            page_copy(buf_slot, p, pt_ref[sample, p]).start()
            return 0
        lax.fori_loop(0, n_pages, start, 0, unroll=SUBLANES)

    @pl.when(b == 0)
    def _():
        start_all(b, slot)

    @pl.when(b + 1 < pl.num_programs(0))
    def _():
        start_all(b + 1, 1 - slot)

    def wait(p, _):
        page_copy(slot, p, 0).wait()
        return 0

    lax.fori_loop(0, n_pages, wait, 0, unroll=SUBLANES)

    qi = qi_ref[...].astype(BF16)
    w = w_ref[...] * IDX_SCALE
    n_past = n_pages * PAGE_SIZE
    step = SUBLANES * PAGE_SIZE
    for c in range(n_past // step):
        cols = slice(c * step, (c + 1) * step)
        s = _dot(qi, ibuf[slot, :, cols].astype(BF16))
        o_ref[:, cols] = jnp.sum(jnp.maximum(s, 0.0) * w, axis=0, keepdims=True)
    s_new = jnp.sum(qi.astype(F32) * kin_ref[...].astype(BF16).astype(F32), axis=1, keepdims=True)
    s_new = jnp.sum(jnp.maximum(s_new, 0.0) * w, axis=0, keepdims=True)
    lane = lax.broadcasted_iota(I32, (1, PAGE_SIZE), 1)
    o_ref[:, n_past:n_past + PAGE_SIZE] = jnp.where(lane == 0, s_new, -jnp.inf)


def _idx_sample(page_table, q_idx, w_idx, k_idx_new, cache_idx_t, *, layer):
    bsz, n_pages = page_table.shape
    width = (n_pages + 1) * PAGE_SIZE
    blk = lambda r, c: pl.BlockSpec((None, r, c), lambda b, pt: (b, 0, 0))
    return pl.pallas_call(
        functools.partial(_idx_sample_kernel, layer=layer, n_pages=n_pages),
        grid_spec=pltpu.PrefetchScalarGridSpec(
            num_scalar_prefetch=1,
            grid=(bsz,),
            in_specs=[blk(IDX_HEADS, IDX_DIM), blk(IDX_HEADS, 1), blk(1, IDX_DIM), pl.BlockSpec(memory_space=pl.ANY)],
            out_specs=blk(1, width),
            scratch_shapes=[pltpu.VMEM((2, IDX_DIM, n_pages * PAGE_SIZE), F32), pltpu.SemaphoreType.DMA((2,))],
        ),
        out_shape=jax.ShapeDtypeStruct((bsz, 1, width), F32),
        compiler_params=_cparams("arbitrary"),
        name="idx_sample",
    )(page_table, q_idx, w_idx, k_idx_new, cache_idx_t)


def _threshold_sample_kernel(sc_ref, thr_ref, tie_ref, *, n_keys, topk):
    sc = sc_ref[...]
    mx = jnp.max(sc, axis=1, keepdims=True)
    mn = jnp.min(jnp.where(sc == -jnp.inf, jnp.inf, sc), axis=1, keepdims=True)

    def count_ge(t):
        return jnp.sum((sc_ref[...] >= t).astype(I32), axis=1, keepdims=True)

    n = jnp.full(mn.shape, n_keys, I32)
    thr, tie = _search(count_ge, _search_init(mn, mx, n, topk), topk)
    thr_ref[...] = jnp.broadcast_to(thr, thr_ref.shape)
    tie_ref[...] = jnp.broadcast_to(tie, tie_ref.shape)


def _threshold_sample(scores, *, n_keys, topk):
    bsz = scores.shape[0]
    return pl.pallas_call(
        functools.partial(_threshold_sample_kernel, n_keys=n_keys, topk=topk),
        out_shape=[jax.ShapeDtypeStruct((bsz, LANES), F32), jax.ShapeDtypeStruct((bsz, LANES), I32)],
        compiler_params=pltpu.CompilerParams(vmem_limit_bytes=VMEM_LIMIT_BYTES),
        name="threshold_sample",
    )(scores)


def _attend_sample_kernel(pt_ref, sc_ref, snew_ref, thr_ref, tie_ref, q_ref, kn_ref, vn_ref, ck_ref, cv_ref, o_ref,
                          kbuf, vbuf, qb, lg_ref, sem_k, sem_v, *, layer, n_pages, topk):
    b = pl.program_id(0)

    def copies(src_ref, buf, sem, p, page):
        return pltpu.make_async_copy(src_ref.at[layer, page], buf.at[p], sem)

    streams = ((ck_ref, kbuf, sem_k), (cv_ref, vbuf, sem_v))

    def start_all(src_ref, buf, sem, sample):
        def start(p, _):
            copies(src_ref, buf, sem, p, pt_ref[sample, p]).start()
            return 0
        lax.fori_loop(0, n_pages, start, 0)

    @pl.when(b == 0)
    def _():
        start_all(*streams[0], b)

    start_all(*streams[1], b)

    def wait_all(src_ref, buf, sem):
        def wait(p, _):
            copies(src_ref, buf, sem, p, 0).wait()
            return 0
        lax.fori_loop(0, n_pages, wait, 0)

    for h in range(N_HEADS):
        qb[h] = jnp.broadcast_to(q_ref[h] * ATT_SCALE, (HEAD_DIM, PAGE_SIZE))
    dsum = lambda x: jnp.sum(x, axis=0, keepdims=True)
    sc = sc_ref[...]
    s_new, thr, tie = snew_ref[...], thr_ref[...], tie_ref[...]

    def total(x, op=jnp.sum):
        return op(op(x, axis=1, keepdims=True), axis=0, keepdims=True)

    upper = (lax.broadcasted_iota(I32, (PAGE_SIZE, PAGE_SIZE), 0)
             <= lax.broadcasted_iota(I32, (PAGE_SIZE, PAGE_SIZE), 1)).astype(BF16)
    lower = (lax.broadcasted_iota(I32, (n_pages, n_pages), 1)
             < lax.broadcasted_iota(I32, (n_pages, n_pages), 0)).astype(BF16)

    def flat_rank(mask):
        mb = mask.astype(BF16)
        incl = _dot(mb, upper)
        before = jnp.sum(_dot(lower, mb), axis=1, keepdims=True)
        return (incl + before).astype(I32)

    gt = sc > thr
    tied = tie > 0
    is_tie = (sc == thr) & tied
    quota = jnp.where(tied, topk - total(gt.astype(I32)) - (s_new > thr).astype(I32), topk)
    sel = gt | ((sc == thr) & (flat_rank(is_tie) <= quota))
    n_tie_past = total(is_tie.astype(I32))
    new_sel = (s_new > thr) | ((s_new == thr) & (n_tie_past < quota))

    wait_all(*streams[0])
    for h in range(N_HEADS):
        q_h = qb[h]

        def k_page(p, _, h=h, q_h=q_h):
            lg_ref[h, pl.ds(p, 1), :] = dsum(kbuf[p, h] * q_h)
            return 0

        lax.fori_loop(0, n_pages, k_page, 0, unroll=4)

    @pl.when(b + 1 < pl.num_programs(0))
    def _():
        start_all(*streams[0], b + 1)

    red = lambda x, op: op(x, axis=(1, 2), keepdims=True)
    lg = jnp.where(sel[None], lg_ref[...], -jnp.inf)
    lg_new = jnp.sum(q_ref[...] * ATT_SCALE * kn_ref[...], axis=1, keepdims=True)
    lg_new = jnp.where(new_sel[None], lg_new, -jnp.inf)
    m = jnp.maximum(red(lg, jnp.max), lg_new)
    p = jnp.exp(lg - m)
    e_new = jnp.exp(lg_new - m)
    denom = red(p, jnp.sum) + e_new
    lg_ref[...] = p / denom
    p_new = e_new / denom

    wait_all(*streams[1])
    for h in range(N_HEADS):
        def v_page(p, acc, h=h):
            return acc + vbuf[p, h] * lg_ref[h, pl.ds(p, 1), :]

        acc = lax.fori_loop(0, n_pages, v_page, jnp.zeros((HEAD_DIM, PAGE_SIZE), F32), unroll=4)
        o_ref[h] = jnp.sum(acc, axis=1, keepdims=True) + p_new[h] * vn_ref[h]


def _attend_sample(page_table, scores, s_new, thr, tie, q, k_new, v_new, cache_k_t, cache_v_t, *, layer, topk):
    bsz, n_pages = page_table.shape
    one = pl.BlockSpec((None, 1, 1), lambda b, pt: (b, 0, 0))
    hcol = pl.BlockSpec((None, N_HEADS, HEAD_DIM, 1), lambda b, pt: (b, 0, 0, 0))
    any_spec = pl.BlockSpec(memory_space=pl.ANY)
    return pl.pallas_call(
        functools.partial(_attend_sample_kernel, layer=layer, n_pages=n_pages, topk=topk),
        grid_spec=pltpu.PrefetchScalarGridSpec(
            num_scalar_prefetch=1,
            grid=(bsz,),
            in_specs=[pl.BlockSpec((None, n_pages, PAGE_SIZE), lambda b, pt: (b, 0, 0)), one, one, one,
                      hcol, hcol, hcol, any_spec, any_spec],
            out_specs=hcol,
            scratch_shapes=[pltpu.VMEM((n_pages, N_HEADS, HEAD_DIM, PAGE_SIZE), F32),
                            pltpu.VMEM((n_pages, N_HEADS, HEAD_DIM, PAGE_SIZE), F32),
                            pltpu.VMEM((N_HEADS, HEAD_DIM, PAGE_SIZE), F32),
                            pltpu.VMEM((N_HEADS, n_pages, PAGE_SIZE), F32),
                            pltpu.SemaphoreType.DMA(()), pltpu.SemaphoreType.DMA(())],
        ),
        out_shape=jax.ShapeDtypeStruct((bsz, N_HEADS, HEAD_DIM, 1), F32),
        compiler_params=_cparams("arbitrary"),
        name="attend_sample",
    )(page_table, scores, s_new, thr, tie, q, k_new, v_new, cache_k_t, cache_v_t)


def _mix_kernel(x_ref, mod_ref, npre_ref, npost_ref, ys_ref, at_ref, gw_ref, gv_ref, wba_ref, wg_ref, wo_ref,
                o_ref):
    x = x_ref[...]
    d = x.shape[1]
    sh, sc, gt = mod_ref[3], mod_ref[4], mod_ref[5]
    h = (_rms(x, npre_ref[1:2, :]) * (1.0 + sc) + sh).astype(BF16)
    gates = _dot(h, wg_ref[...])
    ys = ys_ref[...].astype(BF16)
    y_a = _dot(ys, gw_ref[...]) * jax.nn.sigmoid(_dot(ys, gv_ref[...]))
    y_b = _dot(at_ref[...].astype(BF16), wba_ref[...])
    mixed = jax.nn.sigmoid(gates[:, 0:d]) * y_a + jax.nn.sigmoid(gates[:, d:2 * d]) * y_b
    y = _dot(mixed.astype(BF16), wo_ref[...])
    o_ref[...] = x + gt * _rms(y, npost_ref[1:2, :])


def _mix(x, mod, npre, npost, y_ssm, attn, glu_w, glu_v, wba, w_gates, w_out, *, bm, blocks_per_batch):
    m, d = x.shape
    nb = mod.shape[2]
    return pl.pallas_call(
        _mix_kernel,
        grid=(m // bm,),
        in_specs=[
            pl.BlockSpec((bm, d), lambda i: (i, 0)),
            _mod_spec(nb, d, blocks_per_batch),
            _const_spec(npre.shape), _const_spec(npost.shape),
            pl.BlockSpec((bm, y_ssm.shape[1]), lambda i: (i, 0)),
            pl.BlockSpec((bm, attn.shape[1]), lambda i: (i, 0)),
            _const_spec(glu_w.shape), _const_spec(glu_v.shape), _const_spec(wba.shape),
            _const_spec(w_gates.shape), _const_spec(w_out.shape),
        ],
        out_specs=pl.BlockSpec((bm, d), lambda i: (i, 0)),
        out_shape=jax.ShapeDtypeStruct((m, d), F32),
        compiler_params=_cparams("parallel"),
        name="mix",
    )(x, mod, npre, npost, y_ssm, attn, glu_w, glu_v, wba, w_gates, w_out)


def _pad_cols(w, n):
    return jnp.pad(w, ((0, 0), (0, n - w.shape[1])))


def kernel(x_prompt, x_sample, cache_k, cache_v, cache_idx_k, state_ssm_re, state_ssm_im, page_table,
           c_prompt, c_sample, mod_w, mod_b, norm_pre, norm_post, ffn1_in, ffn1_out, w_in,
           ssm_log_dt, ssm_a_re, ssm_a_im, ssm_b_re, ssm_b_im, ssm_c_re, ssm_c_im, ssm_d,
           glu_w, glu_v, w_branch_attn, w_out, ffn2_in, ffn2_out):
    batch, t_len, d = x_prompt.shape
    dec_batch, dec_seq, _ = x_sample.shape
    depth = mod_w.shape[0]
    d_ssm = ssm_d.shape[1]
    d_att = N_HEADS * HEAD_DIM
    n_groups = d_ssm // SSM_GROUP
    assert dec_seq == 1 and t_len % QB == 0 and d_ssm % LANES == 0
    m = batch * t_len
    bm = 512 if m % 512 == 0 else QB
    topk_p = min(INDEX_TOPK, t_len // 4)
    n_pages = page_table.shape[1]
    n_past = n_pages * PAGE_SIZE
    assert n_pages % SUBLANES == 0
    topk_s = min(INDEX_TOPK, (n_past + dec_seq) // 4)

    mod = _modulation(jnp.concatenate([c_prompt, c_sample], 0), mod_w, mod_b).reshape(depth, -1, 9, d)

    widths = (d_ssm, d_att, d_att, d_att, IDX_HEADS * IDX_DIM, IDX_DIM, IDX_HEADS, d, d)
    off = np.concatenate([[0], np.cumsum(widths)])
    o_u, o_q, o_k, o_v, o_qi, o_ki, o_wi, o_ga, o_gb, o_end = (int(v) for v in off)

    cache_k_t = cache_k.transpose(0, 1, 3, 4, 2)
    cache_v_t = cache_v.transpose(0, 1, 3, 4, 2)
    cache_idx_t = cache_idx_k.transpose(0, 1, 3, 2)

    xp = x_prompt.reshape(m, d)
    xs = x_sample.reshape(dec_batch, d)
    new_p, new_s = [], []
    kv_t = ()
    for l in range(depth):
        bf = lambda w: w[l].astype(BF16)
        modp = mod[l, :batch].transpose(1, 0, 2).reshape(9, batch, 1, d)
        mods = mod[l, batch:].transpose(1, 0, 2).reshape(9, 1, dec_batch, d)
        npre, npost = norm_pre[l], norm_post[l]
        f1_in, f1_out, f2_in, f2_out = bf(ffn1_in), bf(ffn1_out), bf(ffn2_in), bf(ffn2_out)
        wl = w_in[l]
        w_t = wl[:, o_q:o_ga].T.astype(BF16)
        w_gates = wl[:, o_ga:o_end].astype(BF16)
        w_all = _pad_cols(wl, 37 * LANES).astype(BF16)
        g_w, g_v, wba, wo = bf(glu_w), bf(glu_v), bf(w_branch_attn), bf(w_out)
        prep = _ssm_prep(ssm_log_dt[l], ssm_a_re[l], ssm_a_im[l], ssm_b_re[l], ssm_b_im[l],
                         ssm_c_re[l], ssm_c_im[l])
        ops = _ssm_operators(prep, ssm_c_re[l], ssm_c_im[l])

        xp = _ffn(xp, modp, npre, npost, f1_in, f1_out, k=0, bm=bm, blocks_per_batch=t_len // bm)
        u, kh, kidxb, *kv_t, v3, qt, qit, wt = _proj_prompt(
            xp, modp, npre, w_all, w_t, tuple(kv_t), batch=batch, bm=bm, blocks_per_batch=t_len // bm,
            d_ssm=d_ssm, d_att=d_att)
        y_ssm, sp_re, sp_im = _ssm_prompt(u, ops, ssm_d[l], batch=batch, t_len=t_len)
        attn = _attn_prompt(qt, qit, wt, kh, kidxb, v3, batch=batch, t_len=t_len, topk=topk_p)
        xp = _mix(xp, modp, npre, npost, y_ssm, attn, g_w, g_v, wba, w_gates, wo,
                  bm=bm, blocks_per_batch=t_len // bm)
        xp = _ffn(xp, modp, npre, npost, f2_in, f2_out, k=2, bm=bm, blocks_per_batch=t_len // bm)
        new_p.append((sp_re, sp_im))

        xs = _ffn(xs, mods, npre, npost, f1_in, f1_out, k=0, bm=dec_batch, blocks_per_batch=1)
        pr = _proj_sample(xs, mods, npre, w_all)
        u_s, q_s, k_s, v_s = pr[:, o_u:o_q], pr[:, o_q:o_k], pr[:, o_k:o_v], pr[:, o_v:o_qi]
        qi_s, ki_s, wi_s = pr[:, o_qi:o_ki], pr[:, o_ki:o_wi], pr[:, o_wi:o_ga]
        y_ssm_s, ss_re, ss_im = _ssm_sample(u_s, state_ssm_re[l], state_ssm_im[l], ops, ops["c_step"], ssm_d[l])
        hcol = lambda a: a.reshape(dec_batch, N_HEADS, HEAD_DIM, 1)
        sc_s = _idx_sample(page_table, qi_s.reshape(dec_batch, IDX_HEADS, IDX_DIM),
                           wi_s.reshape(dec_batch, IDX_HEADS, 1), ki_s.reshape(dec_batch, 1, IDX_DIM),
                           cache_idx_t, layer=l)
        thr_s, tie_s = _threshold_sample(sc_s.reshape(dec_batch, -1), n_keys=n_past + 1, topk=topk_s)
        attn_s = _attend_sample(page_table, sc_s[:, 0, :n_past].reshape(dec_batch, n_pages, PAGE_SIZE),
                                sc_s[:, :, n_past:n_past + 1], thr_s[:, :1].reshape(dec_batch, 1, 1),
                                tie_s[:, :1].reshape(dec_batch, 1, 1), hcol(q_s), hcol(k_s), hcol(v_s),
                                cache_k_t, cache_v_t, layer=l, topk=topk_s)
        xs = _mix(xs, mods, npre, npost, y_ssm_s, attn_s.reshape(dec_batch, d_att), g_w, g_v, wba, w_gates, wo,
                  bm=dec_batch, blocks_per_batch=1)
        xs = _ffn(xs, mods, npre, npost, f2_in, f2_out, k=2, bm=dec_batch, blocks_per_batch=1)
        new_s.append((k_s.reshape(dec_batch, 1, N_HEADS, HEAD_DIM), v_s.reshape(dec_batch, 1, N_HEADS, HEAD_DIM),
                      ki_s.reshape(dec_batch, 1, IDX_DIM), ss_re, ss_im))

    stack = lambda states, i: jnp.stack([s[i] for s in states])
    k_t, v_t, kidx_t = kv_t
    heads_last = lambda a: a.reshape(depth, batch, N_HEADS, HEAD_DIM, t_len).transpose(0, 1, 4, 2, 3)
    return (xp.reshape(batch, t_len, d), xs.reshape(dec_batch, 1, d),
            heads_last(k_t), heads_last(v_t), kidx_t.transpose(0, 1, 3, 2), stack(new_p, 0), stack(new_p, 1),
            stack(new_s, 0), stack(new_s, 1), stack(new_s, 2), stack(new_s, 3), stack(new_s, 4))
```

```python
import functools
import math

import jax
import jax.numpy as jnp
import numpy as np
from jax import lax
from jax.experimental import pallas as pl
from jax.experimental.pallas import tpu as pltpu

F32 = jnp.float32
BF16 = jnp.bfloat16
I32 = jnp.int32

EPS = 1e-6
SSM_GROUP = 16
SSM_N = 64
N_HEADS = 8
HEAD_DIM = 64
IDX_HEADS = 8
IDX_DIM = 64
INDEX_TOPK = 256
PAGE_SIZE = 128
IDX_SCALE = IDX_DIM ** -0.5 * IDX_HEADS ** -0.5
ATT_SCALE = HEAD_DIM ** -0.5

LANES = 128
SUBLANES = 8
VMEM_LIMIT_BYTES = 56 * 1024 * 1024
CHUNK = 8
OCTET = LANES // SSM_GROUP
QB = 256
NEG_BIG = float(np.finfo(np.float32).min)
FLOAT_MID_ITERS = 40
SEARCH_PROBES_PER_TEST = 4
MIN_NORMAL_KEY = 0x00800000


def _cparams(*sem):
    return pltpu.CompilerParams(dimension_semantics=sem, vmem_limit_bytes=VMEM_LIMIT_BYTES)


def _const_spec(shape):
    nd = len(shape)
    return pl.BlockSpec(shape, lambda *_: (0,) * nd, pipeline_mode=pl.Buffered(1))


def _rms(x, g):
    ms = jnp.mean(x * x, axis=-1, keepdims=True)
    return x * lax.rsqrt(ms + EPS) * g


def _dot(a, b):
    return jnp.dot(a, b, preferred_element_type=F32)


def _dot_nt(a, b):
    return lax.dot_general(a, b, (((1,), (1,)), ((), ())), preferred_element_type=F32)


def _mod_kernel(c_ref, w_ref, b_ref, o_ref):
    a = jax.nn.silu(c_ref[...]).astype(BF16)
    o_ref[...] = _dot(a, w_ref[...].astype(BF16)) + b_ref[...]


def _modulation(c_all, mod_w, mod_b):
    depth, d, n = mod_w.shape
    r = c_all.shape[0]
    tn = 1024
    return pl.pallas_call(
        _mod_kernel,
        grid=(depth, n // tn),
        in_specs=[
            pl.BlockSpec((r, d), lambda l, j: (0, 0)),
            pl.BlockSpec((None, d, tn), lambda l, j: (l, 0, j)),
            pl.BlockSpec((None, 1, tn), lambda l, j: (l, 0, j)),
        ],
        out_specs=pl.BlockSpec((None, r, tn), lambda l, j: (l, 0, j)),
        out_shape=jax.ShapeDtypeStruct((depth, r, n), F32),
        compiler_params=_cparams("parallel", "parallel"),
        name="modulation",
    )(c_all, mod_w, mod_b.reshape(depth, 1, n))


def _mod_spec(nb, d, rows_per_batch_block):
    return pl.BlockSpec((9, None, nb, d), lambda i: (0, i // rows_per_batch_block, 0, 0))


def _ffn_kernel(x_ref, mod_ref, xs_ref, mods_ref, npre_ref, npost_ref, win_ref, wout_ref, o_ref, os_ref,
                h_ref, a_ref, *, k, dff, ck):
    def rows(x, mod):
        n = x.shape[0]
        sh, sc, gt = mod[3 * k], mod[3 * k + 1], mod[3 * k + 2]
        h = _rms(x, npre_ref[k:k + 1, :]) * (1.0 + sc) + sh
        h_ref[0:n, :] = h.astype(BF16)
        for c in range(dff // ck):
            hb = h_ref[0:n, :]
            g = _dot(hb, win_ref[:, c * ck:(c + 1) * ck])
            u = _dot(hb, win_ref[:, dff + c * ck:dff + (c + 1) * ck])
            a_ref[0:n, c * ck:(c + 1) * ck] = (jax.nn.silu(g) * u).astype(BF16)
        y = _dot(a_ref[0:n, :], wout_ref[...])
        return x + 0.5 * gt * _rms(y, npost_ref[k:k + 1, :])

    @pl.when(pl.program_id(0) == 0)
    def _():
        os_ref[...] = rows(xs_ref[...], mods_ref)

    o_ref[...] = rows(x_ref[...], mod_ref)


def _ffn(x, mod, xs, mods, npre, npost, w_in, w_out, *, k, bm, blocks_per_batch):
    m, d = x.shape
    ms = xs.shape[0]
    dff = w_out.shape[0]
    kern = functools.partial(_ffn_kernel, k=k, dff=dff, ck=256)
    return pl.pallas_call(
        kern,
        grid=(m // bm,),
        in_specs=[
            pl.BlockSpec((bm, d), lambda i: (i, 0)),
            _mod_spec(1, d, blocks_per_batch),
            _const_spec((ms, d)),
            pl.BlockSpec((9, None, ms, d), lambda i: (0, 0, 0, 0), pipeline_mode=pl.Buffered(1)),
            _const_spec(npre.shape),
            _const_spec(npost.shape),
            _const_spec(w_in.shape),
            _const_spec(w_out.shape),
        ],
        out_specs=[pl.BlockSpec((bm, d), lambda i: (i, 0)), pl.BlockSpec((ms, d), lambda i: (0, 0))],
        out_shape=[jax.ShapeDtypeStruct((m, d), F32), jax.ShapeDtypeStruct((ms, d), F32)],
        scratch_shapes=[pltpu.VMEM((max(bm, ms), d), BF16), pltpu.VMEM((max(bm, ms), dff), BF16)],
        compiler_params=_cparams("arbitrary"),
        name=f"ffn{k}",
    )(x, mod, xs, mods, npre, npost, w_in, w_out)


def _proj_prompt_kernel(x_ref, mod_ref, npre_ref, wrow_ref, wt_ref, *refs, d_ssm, d_att, layer):
    earlier = refs[:3] if layer else ()
    u_ref, kh_ref, kidxb_ref, kt_all, vt_all, kit_all, v3_ref, qt_ref, qit_ref, wt_out_ref = refs[len(earlier):]
    for src, dst in zip(earlier, (kt_all, vt_all, kit_all)):
        dst[0:layer] = src[...]
    kt_ref, vt_ref, kit_ref = kt_all.at[layer], vt_all.at[layer], kit_all.at[layer]
    x = x_ref[...]
    sh, sc = mod_ref[3], mod_ref[4]
    h = (_rms(x, npre_ref[1:2, :]) * (1.0 + sc) + sh).astype(BF16)
    o_k = d_ssm + d_att
    o_ki = d_ssm + 3 * d_att + IDX_HEADS * IDX_DIM
    u_ref[...] = _dot(h, wrow_ref[:, 0:d_ssm])
    k = _dot(h, wrow_ref[:, o_k:o_k + d_att])
    kidxb_ref[...] = _dot(h, wrow_ref[:, o_ki:o_ki + IDX_DIM]).astype(BF16)
    for hh in range(N_HEADS):
        kh_ref[hh] = k[:, hh * HEAD_DIM:(hh + 1) * HEAD_DIM].astype(BF16)
    pt = _dot_nt(wt_ref[...], h)
    o = 0
    qt_ref[...] = (pt[o:o + d_att] * ATT_SCALE).astype(BF16)
    o += d_att
    kt_ref[...] = pt[o:o + d_att]
    o += d_att
    vt = pt[o:o + d_att]
    vt_ref[...] = vt
    vtb = vt.astype(BF16)
    for jj in range(v3_ref.shape[0]):
        v3_ref[jj] = vtb[:, jj * QB:(jj + 1) * QB]
    o += d_att
    qit_ref[...] = pt[o:o + IDX_HEADS * IDX_DIM].astype(BF16)
    o += IDX_HEADS * IDX_DIM
    kit_ref[...] = pt[o:o + IDX_DIM]
    o += IDX_DIM
    wt_out_ref[...] = pt[o:o + IDX_HEADS] * IDX_SCALE


def _proj_prompt(x, mod, npre, w_row, w_t, earlier, *, batch, bm, blocks_per_batch, d_ssm, d_att):
    m, d = x.shape
    t_len = m // batch
    nbt = blocks_per_batch
    layer = earlier[0].shape[0] if earlier else 0
    row = lambda w: pl.BlockSpec((bm, w), lambda i: (i, 0))
    col = lambda r: pl.BlockSpec((r, bm), lambda i: (0, i))
    lcol = lambda n, r: pl.BlockSpec((n, None, r, bm), lambda i: (0, i // nbt, 0, i % nbt))
    bcol = lambda r: lcol(layer + 1, r)
    kern = functools.partial(_proj_prompt_kernel, d_ssm=d_ssm, d_att=d_att, layer=layer)
    return pl.pallas_call(
        kern,
        grid=(m // bm,),
        in_specs=[
            pl.BlockSpec((bm, d), lambda i: (i, 0)),
            _mod_spec(1, d, blocks_per_batch),
            _const_spec(npre.shape),
            _const_spec(w_row.shape),
            _const_spec(w_t.shape),
        ] + [lcol(layer, a.shape[2]) for a in earlier],
        out_specs=[
            row(d_ssm),
            pl.BlockSpec((N_HEADS, bm, HEAD_DIM), lambda i: (0, i, 0)),
            row(IDX_DIM),
            bcol(d_att), bcol(d_att), bcol(IDX_DIM),
            pl.BlockSpec((bm // QB, d_att, QB), lambda i: (i, 0, 0)),
            col(d_att), col(d_att), col(IDX_HEADS),
        ],
        out_shape=[
            jax.ShapeDtypeStruct((m, d_ssm), F32),
            jax.ShapeDtypeStruct((N_HEADS, m, HEAD_DIM), BF16),
            jax.ShapeDtypeStruct((m, IDX_DIM), BF16),
            jax.ShapeDtypeStruct((layer + 1, batch, d_att, t_len), F32),
            jax.ShapeDtypeStruct((layer + 1, batch, d_att, t_len), F32),
            jax.ShapeDtypeStruct((layer + 1, batch, IDX_DIM, t_len), F32),
            jax.ShapeDtypeStruct((m // QB, d_att, QB), BF16),
            jax.ShapeDtypeStruct((d_att, m), BF16),
            jax.ShapeDtypeStruct((d_att, m), BF16),
            jax.ShapeDtypeStruct((IDX_HEADS, m), F32),
        ],
        compiler_params=_cparams("parallel"),
        name="proj_prompt",
    )(x, mod, npre, w_row, w_t, *earlier)


def _proj_sample_kernel(x_ref, mod_ref, npre_ref, w_ref, o_ref):
    x = x_ref[...]
    sh, sc = mod_ref[3], mod_ref[4]
    h = (_rms(x, npre_ref[1:2, :]) * (1.0 + sc) + sh).astype(BF16)
    o_ref[...] = _dot(h, w_ref[...])


def _proj_sample(x, mod, npre, w):
    m, d = x.shape
    n = w.shape[1]
    return pl.pallas_call(
        _proj_sample_kernel,
        grid=(1,),
        in_specs=[
            pl.BlockSpec((m, d), lambda i: (0, 0)),
            _mod_spec(m, d, 1),
            _const_spec(npre.shape),
            _const_spec(w.shape),
        ],
        out_specs=pl.BlockSpec((m, n), lambda i: (0, 0)),
        out_shape=jax.ShapeDtypeStruct((m, n), F32),
        compiler_params=_cparams("arbitrary"),
        name="proj_sample",
    )(x, mod, npre, w)


def _ssm_prep_kernel(ldt_ref, ar_ref, ai_ref, btre_ref, btim_ref, cre_ref, cim_ref,
                     kk_ref, wre_ref, wim_ref, pre_ref, pim_ref, al_ref, ab_ref, bbre_ref, bbim_ref):
    dt = jnp.exp(ldt_ref[...])
    ar, ai = ar_ref[...], ai_ref[...]
    power = []
    for k in range(CHUNK + 1):
        mag = jnp.exp(dt * ar * float(k))
        ph = dt * ai * float(k)
        power.append((mag * jnp.cos(ph), mag * jnp.sin(ph)))

    abr, abi = power[1]
    den = ar * ar + ai * ai
    z_re = ((abr - 1.0) * ar + abi * ai) / den
    z_im = (abi * ar - (abr - 1.0) * ai) / den
    b_re, b_im = btre_ref[...], btim_ref[...]
    bb_re = z_re * b_re - z_im * b_im
    bb_im = z_re * b_im + z_im * b_re
    bbre_ref[...] = bb_re
    bbim_ref[...] = bb_im
    ab_ref[0] = abr
    ab_ref[1] = abi
    al_ref[0], al_ref[1] = power[CHUNK]
    c_re, c_im = cre_ref[...], cim_ref[...]
    bnt = functools.partial(lax.dot_general, dimension_numbers=(((2,), (2,)), ((0,), (0,))),
                            precision=lax.Precision.HIGHEST, preferred_element_type=F32)
    for k in range(CHUNK):
        pr, pi = power[CHUNK - 1 - k]
        wre_ref[k] = pr * bb_re - pi * bb_im
        wim_ref[k] = pr * bb_im + pi * bb_re
        pr, pi = power[k]
        kk_ref[k] = bnt(bb_re, c_re * pr - c_im * pi) - bnt(bb_im, c_re * pi + c_im * pr)
        pr, pi = power[k + 1]
        pre_ref[k] = c_re * pr - c_im * pi
        pim_ref[k] = -(c_re * pi + c_im * pr)


def _ssm_prep(log_dt, a_re, a_im, b_re, b_im, c_re, c_im):
    g, n = a_re.shape
    j = SSM_GROUP
    shp = lambda *s: jax.ShapeDtypeStruct(s, F32)
    swap = lambda a: a.transpose(0, 2, 1)
    return pl.pallas_call(
        _ssm_prep_kernel,
        out_shape=[shp(CHUNK, g, j, j), shp(CHUNK, g, j, n), shp(CHUNK, g, j, n),
                   shp(CHUNK, g, j, n), shp(CHUNK, g, j, n), shp(2, g, 1, n), shp(2, g, 1, n),
                   shp(g, j, n), shp(g, j, n)],
        compiler_params=pltpu.CompilerParams(vmem_limit_bytes=VMEM_LIMIT_BYTES),
        name="ssm_prep",
    )(log_dt.reshape(g, 1, 1), a_re.reshape(g, 1, n), a_im.reshape(g, 1, n), swap(b_re), swap(b_im), c_re, c_im)


def _block_diag(x, groups):
    rows, c = x.shape[-2:]
    keep = (np.arange(rows)[:, None] // (rows // groups)) == (np.arange(groups * c)[None, :] // c)
    return jnp.where(keep, jnp.tile(x, (1,) * (x.ndim - 1) + (groups,)), 0.0)


def _ssm_assemble_kernel(kk_ref, wre_ref, wim_ref, pre_ref, pim_ref, m_ref, w_ref, p_ref):
    j, n = SSM_GROUP, SSM_N

    def block_diag(x, r, c):
        rows, cols = OCTET * r, OCTET * c
        rep = ((lax.broadcasted_iota(I32, (c, cols), 1) & (c - 1)) == lax.broadcasted_iota(I32, (c, cols), 0))
        keep = ((lax.broadcasted_iota(I32, (rows, cols), 0) >> int(math.log2(r)))
                == (lax.broadcasted_iota(I32, (rows, cols), 1) >> int(math.log2(c))))
        return jnp.where(keep, _dot(x.astype(BF16), rep.astype(BF16)), 0.0)

    m_ref[...] = jnp.zeros(m_ref.shape, BF16)
    blk = lambda i, w: slice(i * w, (i + 1) * w)
    for k in range(CHUNK):
        tile = block_diag(kk_ref[k].reshape(OCTET * j, j), j, j).astype(BF16)
        for tau in range(CHUNK - k):
            m_ref[blk(tau, LANES), blk(tau + k, LANES)] = tile
        for part, (w_src, p_src) in enumerate(((wre_ref, pre_ref), (wim_ref, pim_ref))):
            w_ref[blk(k, LANES), blk(part, OCTET * n)] = block_diag(w_src[k].reshape(OCTET * j, n), j, n).astype(BF16)
            p_t = block_diag(p_src[k].reshape(OCTET * j, n), j, n)
            p_ref[blk(part, OCTET * n), blk(k, LANES)] = jnp.transpose(p_t).astype(BF16)


def _ssm_assemble(kk, w_re, w_im, p_re, p_im):
    chunk, g, j, n = w_re.shape
    no, kw, sw = g // OCTET, chunk * LANES, 2 * OCTET * n
    grp = lambda a, b: pl.BlockSpec((chunk, OCTET, a, b), lambda o: (0, o, 0, 0))
    out = lambda a, b: pl.BlockSpec((None, a, b), lambda o: (o, 0, 0))
    return pl.pallas_call(
        _ssm_assemble_kernel,
        grid=(no,),
        in_specs=[grp(j, j), grp(j, n), grp(j, n), grp(j, n), grp(j, n)],
        out_specs=[out(kw, kw), out(kw, sw), out(sw, kw)],
        out_shape=[jax.ShapeDtypeStruct((no, kw, kw), BF16), jax.ShapeDtypeStruct((no, kw, sw), BF16),
                   jax.ShapeDtypeStruct((no, sw, kw), BF16)],
        compiler_params=_cparams("parallel"),
        name="ssm_assemble",
    )(kk, w_re, w_im, p_re, p_im)


def _ssm_operators(prep, c_re, c_im):
    kk, w_re, w_im, p_re, p_im, al, ab, bb_re, bb_im = prep
    g, j, n = bb_re.shape
    no = g // OCTET
    m_op, w_op, p_op = _ssm_assemble(kk, w_re, w_im, p_re, p_im)
    a_chunk = al.reshape(2, no, OCTET * n).transpose(1, 0, 2)
    b_step = _block_diag(jnp.stack([bb_re, bb_im], 0).reshape(2, g * j, n), g)
    b_step = b_step.transpose(1, 0, 2).reshape(g * j, 2 * g * n)
    c_t = jnp.stack([c_re, -c_im], 0).transpose(0, 1, 3, 2).reshape(2, g * n, j)
    c_step = _block_diag(c_t, g).reshape(2 * g * n, g * j).astype(BF16)
    return dict(m=m_op, w=w_op, p=p_op, a_chunk=a_chunk, b_step=b_step, c_step=c_step, abar=ab.reshape(2, g * n))


def _ssm_prompt_kernel(u_ref, m_ref, w_ref, p_ref, al_ref, d_ref, y_ref, sfin_ref,
                       uo_ref, v_ref, sc_ref, *, r, nb):
    half = sc_ref.shape[1] // 2
    rows_all = nb * r
    for tau in range(CHUNK):
        uo_ref[:, tau * LANES:(tau + 1) * LANES] = u_ref[pl.ds(tau, rows_all, stride=CHUNK), :].astype(BF16)
    uo = uo_ref[...]
    v_ref[...] = _dot(uo, w_ref[...])
    a_r, a_i = al_ref[0:1, :], al_ref[1:2, :]

    def step(c, carry):
        out = []
        for s, (s_r, s_i) in enumerate(carry):
            row = pl.ds(s * r + c, 1)
            sc_ref[row, 0:half] = s_r
            sc_ref[row, half:2 * half] = s_i
            v = v_ref[row, :]
            out.append((a_r * s_r - a_i * s_i + v[:, 0:half], a_r * s_i + a_i * s_r + v[:, half:2 * half]))
        return tuple(out)

    zero = jnp.zeros((1, half), F32)
    final = lax.fori_loop(0, r, step, tuple((zero, zero) for _ in range(nb)))
    for s, (s_r, s_i) in enumerate(final):
        sfin_ref[s, 0:1, :] = s_r
        sfin_ref[s, 1:2, :] = s_i
    y = _dot(uo, m_ref[...]) + _dot(sc_ref[...].astype(BF16), p_ref[...])
    d = d_ref[...]
    for t in range(CHUNK):
        rows = pl.ds(t, rows_all, stride=CHUNK)
        y_ref[rows, :] = y[:, t * LANES:(t + 1) * LANES] + d * u_ref[rows, :]


def _ssm_prompt(u, ops, d_skip, *, batch, t_len):
    m, d_ssm = u.shape
    no = d_ssm // LANES
    r = t_len // CHUNK
    kw = CHUNK * LANES
    sw = ops["w"].shape[2]
    nb = math.gcd(batch, 4)
    op_spec = lambda a, b: pl.BlockSpec((None, a, b), lambda o, bb: (o, 0, 0))
    y, sfin = pl.pallas_call(
        functools.partial(_ssm_prompt_kernel, r=r, nb=nb),
        grid=(no, batch // nb),
        in_specs=[
            pl.BlockSpec((nb * t_len, LANES), lambda o, bb: (bb, o)),
            op_spec(kw, kw), op_spec(kw, sw), op_spec(sw, kw),
            pl.BlockSpec((None, 2, sw // 2), lambda o, bb: (o, 0, 0)),
            pl.BlockSpec((1, LANES), lambda o, bb: (0, o)),
        ],
        out_specs=[
            pl.BlockSpec((nb * t_len, LANES), lambda o, bb: (bb, o)),
            pl.BlockSpec((nb, None, 2, sw // 2), lambda o, bb: (bb, o, 0, 0)),
        ],
        out_shape=[jax.ShapeDtypeStruct((m, d_ssm), F32),
                   jax.ShapeDtypeStruct((batch, no, 2, sw // 2), F32)],
        scratch_shapes=[pltpu.VMEM((nb * r, kw), BF16), pltpu.VMEM((nb * r, sw), F32), pltpu.VMEM((nb * r, sw), F32)],
        compiler_params=_cparams("parallel", "parallel"),
        name="ssm_prompt",
    )(u, ops["m"], ops["w"], ops["p"], ops["a_chunk"], d_skip.reshape(1, d_ssm))
    s = sfin.reshape(batch, no, 2, OCTET, SSM_N).transpose(2, 0, 1, 3, 4).reshape(2, batch, no * OCTET, SSM_N)
    return y, s[0], s[1]


def _ssm_sample_kernel(u_ref, h_ref, ab_ref, bstep_ref, cstep_ref, d_ref, y_ref, s_ref):
    u = u_ref[...]
    half = h_ref.shape[2]
    bu = jnp.dot(u, bstep_ref[...], precision=lax.Precision.HIGHEST, preferred_element_type=F32)
    a_r, a_i = ab_ref[0:1, :], ab_ref[1:2, :]
    h_r, h_i = h_ref[0], h_ref[1]
    s_r = a_r * h_r - a_i * h_i + bu[:, 0:half]
    s_i = a_r * h_i + a_i * h_r + bu[:, half:2 * half]
    s_ref[0] = s_r
    s_ref[1] = s_i
    s = jnp.concatenate([s_r, s_i], axis=1).astype(BF16)
    y_ref[...] = _dot(s, cstep_ref[...]) + d_ref[...] * u


def _ssm_sample(u, h_re, h_im, ops, c_step, d_skip):
    bsz, d_ssm = u.shape
    gn = h_re.shape[1] * h_re.shape[2]
    h = jnp.stack([h_re.reshape(bsz, gn), h_im.reshape(bsz, gn)], 0)
    y, s = pl.pallas_call(
        _ssm_sample_kernel,
        out_shape=[jax.ShapeDtypeStruct((bsz, d_ssm), F32), jax.ShapeDtypeStruct((2, bsz, gn), F32)],
        compiler_params=pltpu.CompilerParams(vmem_limit_bytes=VMEM_LIMIT_BYTES),
        name="ssm_sample",
    )(u, h, ops["abar"], ops["b_step"], c_step, d_skip.reshape(1, d_ssm))
    return y, s[0].reshape(h_re.shape), s[1].reshape(h_im.shape)


def _f2key(x):
    b = lax.bitcast_convert_type(x, I32)
    return b ^ ((b >> 31) & 0x7FFFFFFF)


def _key2f(k):
    return lax.bitcast_convert_type(k ^ ((k >> 31) & 0x7FFFFFFF), F32)


def _search_init(mn, mx, n_valid, topk):
    z = jnp.zeros_like(n_valid)
    half = topk + 0.5
    return (_f2key(mn), _f2key(mx) + 1, n_valid.astype(F32) - half, jnp.full(mn.shape, half, F32),
            jnp.full(mn.shape, NEG_BIG, F32), jnp.where(n_valid > topk, 0, 1).astype(I32), z, z)


def _search_probe(state, it, topk):
    lo, hi, f_lo, f_hi = state[:4]
    lo_f, hi_f = _key2f(lo), _key2f(hi)
    frac = jnp.where(it % 4 == 3, 0.5, f_lo / (f_lo + f_hi))
    cand = _f2key(lo_f + (hi_f - lo_f) * frac)
    mid_k = (lo >> 1) + (hi >> 1) + (lo & hi & 1)
    inside = (cand > lo) & (cand < hi) & (it < FLOAT_MID_ITERS)
    probe = jnp.where(inside, cand, mid_k)
    zero_k = jnp.where(it == 0, 0, MIN_NORMAL_KEY)
    return jnp.where((it < 2) & (zero_k > lo) & (zero_k < hi), zero_k, probe)


def _search_update(state, mid, cnt, topk):
    lo, hi, f_lo, f_hi, thr, done, tie, last = state
    hit = cnt == topk
    up, dn = cnt > topk, cnt < topk
    lo_n, hi_n = jnp.where(up, mid, lo), jnp.where(dn, mid, hi)
    adj = ((hi_n == lo_n + 1) | ((lo_n == 0) & (hi_n == MIN_NORMAL_KEY))) & jnp.logical_not(hit)
    fin = hit | adj
    thr_n = jnp.where(hit, _key2f(mid), _key2f(lo_n))
    act = done == 0
    lo = jnp.where(act, lo_n, lo)
    hi = jnp.where(act, hi_n, hi)
    miss = cnt.astype(F32) - (topk + 0.5)
    f_lo = jnp.where(act, jnp.where(up, miss, jnp.where(dn & (last == -1), 0.5 * f_lo, f_lo)), f_lo)
    f_hi = jnp.where(act, jnp.where(dn, -miss, jnp.where(up & (last == 1), 0.5 * f_hi, f_hi)), f_hi)
    last = jnp.where(act, jnp.where(up, 1, jnp.where(dn, -1, last)), last)
    thr = jnp.where(act & fin, thr_n, thr)
    tie = jnp.where(act & adj, 1, tie)
    done = jnp.where(act & fin, 1, done)
    return lo, hi, f_lo, f_hi, thr, done, tie, last


def _search(count_ge, init, topk):
    def pending(state, it):
        return jnp.logical_and(jnp.min(state[5].astype(F32)) == 0.0, it < 96)

    def probe(u, carry):
        it, state = carry
        mid = _search_probe(state, it, topk)
        return it + 1, _search_update(state, mid, count_ge(_key2f(mid)), topk)

    def body(carry):
        it, state = lax.fori_loop(0, SEARCH_PROBES_PER_TEST, probe, carry[1:])
        return pending(state, it), it, state

    _, _, state = lax.while_loop(lambda c: c[0], body, (pending(init, 0), jnp.int32(0), init))
    return state[4], state[6]


def _attn_prompt_kernel(qt_ref, qit_ref, wt_ref, kh_ref, kidx_ref, v3_ref, o_ref, sc_ref, lg_ref, acc_ref, *, topk):
    i = pl.program_id(1)
    nk = i + 1
    qpos = i * QB + lax.broadcasted_iota(I32, (QB, QB), 1)
    krow = lax.broadcasted_iota(I32, (QB, QB), 0)
    fold = lambda x: x.reshape(QB // SUBLANES, SUBLANES, QB)

    tile_rows = lambda j: pl.ds(pl.multiple_of(j * QB, QB), QB)
    rep = lambda x: jnp.broadcast_to(x, (SUBLANES, QB))

    def score_tile(j, carry):
        mn, mx = carry
        kx = kidx_ref[tile_rows(j), :]
        acc = jnp.zeros((QB, QB), F32)
        for h in range(IDX_HEADS):
            s = _dot(kx, qit_ref[h * IDX_DIM:(h + 1) * IDX_DIM, :])
            acc = acc + jnp.maximum(s, 0.0) * wt_ref[h:h + 1, :]
        valid = (j * QB + krow) <= qpos
        sc = jnp.where(valid, acc, -jnp.inf)
        sc_ref[j] = sc
        mx = jnp.maximum(mx, jnp.max(fold(sc), axis=0))
        mn = jnp.minimum(mn, jnp.min(fold(jnp.where(valid, acc, jnp.inf)), axis=0))
        return mn, mx

    mn, mx = lax.fori_loop(0, nk, score_tile,
                           (jnp.full((SUBLANES, QB), jnp.inf, F32), jnp.full((SUBLANES, QB), -jnp.inf, F32)))
    mn = rep(jnp.min(mn, axis=0, keepdims=True))
    mx = rep(jnp.max(mx, axis=0, keepdims=True))

    def count_ge(t):
        t1 = t[0:1, :]

        def body(j, c):
            return c + jnp.sum(fold((sc_ref[j] >= t1).astype(I32)), axis=0)

        c = lax.fori_loop(0, nk, body, jnp.zeros((SUBLANES, QB), I32))
        return rep(jnp.sum(c, axis=0, keepdims=True))

    n_valid = qpos[0:SUBLANES, :] + 1
    thr, tie = _search(count_ge, _search_init(mn, mx, n_valid, topk), topk)

    @pl.when(jnp.max(tie) > 0)
    def _():
        tri = (lax.broadcasted_iota(I32, (QB, QB), 0) >= lax.broadcasted_iota(I32, (QB, QB), 1)).astype(BF16)
        thr1, tie1 = thr[0:1, :], tie[0:1, :] > 0

        def gt_tile(j, c):
            return c + jnp.sum((sc_ref[j] > thr1).astype(I32), axis=0, keepdims=True)

        quota = topk - lax.fori_loop(0, nk, gt_tile, jnp.zeros((1, QB), I32))

        def tie_tile(j, before):
            x = sc_ref[j]
            t = (x == thr1) & tie1
            rank = before + _dot(tri, t.astype(BF16)).astype(I32)
            sc_ref[j] = jnp.where(t & (rank > quota), -jnp.inf, x)
            return before + jnp.sum(t.astype(I32), axis=0, keepdims=True)

        lax.fori_loop(0, nk, tie_tile, jnp.zeros((1, QB), I32))

    thr1 = thr[0:1, :]
    heads = range(N_HEADS)
    hslice = lambda h: slice(h * HEAD_DIM, (h + 1) * HEAD_DIM)

    colmax = lambda x: jnp.max(jnp.max(fold(x), axis=0), axis=0, keepdims=True)
    colsum = lambda x: jnp.sum(jnp.sum(fold(x), axis=0), axis=0, keepdims=True)
    rows_of = lambda rows: jnp.concatenate(rows, axis=0)

    def logits(j, slot, m):
        bias = jnp.where(sc_ref[j] >= thr1, 0.0, -jnp.inf)
        rows = tile_rows(j)
        out = []
        for h in heads:
            lg = _dot(kh_ref[h, rows, :], qt_ref[hslice(h), :]) + bias
            lg_ref[h, slot] = lg
            out.append(jnp.maximum(m[h:h + 1, :], colmax(lg)))
        return rows_of(out)

    def values(j, slot, m_old, m_new, l):
        m_safe = jnp.where(m_new == -jnp.inf, 0.0, m_new)
        alpha = jnp.exp(m_old - m_safe)
        out = []
        for h in heads:
            p = jnp.exp(lg_ref[h, slot] - m_safe[h:h + 1, :])
            acc_ref[h] = acc_ref[h] * alpha[h:h + 1, :] + _dot(v3_ref[j, hslice(h), :], p.astype(BF16))
            out.append(colsum(p))
        return l * alpha + rows_of(out)

    acc_ref[...] = jnp.zeros(acc_ref.shape, F32)
    neg = jnp.full((N_HEADS, QB), -jnp.inf, F32)

    def step(j, slot, carry):
        m_before, m_upto, l = carry
        m_next = logits(j + 1, 1 - slot, m_upto)
        return m_upto, m_next, values(j, slot, m_before, m_upto, l)

    def two_steps(jj, carry):
        return step(2 * jj + 1, 1, step(2 * jj, 0, carry))

    carry = (neg, logits(0, 0, neg), jnp.zeros((N_HEADS, QB), F32))
    carry = lax.fori_loop(0, (nk - 1) // 2, two_steps, carry)
    odd = (nk - 1) % 2 == 1
    carry = lax.cond(odd, lambda c: step(nk - 2, 0, c), lambda c: c, carry)
    l = lax.cond(odd, lambda c: values(nk - 1, 1, *c), lambda c: values(nk - 1, 0, *c), carry)
    outs = [acc_ref[h] / l[h:h + 1, :] for h in heads]
    o_ref[...] = jnp.transpose(jnp.concatenate(outs, axis=0)).astype(BF16)


def _attn_prompt(qt, qit, wt, kh, kidxb, v3, *, batch, t_len, topk):
    d_att, m = qt.shape
    nq = t_len // QB
    return pl.pallas_call(
        functools.partial(_attn_prompt_kernel, topk=topk),
        grid=(batch, nq),
        in_specs=[
            pl.BlockSpec((d_att, QB), lambda b, i: (0, b * nq + i)),
            pl.BlockSpec((d_att, QB), lambda b, i: (0, b * nq + i)),
            pl.BlockSpec((IDX_HEADS, QB), lambda b, i: (0, b * nq + i)),
            pl.BlockSpec((N_HEADS, t_len, HEAD_DIM), lambda b, i: (0, b, 0)),
            pl.BlockSpec((t_len, IDX_DIM), lambda b, i: (b, 0)),
            pl.BlockSpec((nq, d_att, QB), lambda b, i: (b, 0, 0)),
        ],
        out_specs=pl.BlockSpec((QB, d_att), lambda b, i: (b * nq + i, 0)),
        out_shape=jax.ShapeDtypeStruct((m, d_att), BF16),
        scratch_shapes=[pltpu.VMEM((nq, QB, QB), F32), pltpu.VMEM((N_HEADS, 2, QB, QB), F32),
                        pltpu.VMEM((N_HEADS, HEAD_DIM, QB), F32)],
        compiler_params=_cparams("parallel", "arbitrary"),
        name="attn_prompt",
    )(qt, qit, wt, kh, kidxb, v3)


def _idx_sample_kernel(pt_ref, qi_ref, w_ref, kin_ref, ci_ref, o_ref, ibuf, sem, *, layer, n_pages):
    b = pl.program_id(0)
    slot = b % 2

    def page_copy(buf_slot, p, page):
        lanes = pl.ds(pl.multiple_of(p * PAGE_SIZE, PAGE_SIZE), PAGE_SIZE)
        return pltpu.make_async_copy(ci_ref.at[layer, page], ibuf.at[buf_slot, :, lanes], sem.at[buf_slot])

    def start_all(sample, buf_slot):
        def start(p, _):
            page_copy(buf_slot, p, pt_ref[sample, p]).start()
            return 0
        lax.fori_loop(0, n_pages, start, 0, unroll=SUBLANES)

    @pl.when(b == 0)
    def _():
        start_all(b, slot)

    @pl.when(b + 1 < pl.num_programs(0))
    def _():
        start_all(b + 1, 1 - slot)

    def wait(p, _):
        page_copy(slot, p, 0).wait()
        return 0

    lax.fori_loop(0, n_pages, wait, 0, unroll=SUBLANES)

    qi = qi_ref[...].astype(BF16)
    w = w_ref[...] * IDX_SCALE
    n_past = n_pages * PAGE_SIZE
    step = SUBLANES * PAGE_SIZE
    for c in range(n_past // step):
        cols = slice(c * step, (c + 1) * step)
        s = _dot(qi, ibuf[slot, :, cols].astype(BF16))
        o_ref[:, cols] = jnp.sum(jnp.maximum(s, 0.0) * w, axis=0, keepdims=True)
    s_new = jnp.sum(qi.astype(F32) * kin_ref[...].astype(BF16).astype(F32), axis=1, keepdims=True)
    s_new = jnp.sum(jnp.maximum(s_new, 0.0) * w, axis=0, keepdims=True)
    lane = lax.broadcasted_iota(I32, (1, PAGE_SIZE), 1)
    o_ref[:, n_past:n_past + PAGE_SIZE] = jnp.where(lane == 0, s_new, -jnp.inf)


def _idx_sample(page_table, q_idx, w_idx, k_idx_new, cache_idx_t, *, layer):
    bsz, n_pages = page_table.shape
    width = (n_pages + 1) * PAGE_SIZE
    blk = lambda r, c: pl.BlockSpec((None, r, c), lambda b, pt: (b, 0, 0))
    return pl.pallas_call(
        functools.partial(_idx_sample_kernel, layer=layer, n_pages=n_pages),
        grid_spec=pltpu.PrefetchScalarGridSpec(
            num_scalar_prefetch=1,
            grid=(bsz,),
            in_specs=[blk(IDX_HEADS, IDX_DIM), blk(IDX_HEADS, 1), blk(1, IDX_DIM), pl.BlockSpec(memory_space=pl.ANY)],
            out_specs=blk(1, width),
            scratch_shapes=[pltpu.VMEM((2, IDX_DIM, n_pages * PAGE_SIZE), F32), pltpu.SemaphoreType.DMA((2,))],
        ),
        out_shape=jax.ShapeDtypeStruct((bsz, 1, width), F32),
        compiler_params=_cparams("arbitrary"),
        name="idx_sample",
    )(page_table, q_idx, w_idx, k_idx_new, cache_idx_t)


def _threshold_sample_kernel(sc_ref, thr_ref, tie_ref, *, n_keys, topk):
    sc = sc_ref[...]
    mx = jnp.max(sc, axis=1, keepdims=True)
    mn = jnp.min(jnp.where(sc == -jnp.inf, jnp.inf, sc), axis=1, keepdims=True)

    def count_ge(t):
        return jnp.sum((sc_ref[...] >= t).astype(I32), axis=1, keepdims=True)

    n = jnp.full(mn.shape, n_keys, I32)
    thr, tie = _search(count_ge, _search_init(mn, mx, n, topk), topk)
    thr_ref[...] = jnp.broadcast_to(thr, thr_ref.shape)
    tie_ref[...] = jnp.broadcast_to(tie, tie_ref.shape)


def _threshold_sample(scores, *, n_keys, topk):
    bsz = scores.shape[0]
    return pl.pallas_call(
        functools.partial(_threshold_sample_kernel, n_keys=n_keys, topk=topk),
        out_shape=[jax.ShapeDtypeStruct((bsz, LANES), F32), jax.ShapeDtypeStruct((bsz, LANES), I32)],
        compiler_params=pltpu.CompilerParams(vmem_limit_bytes=VMEM_LIMIT_BYTES),
        name="threshold_sample",
    )(scores)


def _attend_sample_kernel(pt_ref, sc_ref, snew_ref, thr_ref, tie_ref, q_ref, kn_ref, vn_ref, ck_ref, cv_ref, o_ref,
                          kbuf, vbuf, qb, lg_ref, sem_k, sem_v, *, layer, n_pages, topk):
    b = pl.program_id(0)

    def copies(src_ref, buf, sem, p, page):
        return pltpu.make_async_copy(src_ref.at[layer, page], buf.at[p], sem)

    streams = ((ck_ref, kbuf, sem_k), (cv_ref, vbuf, sem_v))

    def start_all(src_ref, buf, sem, sample):
        def start(p, _):
            copies(src_ref, buf, sem, p, pt_ref[sample, p]).start()
            return 0
        lax.fori_loop(0, n_pages, start, 0)

    @pl.when(b == 0)
    def _():
        start_all(*streams[0], b)

    start_all(*streams[1], b)

    def wait_all(src_ref, buf, sem):
        def wait(p, _):
            copies(src_ref, buf, sem, p, 0).wait()
            return 0
        lax.fori_loop(0, n_pages, wait, 0)

    for h in range(N_HEADS):
        qb[h] = jnp.broadcast_to(q_ref[h] * ATT_SCALE, (HEAD_DIM, PAGE_SIZE))
    dsum = lambda x: jnp.sum(x, axis=0, keepdims=True)
    sc = sc_ref[...]
    s_new, thr, tie = snew_ref[...], thr_ref[...], tie_ref[...]

    def total(x, op=jnp.sum):
        return op(op(x, axis=1, keepdims=True), axis=0, keepdims=True)

    upper = (lax.broadcasted_iota(I32, (PAGE_SIZE, PAGE_SIZE), 0)
             <= lax.broadcasted_iota(I32, (PAGE_SIZE, PAGE_SIZE), 1)).astype(BF16)
    lower = (lax.broadcasted_iota(I32, (n_pages, n_pages), 1)
             < lax.broadcasted_iota(I32, (n_pages, n_pages), 0)).astype(BF16)

    def flat_rank(mask):
        mb = mask.astype(BF16)
        incl = _dot(mb, upper)
        before = jnp.sum(_dot(lower, mb), axis=1, keepdims=True)
        return (incl + before).astype(I32)

    gt = sc > thr
    tied = tie > 0
    is_tie = (sc == thr) & tied
    quota = jnp.where(tied, topk - total(gt.astype(I32)) - (s_new > thr).astype(I32), topk)
    sel = gt | ((sc == thr) & (flat_rank(is_tie) <= quota))
    n_tie_past = total(is_tie.astype(I32))
    new_sel = (s_new > thr) | ((s_new == thr) & (n_tie_past < quota))

    wait_all(*streams[0])
    for h in range(N_HEADS):
        q_h = qb[h]

        def k_page(p, _, h=h, q_h=q_h):
            lg_ref[h, pl.ds(p, 1), :] = dsum(kbuf[p, h] * q_h)
            return 0

        lax.fori_loop(0, n_pages, k_page, 0, unroll=4)

    @pl.when(b + 1 < pl.num_programs(0))
    def _():
        start_all(*streams[0], b + 1)

    red = lambda x, op: op(x, axis=(1, 2), keepdims=True)
    lg = jnp.where(sel[None], lg_ref[...], -jnp.inf)
    lg_new = jnp.sum(q_ref[...] * ATT_SCALE * kn_ref[...], axis=1, keepdims=True)
    lg_new = jnp.where(new_sel[None], lg_new, -jnp.inf)
    m = jnp.maximum(red(lg, jnp.max), lg_new)
    p = jnp.exp(lg - m)
    e_new = jnp.exp(lg_new - m)
    denom = red(p, jnp.sum) + e_new
    lg_ref[...] = p / denom
    p_new = e_new / denom

    wait_all(*streams[1])
    for h in range(N_HEADS):
        def v_page(p, acc, h=h):
            return acc + vbuf[p, h] * lg_ref[h, pl.ds(p, 1), :]

        acc = lax.fori_loop(0, n_pages, v_page, jnp.zeros((HEAD_DIM, PAGE_SIZE), F32), unroll=4)
        o_ref[h] = jnp.sum(acc, axis=1, keepdims=True) + p_new[h] * vn_ref[h]


def _attend_sample(page_table, scores, s_new, thr, tie, q, k_new, v_new, cache_k_t, cache_v_t, *, layer, topk):
    bsz, n_pages = page_table.shape
    one = pl.BlockSpec((None, 1, 1), lambda b, pt: (b, 0, 0))
    hcol = pl.BlockSpec((None, N_HEADS, HEAD_DIM, 1), lambda b, pt: (b, 0, 0, 0))
    any_spec = pl.BlockSpec(memory_space=pl.ANY)
    return pl.pallas_call(
        functools.partial(_attend_sample_kernel, layer=layer, n_pages=n_pages, topk=topk),
        grid_spec=pltpu.PrefetchScalarGridSpec(
            num_scalar_prefetch=1,
            grid=(bsz,),
            in_specs=[pl.BlockSpec((None, n_pages, PAGE_SIZE), lambda b, pt: (b, 0, 0)), one, one, one,
                      hcol, hcol, hcol, any_spec, any_spec],
            out_specs=hcol,
            scratch_shapes=[pltpu.VMEM((n_pages, N_HEADS, HEAD_DIM, PAGE_SIZE), F32),
                            pltpu.VMEM((n_pages, N_HEADS, HEAD_DIM, PAGE_SIZE), F32),
                            pltpu.VMEM((N_HEADS, HEAD_DIM, PAGE_SIZE), F32),
                            pltpu.VMEM((N_HEADS, n_pages, PAGE_SIZE), F32),
                            pltpu.SemaphoreType.DMA(()), pltpu.SemaphoreType.DMA(())],
        ),
        out_shape=jax.ShapeDtypeStruct((bsz, N_HEADS, HEAD_DIM, 1), F32),
        compiler_params=_cparams("arbitrary"),
        name="attend_sample",
    )(page_table, scores, s_new, thr, tie, q, k_new, v_new, cache_k_t, cache_v_t)


def _mix_kernel(x_ref, mod_ref, npre_ref, npost_ref, ys_ref, at_ref, gw_ref, gv_ref, wba_ref, wg_ref, wo_ref,
                o_ref):
    x = x_ref[...]
    d = x.shape[1]
    sh, sc, gt = mod_ref[3], mod_ref[4], mod_ref[5]
    h = (_rms(x, npre_ref[1:2, :]) * (1.0 + sc) + sh).astype(BF16)
    gates = _dot(h, wg_ref[...])
    ys = ys_ref[...].astype(BF16)
    y_a = _dot(ys, gw_ref[...]) * jax.nn.sigmoid(_dot(ys, gv_ref[...]))
    y_b = _dot(at_ref[...].astype(BF16), wba_ref[...])
    mixed = jax.nn.sigmoid(gates[:, 0:d]) * y_a + jax.nn.sigmoid(gates[:, d:2 * d]) * y_b
    y = _dot(mixed.astype(BF16), wo_ref[...])
    o_ref[...] = x + gt * _rms(y, npost_ref[1:2, :])


def _mix(x, mod, npre, npost, y_ssm, attn, glu_w, glu_v, wba, w_gates, w_out, *, bm, blocks_per_batch):
    m, d = x.shape
    nb = mod.shape[2]
    return pl.pallas_call(
        _mix_kernel,
        grid=(m // bm,),
        in_specs=[
            pl.BlockSpec((bm, d), lambda i: (i, 0)),
            _mod_spec(nb, d, blocks_per_batch),
            _const_spec(npre.shape), _const_spec(npost.shape),
            pl.BlockSpec((bm, y_ssm.shape[1]), lambda i: (i, 0)),
            pl.BlockSpec((bm, attn.shape[1]), lambda i: (i, 0)),
            _const_spec(glu_w.shape), _const_spec(glu_v.shape), _const_spec(wba.shape),
            _const_spec(w_gates.shape), _const_spec(w_out.shape),
        ],
        out_specs=pl.BlockSpec((bm, d), lambda i: (i, 0)),
        out_shape=jax.ShapeDtypeStruct((m, d), F32),
        compiler_params=_cparams("parallel"),
        name="mix",
    )(x, mod, npre, npost, y_ssm, attn, glu_w, glu_v, wba, w_gates, w_out)


def _pad_cols(w, n):
    return jnp.pad(w, ((0, 0), (0, n - w.shape[1])))


def kernel(x_prompt, x_sample, cache_k, cache_v, cache_idx_k, state_ssm_re, state_ssm_im, page_table,
           c_prompt, c_sample, mod_w, mod_b, norm_pre, norm_post, ffn1_in, ffn1_out, w_in,
           ssm_log_dt, ssm_a_re, ssm_a_im, ssm_b_re, ssm_b_im, ssm_c_re, ssm_c_im, ssm_d,
           glu_w, glu_v, w_branch_attn, w_out, ffn2_in, ffn2_out):
    batch, t_len, d = x_prompt.shape
    dec_batch, dec_seq, _ = x_sample.shape
    depth = mod_w.shape[0]
    d_ssm = ssm_d.shape[1]
    d_att = N_HEADS * HEAD_DIM
    n_groups = d_ssm // SSM_GROUP
    assert dec_seq == 1 and t_len % QB == 0 and d_ssm % LANES == 0
    m = batch * t_len
    bm = 512 if m % 512 == 0 else QB
    topk_p = min(INDEX_TOPK, t_len // 4)
    n_pages = page_table.shape[1]
    n_past = n_pages * PAGE_SIZE
    assert n_pages % SUBLANES == 0
    topk_s = min(INDEX_TOPK, (n_past + dec_seq) // 4)

    mod = _modulation(jnp.concatenate([c_prompt, c_sample], 0), mod_w, mod_b).reshape(depth, -1, 9, d)

    widths = (d_ssm, d_att, d_att, d_att, IDX_HEADS * IDX_DIM, IDX_DIM, IDX_HEADS, d, d)
    off = np.concatenate([[0], np.cumsum(widths)])
    o_u, o_q, o_k, o_v, o_qi, o_ki, o_wi, o_ga, o_gb, o_end = (int(v) for v in off)

    cache_k_t = cache_k.transpose(0, 1, 3, 4, 2)
    cache_v_t = cache_v.transpose(0, 1, 3, 4, 2)
    cache_idx_t = cache_idx_k.transpose(0, 1, 3, 2)

    xp = x_prompt.reshape(m, d)
    xs = x_sample.reshape(dec_batch, d)
    new_p, new_s = [], []
    kv_t = ()
    for l in range(depth):
        bf = lambda w: w[l].astype(BF16)
        modp = mod[l, :batch].transpose(1, 0, 2).reshape(9, batch, 1, d)
        mods = mod[l, batch:].transpose(1, 0, 2).reshape(9, 1, dec_batch, d)
        npre, npost = norm_pre[l], norm_post[l]
        f1_in, f1_out, f2_in, f2_out = bf(ffn1_in), bf(ffn1_out), bf(ffn2_in), bf(ffn2_out)
        wl = w_in[l]
        w_t = wl[:, o_q:o_ga].T.astype(BF16)
        w_gates = wl[:, o_ga:o_end].astype(BF16)
        w_all = _pad_cols(wl, 37 * LANES).astype(BF16)
        g_w, g_v, wba, wo = bf(glu_w), bf(glu_v), bf(w_branch_attn), bf(w_out)
        prep = _ssm_prep(ssm_log_dt[l], ssm_a_re[l], ssm_a_im[l], ssm_b_re[l], ssm_b_im[l],
                         ssm_c_re[l], ssm_c_im[l])
        ops = _ssm_operators(prep, ssm_c_re[l], ssm_c_im[l])

        xp, xs = _ffn(xp, modp, xs, mods, npre, npost, f1_in, f1_out, k=0, bm=bm, blocks_per_batch=t_len // bm)
        u, kh, kidxb, *kv_t, v3, qt, qit, wt = _proj_prompt(
            xp, modp, npre, w_all, w_t, tuple(kv_t), batch=batch, bm=bm, blocks_per_batch=t_len // bm,
            d_ssm=d_ssm, d_att=d_att)
        y_ssm, sp_re, sp_im = _ssm_prompt(u, ops, ssm_d[l], batch=batch, t_len=t_len)
        attn = _attn_prompt(qt, qit, wt, kh, kidxb, v3, batch=batch, t_len=t_len, topk=topk_p)
        xp = _mix(xp, modp, npre, npost, y_ssm, attn, g_w, g_v, wba, w_gates, wo,
                  bm=bm, blocks_per_batch=t_len // bm)
        new_p.append((sp_re, sp_im))

        pr = _proj_sample(xs, mods, npre, w_all)
        u_s, q_s, k_s, v_s = pr[:, o_u:o_q], pr[:, o_q:o_k], pr[:, o_k:o_v], pr[:, o_v:o_qi]
        qi_s, ki_s, wi_s = pr[:, o_qi:o_ki], pr[:, o_ki:o_wi], pr[:, o_wi:o_ga]
        y_ssm_s, ss_re, ss_im = _ssm_sample(u_s, state_ssm_re[l], state_ssm_im[l], ops, ops["c_step"], ssm_d[l])
        hcol = lambda a: a.reshape(dec_batch, N_HEADS, HEAD_DIM, 1)
        sc_s = _idx_sample(page_table, qi_s.reshape(dec_batch, IDX_HEADS, IDX_DIM),
                           wi_s.reshape(dec_batch, IDX_HEADS, 1), ki_s.reshape(dec_batch, 1, IDX_DIM),
                           cache_idx_t, layer=l)
        thr_s, tie_s = _threshold_sample(sc_s.reshape(dec_batch, -1), n_keys=n_past + 1, topk=topk_s)
        attn_s = _attend_sample(page_table, sc_s[:, 0, :n_past].reshape(dec_batch, n_pages, PAGE_SIZE),
                                sc_s[:, :, n_past:n_past + 1], thr_s[:, :1].reshape(dec_batch, 1, 1),
                                tie_s[:, :1].reshape(dec_batch, 1, 1), hcol(q_s), hcol(k_s), hcol(v_s),
                                cache_k_t, cache_v_t, layer=l, topk=topk_s)
        xs = _mix(xs, mods, npre, npost, y_ssm_s, attn_s.reshape(dec_batch, d_att), g_w, g_v, wba, w_gates, wo,
                  bm=dec_batch, blocks_per_batch=1)
        new_s.append((k_s.reshape(dec_batch, 1, N_HEADS, HEAD_DIM), v_s.reshape(dec_batch, 1, N_HEADS, HEAD_DIM),
                      ki_s.reshape(dec_batch, 1, IDX_DIM), ss_re, ss_im))

        xp, xs = _ffn(xp, modp, xs, mods, npre, npost, f2_in, f2_out, k=2, bm=bm, blocks_per_batch=t_len // bm)

    stack = lambda states, i: jnp.stack([s[i] for s in states])
    k_t, v_t, kidx_t = kv_t
    heads_last = lambda a: a.reshape(depth, batch, N_HEADS, HEAD_DIM, t_len).transpose(0, 1, 4, 2, 3)
    return (xp.reshape(batch, t_len, d), xs.reshape(dec_batch, 1, d),
            heads_last(k_t), heads_last(v_t), kidx_t.transpose(0, 1, 3, 2), stack(new_p, 0), stack(new_p, 1),
            stack(new_s, 0), stack(new_s, 1), stack(new_s, 2), stack(new_s, 3), stack(new_s, 4))
```

```python
import functools
import math

import jax
import jax.numpy as jnp
import numpy as np
from jax import lax
from jax.experimental import pallas as pl
from jax.experimental.pallas import tpu as pltpu

F32 = jnp.float32
BF16 = jnp.bfloat16
I32 = jnp.int32

EPS = 1e-6
SSM_GROUP = 16
SSM_N = 64
N_HEADS = 8
HEAD_DIM = 64
IDX_HEADS = 8
IDX_DIM = 64
INDEX_TOPK = 256
PAGE_SIZE = 128
IDX_SCALE = IDX_DIM ** -0.5 * IDX_HEADS ** -0.5
ATT_SCALE = HEAD_DIM ** -0.5

LANES = 128
SUBLANES = 8
VMEM_LIMIT_BYTES = 56 * 1024 * 1024
CHUNK = 8
OCTET = LANES // SSM_GROUP
QB = 256
NEG_BIG = float(np.finfo(np.float32).min)
FLOAT_MID_ITERS = 40
SEARCH_PROBES_PER_TEST = 4
MIN_NORMAL_KEY = 0x00800000


def _cparams(*sem):
    return pltpu.CompilerParams(dimension_semantics=sem, vmem_limit_bytes=VMEM_LIMIT_BYTES)


def _const_spec(shape):
    nd = len(shape)
    return pl.BlockSpec(shape, lambda *_: (0,) * nd, pipeline_mode=pl.Buffered(1))


def _rms(x, g):
    ms = jnp.mean(x * x, axis=-1, keepdims=True)
    return x * lax.rsqrt(ms + EPS) * g


def _dot(a, b):
    return jnp.dot(a, b, preferred_element_type=F32)


def _dot_nt(a, b):
    return lax.dot_general(a, b, (((1,), (1,)), ((), ())), preferred_element_type=F32)


def _mod_kernel(c_ref, w_ref, b_ref, o_ref):
    a = jax.nn.silu(c_ref[...]).astype(BF16)
    o_ref[...] = _dot(a, w_ref[...].astype(BF16)) + b_ref[...]


def _modulation(c_all, mod_w, mod_b):
    depth, d, n = mod_w.shape
    r = c_all.shape[0]
    tn = 1024
    return pl.pallas_call(
        _mod_kernel,
        grid=(depth, n // tn),
        in_specs=[
            pl.BlockSpec((r, d), lambda l, j: (0, 0)),
            pl.BlockSpec((None, d, tn), lambda l, j: (l, 0, j)),
            pl.BlockSpec((None, 1, tn), lambda l, j: (l, 0, j)),
        ],
        out_specs=pl.BlockSpec((None, r, tn), lambda l, j: (l, 0, j)),
        out_shape=jax.ShapeDtypeStruct((depth, r, n), F32),
        compiler_params=_cparams("parallel", "parallel"),
        name="modulation",
    )(c_all, mod_w, mod_b.reshape(depth, 1, n))


def _mod_spec(nb, d, rows_per_batch_block):
    return pl.BlockSpec((9, None, nb, d), lambda i: (0, i // rows_per_batch_block, 0, 0))


def _ffn_kernel(x_ref, mod_ref, xs_ref, mods_ref, npre_ref, npost_ref, win_ref, wout_ref, o_ref, os_ref,
                h_ref, a_ref, *, k, dff, ck):
    def rows(x, mod):
        n = x.shape[0]
        sh, sc, gt = mod[3 * k], mod[3 * k + 1], mod[3 * k + 2]
        h = _rms(x, npre_ref[k:k + 1, :]) * (1.0 + sc) + sh
        h_ref[0:n, :] = h.astype(BF16)
        for c in range(dff // ck):
            hb = h_ref[0:n, :]
            g = _dot(hb, win_ref[:, c * ck:(c + 1) * ck])
            u = _dot(hb, win_ref[:, dff + c * ck:dff + (c + 1) * ck])
            a_ref[0:n, c * ck:(c + 1) * ck] = (jax.nn.silu(g) * u).astype(BF16)
        y = _dot(a_ref[0:n, :], wout_ref[...])
        return x + 0.5 * gt * _rms(y, npost_ref[k:k + 1, :])

    @pl.when(pl.program_id(0) == 0)
    def _():
        os_ref[...] = rows(xs_ref[...], mods_ref)

    o_ref[...] = rows(x_ref[...], mod_ref)


def _ffn(x, mod, xs, mods, npre, npost, w_in, w_out, *, k, bm, blocks_per_batch):
    m, d = x.shape
    ms = xs.shape[0]
    dff = w_out.shape[0]
    kern = functools.partial(_ffn_kernel, k=k, dff=dff, ck=256)
    return pl.pallas_call(
        kern,
        grid=(m // bm,),
        in_specs=[
            pl.BlockSpec((bm, d), lambda i: (i, 0)),
            _mod_spec(1, d, blocks_per_batch),
            _const_spec((ms, d)),
            pl.BlockSpec((9, None, ms, d), lambda i: (0, 0, 0, 0), pipeline_mode=pl.Buffered(1)),
            _const_spec(npre.shape),
            _const_spec(npost.shape),
            _const_spec(w_in.shape),
            _const_spec(w_out.shape),
        ],
        out_specs=[pl.BlockSpec((bm, d), lambda i: (i, 0)), pl.BlockSpec((ms, d), lambda i: (0, 0))],
        out_shape=[jax.ShapeDtypeStruct((m, d), F32), jax.ShapeDtypeStruct((ms, d), F32)],
        scratch_shapes=[pltpu.VMEM((max(bm, ms), d), BF16), pltpu.VMEM((max(bm, ms), dff), BF16)],
        compiler_params=_cparams("arbitrary"),
        name=f"ffn{k}",
    )(x, mod, xs, mods, npre, npost, w_in, w_out)


def _proj_prompt_kernel(x_ref, mod_ref, npre_ref, wrow_ref, wt_ref, *refs, d_ssm, d_att, layer):
    earlier = refs[:3] if layer else ()
    u_ref, kh_ref, kidxb_ref, kt_all, vt_all, kit_all, v3_ref, qt_ref, qit_ref, wt_out_ref = refs[len(earlier):]
    for src, dst in zip(earlier, (kt_all, vt_all, kit_all)):
        dst[0:layer] = src[...]
    kt_ref, vt_ref, kit_ref = kt_all.at[layer], vt_all.at[layer], kit_all.at[layer]
    x = x_ref[...]
    sh, sc = mod_ref[3], mod_ref[4]
    h = (_rms(x, npre_ref[1:2, :]) * (1.0 + sc) + sh).astype(BF16)
    o_k = d_ssm + d_att
    o_ki = d_ssm + 3 * d_att + IDX_HEADS * IDX_DIM
    u_ref[...] = _dot(h, wrow_ref[:, 0:d_ssm])
    k = _dot(h, wrow_ref[:, o_k:o_k + d_att])
    kidxb_ref[...] = _dot(h, wrow_ref[:, o_ki:o_ki + IDX_DIM]).astype(BF16)
    for hh in range(N_HEADS):
        kh_ref[hh] = k[:, hh * HEAD_DIM:(hh + 1) * HEAD_DIM].astype(BF16)
    pt = _dot_nt(wt_ref[...], h)
    o = 0
    qt_ref[...] = (pt[o:o + d_att] * ATT_SCALE).astype(BF16)
    o += d_att
    kt_ref[...] = pt[o:o + d_att]
    o += d_att
    vt = pt[o:o + d_att]
    vt_ref[...] = vt
    vtb = vt.astype(BF16)
    for jj in range(v3_ref.shape[0]):
        v3_ref[jj] = vtb[:, jj * QB:(jj + 1) * QB]
    o += d_att
    qit_ref[...] = pt[o:o + IDX_HEADS * IDX_DIM].astype(BF16)
    o += IDX_HEADS * IDX_DIM
    kit_ref[...] = pt[o:o + IDX_DIM]
    o += IDX_DIM
    wt_out_ref[...] = pt[o:o + IDX_HEADS] * IDX_SCALE


def _proj_prompt(x, mod, npre, w_row, w_t, earlier, *, batch, bm, blocks_per_batch, d_ssm, d_att):
    m, d = x.shape
    t_len = m // batch
    nbt = blocks_per_batch
    layer = earlier[0].shape[0] if earlier else 0
    row = lambda w: pl.BlockSpec((bm, w), lambda i: (i, 0))
    col = lambda r: pl.BlockSpec((r, bm), lambda i: (0, i))
    lcol = lambda n, r: pl.BlockSpec((n, None, r, bm), lambda i: (0, i // nbt, 0, i % nbt))
    bcol = lambda r: lcol(layer + 1, r)
    kern = functools.partial(_proj_prompt_kernel, d_ssm=d_ssm, d_att=d_att, layer=layer)
    return pl.pallas_call(
        kern,
        grid=(m // bm,),
        in_specs=[
            pl.BlockSpec((bm, d), lambda i: (i, 0)),
            _mod_spec(1, d, blocks_per_batch),
            _const_spec(npre.shape),
            _const_spec(w_row.shape),
            _const_spec(w_t.shape),
        ] + [lcol(layer, a.shape[2]) for a in earlier],
        out_specs=[
            row(d_ssm),
            pl.BlockSpec((N_HEADS, bm, HEAD_DIM), lambda i: (0, i, 0)),
            row(IDX_DIM),
            bcol(d_att), bcol(d_att), bcol(IDX_DIM),
            pl.BlockSpec((bm // QB, d_att, QB), lambda i: (i, 0, 0)),
            col(d_att), col(d_att), col(IDX_HEADS),
        ],
        out_shape=[
            jax.ShapeDtypeStruct((m, d_ssm), F32),
            jax.ShapeDtypeStruct((N_HEADS, m, HEAD_DIM), BF16),
            jax.ShapeDtypeStruct((m, IDX_DIM), BF16),
            jax.ShapeDtypeStruct((layer + 1, batch, d_att, t_len), F32),
            jax.ShapeDtypeStruct((layer + 1, batch, d_att, t_len), F32),
            jax.ShapeDtypeStruct((layer + 1, batch, IDX_DIM, t_len), F32),
            jax.ShapeDtypeStruct((m // QB, d_att, QB), BF16),
            jax.ShapeDtypeStruct((d_att, m), BF16),
            jax.ShapeDtypeStruct((d_att, m), BF16),
            jax.ShapeDtypeStruct((IDX_HEADS, m), F32),
        ],
        compiler_params=_cparams("parallel"),
        name="proj_prompt",
    )(x, mod, npre, w_row, w_t, *earlier)


def _proj_sample_kernel(x_ref, mod_ref, npre_ref, w_ref, o_ref):
    x = x_ref[...]
    sh, sc = mod_ref[3], mod_ref[4]
    h = (_rms(x, npre_ref[1:2, :]) * (1.0 + sc) + sh).astype(BF16)
    o_ref[...] = _dot(h, w_ref[...])


def _proj_sample(x, mod, npre, w):
    m, d = x.shape
    n = w.shape[1]
    return pl.pallas_call(
        _proj_sample_kernel,
        grid=(1,),
        in_specs=[
            pl.BlockSpec((m, d), lambda i: (0, 0)),
            _mod_spec(m, d, 1),
            _const_spec(npre.shape),
            _const_spec(w.shape),
        ],
        out_specs=pl.BlockSpec((m, n), lambda i: (0, 0)),
        out_shape=jax.ShapeDtypeStruct((m, n), F32),
        compiler_params=_cparams("arbitrary"),
        name="proj_sample",
    )(x, mod, npre, w)


def _ssm_prep_kernel(ldt_ref, ar_ref, ai_ref, btre_ref, btim_ref, cre_ref, cim_ref,
                     kk_ref, wre_ref, wim_ref, pre_ref, pim_ref, al_ref, ab_ref, bbre_ref, bbim_ref):
    dt = jnp.exp(ldt_ref[...])
    ar, ai = ar_ref[...], ai_ref[...]
    power = []
    for k in range(CHUNK + 1):
        mag = jnp.exp(dt * ar * float(k))
        ph = dt * ai * float(k)
        power.append((mag * jnp.cos(ph), mag * jnp.sin(ph)))

    abr, abi = power[1]
    den = ar * ar + ai * ai
    z_re = ((abr - 1.0) * ar + abi * ai) / den
    z_im = (abi * ar - (abr - 1.0) * ai) / den
    b_re, b_im = btre_ref[...], btim_ref[...]
    bb_re = z_re * b_re - z_im * b_im
    bb_im = z_re * b_im + z_im * b_re
    bbre_ref[...] = bb_re
    bbim_ref[...] = bb_im
    ab_ref[0] = abr
    ab_ref[1] = abi
    al_ref[0], al_ref[1] = power[CHUNK]
    c_re, c_im = cre_ref[...], cim_ref[...]
    bnt = functools.partial(lax.dot_general, dimension_numbers=(((2,), (2,)), ((0,), (0,))),
                            precision=lax.Precision.HIGHEST, preferred_element_type=F32)
    for k in range(CHUNK):
        pr, pi = power[CHUNK - 1 - k]
        wre_ref[k] = pr * bb_re - pi * bb_im
        wim_ref[k] = pr * bb_im + pi * bb_re
        pr, pi = power[k]
        kk_ref[k] = bnt(bb_re, c_re * pr - c_im * pi) - bnt(bb_im, c_re * pi + c_im * pr)
        pr, pi = power[k + 1]
        pre_ref[k] = c_re * pr - c_im * pi
        pim_ref[k] = -(c_re * pi + c_im * pr)


def _ssm_prep(log_dt, a_re, a_im, b_re, b_im, c_re, c_im):
    g, n = a_re.shape
    j = SSM_GROUP
    shp = lambda *s: jax.ShapeDtypeStruct(s, F32)
    swap = lambda a: a.transpose(0, 2, 1)
    return pl.pallas_call(
        _ssm_prep_kernel,
        out_shape=[shp(CHUNK, g, j, j), shp(CHUNK, g, j, n), shp(CHUNK, g, j, n),
                   shp(CHUNK, g, j, n), shp(CHUNK, g, j, n), shp(2, g, 1, n), shp(2, g, 1, n),
                   shp(g, j, n), shp(g, j, n)],
        compiler_params=pltpu.CompilerParams(vmem_limit_bytes=VMEM_LIMIT_BYTES),
        name="ssm_prep",
    )(log_dt.reshape(g, 1, 1), a_re.reshape(g, 1, n), a_im.reshape(g, 1, n), swap(b_re), swap(b_im), c_re, c_im)


def _block_diag(x, groups):
    rows, c = x.shape[-2:]
    keep = (np.arange(rows)[:, None] // (rows // groups)) == (np.arange(groups * c)[None, :] // c)
    return jnp.where(keep, jnp.tile(x, (1,) * (x.ndim - 1) + (groups,)), 0.0)


def _ssm_assemble_kernel(kk_ref, wre_ref, wim_ref, pre_ref, pim_ref, m_ref, w_ref, p_ref):
    j, n = SSM_GROUP, SSM_N

    def block_diag(x, r, c):
        rows, cols = OCTET * r, OCTET * c
        rep = ((lax.broadcasted_iota(I32, (c, cols), 1) & (c - 1)) == lax.broadcasted_iota(I32, (c, cols), 0))
        keep = ((lax.broadcasted_iota(I32, (rows, cols), 0) >> int(math.log2(r)))
                == (lax.broadcasted_iota(I32, (rows, cols), 1) >> int(math.log2(c))))
        return jnp.where(keep, _dot(x.astype(BF16), rep.astype(BF16)), 0.0)

    m_ref[...] = jnp.zeros(m_ref.shape, BF16)
    blk = lambda i, w: slice(i * w, (i + 1) * w)
    for k in range(CHUNK):
        tile = block_diag(kk_ref[k].reshape(OCTET * j, j), j, j).astype(BF16)
        for tau in range(CHUNK - k):
            m_ref[blk(tau, LANES), blk(tau + k, LANES)] = tile
        for part, (w_src, p_src) in enumerate(((wre_ref, pre_ref), (wim_ref, pim_ref))):
            w_ref[blk(k, LANES), blk(part, OCTET * n)] = block_diag(w_src[k].reshape(OCTET * j, n), j, n).astype(BF16)
            p_t = block_diag(p_src[k].reshape(OCTET * j, n), j, n)
            p_ref[blk(part, OCTET * n), blk(k, LANES)] = jnp.transpose(p_t).astype(BF16)


def _ssm_assemble(kk, w_re, w_im, p_re, p_im):
    chunk, g, j, n = w_re.shape
    no, kw, sw = g // OCTET, chunk * LANES, 2 * OCTET * n
    grp = lambda a, b: pl.BlockSpec((chunk, OCTET, a, b), lambda o: (0, o, 0, 0))
    out = lambda a, b: pl.BlockSpec((None, a, b), lambda o: (o, 0, 0))
    return pl.pallas_call(
        _ssm_assemble_kernel,
        grid=(no,),
        in_specs=[grp(j, j), grp(j, n), grp(j, n), grp(j, n), grp(j, n)],
        out_specs=[out(kw, kw), out(kw, sw), out(sw, kw)],
        out_shape=[jax.ShapeDtypeStruct((no, kw, kw), BF16), jax.ShapeDtypeStruct((no, kw, sw), BF16),
                   jax.ShapeDtypeStruct((no, sw, kw), BF16)],
        compiler_params=_cparams("parallel"),
        name="ssm_assemble",
    )(kk, w_re, w_im, p_re, p_im)


def _ssm_operators(prep, c_re, c_im):
    kk, w_re, w_im, p_re, p_im, al, ab, bb_re, bb_im = prep
    g, j, n = bb_re.shape
    no = g // OCTET
    m_op, w_op, p_op = _ssm_assemble(kk, w_re, w_im, p_re, p_im)
    a_chunk = al.reshape(2, no, OCTET * n).transpose(1, 0, 2)
    b_step = _block_diag(jnp.stack([bb_re, bb_im], 0).reshape(2, g * j, n), g)
    b_step = b_step.transpose(1, 0, 2).reshape(g * j, 2 * g * n)
    c_t = jnp.stack([c_re, -c_im], 0).transpose(0, 1, 3, 2).reshape(2, g * n, j)
    c_step = _block_diag(c_t, g).reshape(2 * g * n, g * j).astype(BF16)
    return dict(m=m_op, w=w_op, p=p_op, a_chunk=a_chunk, b_step=b_step, c_step=c_step, abar=ab.reshape(2, g * n))


def _ssm_prompt_kernel(u_ref, m_ref, w_ref, p_ref, al_ref, d_ref, y_ref, sfin_ref,
                       uo_ref, v_ref, sc_ref, *, r, nb):
    half = sc_ref.shape[1] // 2
    rows_all = nb * r
    for tau in range(CHUNK):
        uo_ref[:, tau * LANES:(tau + 1) * LANES] = u_ref[pl.ds(tau, rows_all, stride=CHUNK), :].astype(BF16)
    uo = uo_ref[...]
    v_ref[...] = _dot(uo, w_ref[...])
    a_r, a_i = al_ref[0:1, :], al_ref[1:2, :]

    def step(c, carry):
        out = []
        for s, (s_r, s_i) in enumerate(carry):
            row = pl.ds(s * r + c, 1)
            sc_ref[row, 0:half] = s_r
            sc_ref[row, half:2 * half] = s_i
            v = v_ref[row, :]
            out.append((a_r * s_r - a_i * s_i + v[:, 0:half], a_r * s_i + a_i * s_r + v[:, half:2 * half]))
        return tuple(out)

    zero = jnp.zeros((1, half), F32)
    final = lax.fori_loop(0, r, step, tuple((zero, zero) for _ in range(nb)))
    for s, (s_r, s_i) in enumerate(final):
        sfin_ref[s, 0:1, :] = s_r
        sfin_ref[s, 1:2, :] = s_i
    y = _dot(uo, m_ref[...]) + _dot(sc_ref[...].astype(BF16), p_ref[...])
    d = d_ref[...]
    for t in range(CHUNK):
        rows = pl.ds(t, rows_all, stride=CHUNK)
        y_ref[rows, :] = y[:, t * LANES:(t + 1) * LANES] + d * u_ref[rows, :]


def _ssm_prompt(u, ops, d_skip, *, batch, t_len):
    m, d_ssm = u.shape
    no = d_ssm // LANES
    r = t_len // CHUNK
    kw = CHUNK * LANES
    sw = ops["w"].shape[2]
    nb = math.gcd(batch, 4)
    op_spec = lambda a, b: pl.BlockSpec((None, a, b), lambda o, bb: (o, 0, 0))
    y, sfin = pl.pallas_call(
        functools.partial(_ssm_prompt_kernel, r=r, nb=nb),
        grid=(no, batch // nb),
        in_specs=[
            pl.BlockSpec((nb * t_len, LANES), lambda o, bb: (bb, o)),
            op_spec(kw, kw), op_spec(kw, sw), op_spec(sw, kw),
            pl.BlockSpec((None, 2, sw // 2), lambda o, bb: (o, 0, 0)),
            pl.BlockSpec((1, LANES), lambda o, bb: (0, o)),
        ],
        out_specs=[
            pl.BlockSpec((nb * t_len, LANES), lambda o, bb: (bb, o)),
            pl.BlockSpec((nb, None, 2, sw // 2), lambda o, bb: (bb, o, 0, 0)),
        ],
        out_shape=[jax.ShapeDtypeStruct((m, d_ssm), F32),
                   jax.ShapeDtypeStruct((batch, no, 2, sw // 2), F32)],
        scratch_shapes=[pltpu.VMEM((nb * r, kw), BF16), pltpu.VMEM((nb * r, sw), F32), pltpu.VMEM((nb * r, sw), F32)],
        compiler_params=_cparams("parallel", "parallel"),
        name="ssm_prompt",
    )(u, ops["m"], ops["w"], ops["p"], ops["a_chunk"], d_skip.reshape(1, d_ssm))
    s = sfin.reshape(batch, no, 2, OCTET, SSM_N).transpose(2, 0, 1, 3, 4).reshape(2, batch, no * OCTET, SSM_N)
    return y, s[0], s[1]


def _ssm_sample_kernel(u_ref, h_ref, ab_ref, bstep_ref, cstep_ref, d_ref, y_ref, s_ref):
    u = u_ref[...]
    half = h_ref.shape[2]
    bu = jnp.dot(u, bstep_ref[...], precision=lax.Precision.HIGHEST, preferred_element_type=F32)
    a_r, a_i = ab_ref[0:1, :], ab_ref[1:2, :]
    h_r, h_i = h_ref[0], h_ref[1]
    s_r = a_r * h_r - a_i * h_i + bu[:, 0:half]
    s_i = a_r * h_i + a_i * h_r + bu[:, half:2 * half]
    s_ref[0] = s_r
    s_ref[1] = s_i
    s = jnp.concatenate([s_r, s_i], axis=1).astype(BF16)
    y_ref[...] = _dot(s, cstep_ref[...]) + d_ref[...] * u


def _ssm_sample(u, h_re, h_im, ops, c_step, d_skip):
    bsz, d_ssm = u.shape
    gn = h_re.shape[1] * h_re.shape[2]
    h = jnp.stack([h_re.reshape(bsz, gn), h_im.reshape(bsz, gn)], 0)
    y, s = pl.pallas_call(
        _ssm_sample_kernel,
        out_shape=[jax.ShapeDtypeStruct((bsz, d_ssm), F32), jax.ShapeDtypeStruct((2, bsz, gn), F32)],
        compiler_params=pltpu.CompilerParams(vmem_limit_bytes=VMEM_LIMIT_BYTES),
        name="ssm_sample",
    )(u, h, ops["abar"], ops["b_step"], c_step, d_skip.reshape(1, d_ssm))
    return y, s[0].reshape(h_re.shape), s[1].reshape(h_im.shape)


def _f2key(x):
    b = lax.bitcast_convert_type(x, I32)
    return b ^ ((b >> 31) & 0x7FFFFFFF)


def _key2f(k):
    return lax.bitcast_convert_type(k ^ ((k >> 31) & 0x7FFFFFFF), F32)


def _search_init(mn, mx, n_valid, topk):
    z = jnp.zeros_like(n_valid)
    half = topk + 0.5
    return (_f2key(mn), _f2key(mx) + 1, n_valid.astype(F32) - half, jnp.full(mn.shape, half, F32),
            jnp.full(mn.shape, NEG_BIG, F32), jnp.where(n_valid > topk, 0, 1).astype(I32), z, z)


def _search_probe(state, it, topk):
    lo, hi, f_lo, f_hi = state[:4]
    lo_f, hi_f = _key2f(lo), _key2f(hi)
    frac = jnp.where(it % 4 == 3, 0.5, f_lo / (f_lo + f_hi))
    cand = _f2key(lo_f + (hi_f - lo_f) * frac)
    mid_k = (lo >> 1) + (hi >> 1) + (lo & hi & 1)
    inside = (cand > lo) & (cand < hi) & (it < FLOAT_MID_ITERS)
    probe = jnp.where(inside, cand, mid_k)
    zero_k = jnp.where(it == 0, 0, MIN_NORMAL_KEY)
    return jnp.where((it < 2) & (zero_k > lo) & (zero_k < hi), zero_k, probe)


def _search_update(state, mid, cnt, topk):
    lo, hi, f_lo, f_hi, thr, done, tie, last = state
    hit = cnt == topk
    up, dn = cnt > topk, cnt < topk
    lo_n, hi_n = jnp.where(up, mid, lo), jnp.where(dn, mid, hi)
    adj = ((hi_n == lo_n + 1) | ((lo_n == 0) & (hi_n == MIN_NORMAL_KEY))) & jnp.logical_not(hit)
    fin = hit | adj
    thr_n = jnp.where(hit, _key2f(mid), _key2f(lo_n))
    act = done == 0
    lo = jnp.where(act, lo_n, lo)
    hi = jnp.where(act, hi_n, hi)
    miss = cnt.astype(F32) - (topk + 0.5)
    f_lo = jnp.where(act, jnp.where(up, miss, jnp.where(dn & (last == -1), 0.5 * f_lo, f_lo)), f_lo)
    f_hi = jnp.where(act, jnp.where(dn, -miss, jnp.where(up & (last == 1), 0.5 * f_hi, f_hi)), f_hi)
    last = jnp.where(act, jnp.where(up, 1, jnp.where(dn, -1, last)), last)
    thr = jnp.where(act & fin, thr_n, thr)
    tie = jnp.where(act & adj, 1, tie)
    done = jnp.where(act & fin, 1, done)
    return lo, hi, f_lo, f_hi, thr, done, tie, last


def _search(count_ge, init, topk):
    def pending(state, it):
        return jnp.logical_and(jnp.min(state[5].astype(F32)) == 0.0, it < 96)

    def probe(u, carry):
        it, state = carry
        mid = _search_probe(state, it, topk)
        return it + 1, _search_update(state, mid, count_ge(_key2f(mid)), topk)

    def body(carry):
        it, state = lax.fori_loop(0, SEARCH_PROBES_PER_TEST, probe, carry[1:])
        return pending(state, it), it, state

    _, _, state = lax.while_loop(lambda c: c[0], body, (pending(init, 0), jnp.int32(0), init))
    return state[4], state[6]


def _attn_prompt_kernel(qt_ref, qit_ref, wt_ref, kh_ref, kidx_ref, v3_ref, o_ref, sc_ref, lg_ref, acc_ref, *, topk):
    i = pl.program_id(1)
    nk = i + 1
    qpos = i * QB + lax.broadcasted_iota(I32, (QB, QB), 1)
    krow = lax.broadcasted_iota(I32, (QB, QB), 0)
    fold = lambda x: x.reshape(QB // SUBLANES, SUBLANES, QB)

    tile_rows = lambda j: pl.ds(pl.multiple_of(j * QB, QB), QB)
    rep = lambda x: jnp.broadcast_to(x, (SUBLANES, QB))

    def score_tile(j, carry):
        mn, mx = carry
        kx = kidx_ref[tile_rows(j), :]
        acc = jnp.zeros((QB, QB), F32)
        for h in range(IDX_HEADS):
            s = _dot(kx, qit_ref[h * IDX_DIM:(h + 1) * IDX_DIM, :])
            acc = acc + jnp.maximum(s, 0.0) * wt_ref[h:h + 1, :]
        valid = (j * QB + krow) <= qpos
        sc = jnp.where(valid, acc, -jnp.inf)
        sc_ref[j] = sc
        mx = jnp.maximum(mx, jnp.max(fold(sc), axis=0))
        mn = jnp.minimum(mn, jnp.min(fold(jnp.where(valid, acc, jnp.inf)), axis=0))
        return mn, mx

    mn, mx = lax.fori_loop(0, nk, score_tile,
                           (jnp.full((SUBLANES, QB), jnp.inf, F32), jnp.full((SUBLANES, QB), -jnp.inf, F32)))
    mn = rep(jnp.min(mn, axis=0, keepdims=True))
    mx = rep(jnp.max(mx, axis=0, keepdims=True))

    def count_ge(t):
        t1 = t[0:1, :]

        def body(j, c):
            return c + jnp.sum(fold((sc_ref[j] >= t1).astype(I32)), axis=0)

        c = lax.fori_loop(0, nk, body, jnp.zeros((SUBLANES, QB), I32))
        return rep(jnp.sum(c, axis=0, keepdims=True))

    n_valid = qpos[0:SUBLANES, :] + 1
    thr, tie = _search(count_ge, _search_init(mn, mx, n_valid, topk), topk)

    @pl.when(jnp.max(tie) > 0)
    def _():
        tri = (lax.broadcasted_iota(I32, (QB, QB), 0) >= lax.broadcasted_iota(I32, (QB, QB), 1)).astype(BF16)
        thr1, tie1 = thr[0:1, :], tie[0:1, :] > 0

        def gt_tile(j, c):
            return c + jnp.sum((sc_ref[j] > thr1).astype(I32), axis=0, keepdims=True)

        quota = topk - lax.fori_loop(0, nk, gt_tile, jnp.zeros((1, QB), I32))

        def tie_tile(j, before):
            x = sc_ref[j]
            t = (x == thr1) & tie1
            rank = before + _dot(tri, t.astype(BF16)).astype(I32)
            sc_ref[j] = jnp.where(t & (rank > quota), -jnp.inf, x)
            return before + jnp.sum(t.astype(I32), axis=0, keepdims=True)

        lax.fori_loop(0, nk, tie_tile, jnp.zeros((1, QB), I32))

    thr1 = thr[0:1, :]
    heads = range(N_HEADS)
    hslice = lambda h: slice(h * HEAD_DIM, (h + 1) * HEAD_DIM)

    colmax = lambda x: jnp.max(jnp.max(fold(x), axis=0), axis=0, keepdims=True)
    colsum = lambda x: jnp.sum(jnp.sum(fold(x), axis=0), axis=0, keepdims=True)
    rows_of = lambda rows: jnp.concatenate(rows, axis=0)

    def logits(j, slot, m):
        bias = jnp.where(sc_ref[j] >= thr1, 0.0, -jnp.inf)
        rows = tile_rows(j)
        out = []
        for h in heads:
            lg = _dot(kh_ref[h, rows, :], qt_ref[hslice(h), :]) + bias
            lg_ref[h, slot] = lg
            out.append(jnp.maximum(m[h:h + 1, :], colmax(lg)))
        return rows_of(out)

    def values(j, slot, m_old, m_new, l):
        m_safe = jnp.where(m_new == -jnp.inf, 0.0, m_new)
        alpha = jnp.exp(m_old - m_safe)
        out = []
        for h in heads:
            p = jnp.exp(lg_ref[h, slot] - m_safe[h:h + 1, :])
            acc_ref[h] = acc_ref[h] * alpha[h:h + 1, :] + _dot(v3_ref[j, hslice(h), :], p.astype(BF16))
            out.append(colsum(p))
        return l * alpha + rows_of(out)

    acc_ref[...] = jnp.zeros(acc_ref.shape, F32)
    neg = jnp.full((N_HEADS, QB), -jnp.inf, F32)

    def step(j, slot, carry):
        m_before, m_upto, l = carry
        m_next = logits(j + 1, 1 - slot, m_upto)
        return m_upto, m_next, values(j, slot, m_before, m_upto, l)

    def two_steps(jj, carry):
        return step(2 * jj + 1, 1, step(2 * jj, 0, carry))

    carry = (neg, logits(0, 0, neg), jnp.zeros((N_HEADS, QB), F32))
    carry = lax.fori_loop(0, (nk - 1) // 2, two_steps, carry)
    odd = (nk - 1) % 2 == 1
    carry = lax.cond(odd, lambda c: step(nk - 2, 0, c), lambda c: c, carry)
    l = lax.cond(odd, lambda c: values(nk - 1, 1, *c), lambda c: values(nk - 1, 0, *c), carry)
    outs = [acc_ref[h] / l[h:h + 1, :] for h in heads]
    o_ref[...] = jnp.transpose(jnp.concatenate(outs, axis=0)).astype(BF16)


def _attn_prompt(qt, qit, wt, kh, kidxb, v3, *, batch, t_len, topk):
    d_att, m = qt.shape
    nq = t_len // QB
    return pl.pallas_call(
        functools.partial(_attn_prompt_kernel, topk=topk),
        grid=(batch, nq),
        in_specs=[
            pl.BlockSpec((d_att, QB), lambda b, i: (0, b * nq + i)),
            pl.BlockSpec((d_att, QB), lambda b, i: (0, b * nq + i)),
            pl.BlockSpec((IDX_HEADS, QB), lambda b, i: (0, b * nq + i)),
            pl.BlockSpec((N_HEADS, t_len, HEAD_DIM), lambda b, i: (0, b, 0)),
            pl.BlockSpec((t_len, IDX_DIM), lambda b, i: (b, 0)),
            pl.BlockSpec((nq, d_att, QB), lambda b, i: (b, 0, 0)),
        ],
        out_specs=pl.BlockSpec((QB, d_att), lambda b, i: (b * nq + i, 0)),
        out_shape=jax.ShapeDtypeStruct((m, d_att), BF16),
        scratch_shapes=[pltpu.VMEM((nq, QB, QB), F32), pltpu.VMEM((N_HEADS, 2, QB, QB), F32),
                        pltpu.VMEM((N_HEADS, HEAD_DIM, QB), F32)],
        compiler_params=_cparams("parallel", "arbitrary"),
        name="attn_prompt",
    )(qt, qit, wt, kh, kidxb, v3)


def _idx_sample_kernel(pt_ref, qi_ref, w_ref, kin_ref, ci_ref, o_ref, ibuf, sem, *, layer, n_pages):
    b = pl.program_id(0)
    slot = b % 2

    def page_copy(buf_slot, p, page):
        lanes = pl.ds(pl.multiple_of(p * PAGE_SIZE, PAGE_SIZE), PAGE_SIZE)
        return pltpu.make_async_copy(ci_ref.at[layer, page], ibuf.at[buf_slot, :, lanes], sem.at[buf_slot])

    def start_all(sample, buf_slot):
        def start(p, _):
            page_copy(buf_slot, p, pt_ref[sample, p]).start()
            return 0
        lax.fori_loop(0, n_pages, start, 0, unroll=SUBLANES)

    @pl.when(b == 0)
    def _():
        start_all(b, slot)

    @pl.when(b + 1 < pl.num_programs(0))
    def _():
        start_all(b + 1, 1 - slot)

    def wait(p, _):
        page_copy(slot, p, 0).wait()
        return 0

    lax.fori_loop(0, n_pages, wait, 0, unroll=SUBLANES)

    qi = qi_ref[...].astype(BF16)
    w = w_ref[...] * IDX_SCALE
    n_past = n_pages * PAGE_SIZE
    step = SUBLANES * PAGE_SIZE
    for c in range(n_past // step):
        cols = slice(c * step, (c + 1) * step)
        s = _dot(qi, ibuf[slot, :, cols].astype(BF16))
        o_ref[:, cols] = jnp.sum(jnp.maximum(s, 0.0) * w, axis=0, keepdims=True)
    s_new = jnp.sum(qi.astype(F32) * kin_ref[...].astype(BF16).astype(F32), axis=1, keepdims=True)
    s_new = jnp.sum(jnp.maximum(s_new, 0.0) * w, axis=0, keepdims=True)
    lane = lax.broadcasted_iota(I32, (1, PAGE_SIZE), 1)
    o_ref[:, n_past:n_past + PAGE_SIZE] = jnp.where(lane == 0, s_new, -jnp.inf)


def _idx_sample(page_table, q_idx, w_idx, k_idx_new, cache_idx_t, *, layer):
    bsz, n_pages = page_table.shape
    width = (n_pages + 1) * PAGE_SIZE
    blk = lambda r, c: pl.BlockSpec((None, r, c), lambda b, pt: (b, 0, 0))
    return pl.pallas_call(
        functools.partial(_idx_sample_kernel, layer=layer, n_pages=n_pages),
        grid_spec=pltpu.PrefetchScalarGridSpec(
            num_scalar_prefetch=1,
            grid=(bsz,),
            in_specs=[blk(IDX_HEADS, IDX_DIM), blk(IDX_HEADS, 1), blk(1, IDX_DIM), pl.BlockSpec(memory_space=pl.ANY)],
            out_specs=blk(1, width),
            scratch_shapes=[pltpu.VMEM((2, IDX_DIM, n_pages * PAGE_SIZE), F32), pltpu.SemaphoreType.DMA((2,))],
        ),
        out_shape=jax.ShapeDtypeStruct((bsz, 1, width), F32),
        compiler_params=_cparams("arbitrary"),
        name="idx_sample",
    )(page_table, q_idx, w_idx, k_idx_new, cache_idx_t)


def _threshold_sample_kernel(sc_ref, thr_ref, tie_ref, *, n_keys, topk):
    sc = sc_ref[...]
    mx = jnp.max(sc, axis=1, keepdims=True)
    mn = jnp.min(jnp.where(sc == -jnp.inf, jnp.inf, sc), axis=1, keepdims=True)

    def count_ge(t):
        return jnp.sum((sc_ref[...] >= t).astype(I32), axis=1, keepdims=True)

    n = jnp.full(mn.shape, n_keys, I32)
    thr, tie = _search(count_ge, _search_init(mn, mx, n, topk), topk)
    thr_ref[...] = jnp.broadcast_to(thr, thr_ref.shape)
    tie_ref[...] = jnp.broadcast_to(tie, tie_ref.shape)


def _threshold_sample(scores, *, n_keys, topk):
    bsz = scores.shape[0]
    return pl.pallas_call(
        functools.partial(_threshold_sample_kernel, n_keys=n_keys, topk=topk),
        out_shape=[jax.ShapeDtypeStruct((bsz, LANES), F32), jax.ShapeDtypeStruct((bsz, LANES), I32)],
        compiler_params=pltpu.CompilerParams(vmem_limit_bytes=VMEM_LIMIT_BYTES),
        name="threshold_sample",
    )(scores)


def _attend_sample_kernel(pt_ref, sc_ref, snew_ref, thr_ref, tie_ref, q_ref, kn_ref, vn_ref, ck_ref, cv_ref, o_ref,
                          kbuf, vbuf, qb, lg_ref, sem_k, sem_v, *, layer, n_pages, topk):
    b = pl.program_id(0)

    def copies(src_ref, buf, sem, p, page):
        return pltpu.make_async_copy(src_ref.at[layer, page], buf.at[p], sem)

    streams = ((ck_ref, kbuf, sem_k), (cv_ref, vbuf, sem_v))

    def start_all(src_ref, buf, sem, sample):
        def start(p, _):
            copies(src_ref, buf, sem, p, pt_ref[sample, p]).start()
            return 0
        lax.fori_loop(0, n_pages, start, 0)

    @pl.when(b == 0)
    def _():
        start_all(*streams[0], b)

    start_all(*streams[1], b)

    def wait_all(src_ref, buf, sem):
        def wait(p, _):
            copies(src_ref, buf, sem, p, 0).wait()
            return 0
        lax.fori_loop(0, n_pages, wait, 0)

    for h in range(N_HEADS):
        qb[h] = jnp.broadcast_to(q_ref[h] * ATT_SCALE, (HEAD_DIM, PAGE_SIZE))
    dsum = lambda x: jnp.sum(x, axis=0, keepdims=True)
    sc = sc_ref[...]
    s_new, thr, tie = snew_ref[...], thr_ref[...], tie_ref[...]

    def total(x, op=jnp.sum):
        return op(op(x, axis=1, keepdims=True), axis=0, keepdims=True)

    upper = (lax.broadcasted_iota(I32, (PAGE_SIZE, PAGE_SIZE), 0)
             <= lax.broadcasted_iota(I32, (PAGE_SIZE, PAGE_SIZE), 1)).astype(BF16)
    lower = (lax.broadcasted_iota(I32, (n_pages, n_pages), 1)
             < lax.broadcasted_iota(I32, (n_pages, n_pages), 0)).astype(BF16)

    def flat_rank(mask):
        mb = mask.astype(BF16)
        incl = _dot(mb, upper)
        before = jnp.sum(_dot(lower, mb), axis=1, keepdims=True)
        return (incl + before).astype(I32)

    gt = sc > thr
    tied = tie > 0
    is_tie = (sc == thr) & tied
    quota = jnp.where(tied, topk - total(gt.astype(I32)) - (s_new > thr).astype(I32), topk)
    sel = gt | ((sc == thr) & (flat_rank(is_tie) <= quota))
    n_tie_past = total(is_tie.astype(I32))
    new_sel = (s_new > thr) | ((s_new == thr) & (n_tie_past < quota))

    wait_all(*streams[0])
    for h in range(N_HEADS):
        q_h = qb[h]

        def k_page(p, _, h=h, q_h=q_h):
            lg_ref[h, pl.ds(p, 1), :] = dsum(kbuf[p, h] * q_h)
            return 0

        lax.fori_loop(0, n_pages, k_page, 0, unroll=4)

    @pl.when(b + 1 < pl.num_programs(0))
    def _():
        start_all(*streams[0], b + 1)

    red = lambda x, op: op(x, axis=(1, 2), keepdims=True)
    lg = jnp.where(sel[None], lg_ref[...], -jnp.inf)
    lg_new = jnp.sum(q_ref[...] * ATT_SCALE * kn_ref[...], axis=1, keepdims=True)
    lg_new = jnp.where(new_sel[None], lg_new, -jnp.inf)
    m = jnp.maximum(red(lg, jnp.max), lg_new)
    p = jnp.exp(lg - m)
    e_new = jnp.exp(lg_new - m)
    denom = red(p, jnp.sum) + e_new
    lg_ref[...] = p / denom
    p_new = e_new / denom

    wait_all(*streams[1])
    for h in range(N_HEADS):
        def v_page(p, acc, h=h):
            return acc + vbuf[p, h] * lg_ref[h, pl.ds(p, 1), :]

        acc = lax.fori_loop(0, n_pages, v_page, jnp.zeros((HEAD_DIM, PAGE_SIZE), F32), unroll=4)
        o_ref[h] = jnp.sum(acc, axis=1, keepdims=True) + p_new[h] * vn_ref[h]


def _attend_sample(page_table, scores, s_new, thr, tie, q, k_new, v_new, cache_k_t, cache_v_t, *, layer, topk):
    bsz, n_pages = page_table.shape
    one = pl.BlockSpec((None, 1, 1), lambda b, pt: (b, 0, 0))
    hcol = pl.BlockSpec((None, N_HEADS, HEAD_DIM, 1), lambda b, pt: (b, 0, 0, 0))
    any_spec = pl.BlockSpec(memory_space=pl.ANY)
    return pl.pallas_call(
        functools.partial(_attend_sample_kernel, layer=layer, n_pages=n_pages, topk=topk),
        grid_spec=pltpu.PrefetchScalarGridSpec(
            num_scalar_prefetch=1,
            grid=(bsz,),
            in_specs=[pl.BlockSpec((None, n_pages, PAGE_SIZE), lambda b, pt: (b, 0, 0)), one, one, one,
                      hcol, hcol, hcol, any_spec, any_spec],
            out_specs=hcol,
            scratch_shapes=[pltpu.VMEM((n_pages, N_HEADS, HEAD_DIM, PAGE_SIZE), F32),
                            pltpu.VMEM((n_pages, N_HEADS, HEAD_DIM, PAGE_SIZE), F32),
                            pltpu.VMEM((N_HEADS, HEAD_DIM, PAGE_SIZE), F32),
                            pltpu.VMEM((N_HEADS, n_pages, PAGE_SIZE), F32),
                            pltpu.SemaphoreType.DMA(()), pltpu.SemaphoreType.DMA(())],
        ),
        out_shape=jax.ShapeDtypeStruct((bsz, N_HEADS, HEAD_DIM, 1), F32),
        compiler_params=_cparams("arbitrary"),
        name="attend_sample",
    )(page_table, scores, s_new, thr, tie, q, k_new, v_new, cache_k_t, cache_v_t)


def _mix_kernel(x_ref, mod_ref, ys_ref, at_ref, xs_ref, mods_ref, yss_ref, ats_ref, npre_ref, npost_ref,
                gw_ref, gv_ref, wba_ref, wg_ref, wo_ref, o_ref, os_ref):
    def rows(x, mod, y_ssm, attn):
        d = x.shape[1]
        sh, sc, gt = mod[3], mod[4], mod[5]
        h = (_rms(x, npre_ref[1:2, :]) * (1.0 + sc) + sh).astype(BF16)
        gates = _dot(h, wg_ref[...])
        ys = y_ssm.astype(BF16)
        y_a = _dot(ys, gw_ref[...]) * jax.nn.sigmoid(_dot(ys, gv_ref[...]))
        y_b = _dot(attn.astype(BF16), wba_ref[...])
        mixed = jax.nn.sigmoid(gates[:, 0:d]) * y_a + jax.nn.sigmoid(gates[:, d:2 * d]) * y_b
        y = _dot(mixed.astype(BF16), wo_ref[...])
        return x + gt * _rms(y, npost_ref[1:2, :])

    @pl.when(pl.program_id(0) == 0)
    def _():
        os_ref[...] = rows(xs_ref[...], mods_ref, yss_ref[...], ats_ref[...])

    o_ref[...] = rows(x_ref[...], mod_ref, ys_ref[...], at_ref[...])


def _mix(x, mod, y_ssm, attn, xs, mods, y_ssm_s, attn_s, npre, npost, glu_w, glu_v, wba, w_gates, w_out,
         *, bm, blocks_per_batch):
    m, d = x.shape
    ms = xs.shape[0]
    return pl.pallas_call(
        _mix_kernel,
        grid=(m // bm,),
        in_specs=[
            pl.BlockSpec((bm, d), lambda i: (i, 0)),
            _mod_spec(1, d, blocks_per_batch),
            pl.BlockSpec((bm, y_ssm.shape[1]), lambda i: (i, 0)),
            pl.BlockSpec((bm, attn.shape[1]), lambda i: (i, 0)),
            _const_spec((ms, d)),
            pl.BlockSpec((9, None, ms, d), lambda i: (0, 0, 0, 0), pipeline_mode=pl.Buffered(1)),
            _const_spec(y_ssm_s.shape), _const_spec(attn_s.shape),
            _const_spec(npre.shape), _const_spec(npost.shape),
            _const_spec(glu_w.shape), _const_spec(glu_v.shape), _const_spec(wba.shape),
            _const_spec(w_gates.shape), _const_spec(w_out.shape),
        ],
        out_specs=[pl.BlockSpec((bm, d), lambda i: (i, 0)), pl.BlockSpec((ms, d), lambda i: (0, 0))],
        out_shape=[jax.ShapeDtypeStruct((m, d), F32), jax.ShapeDtypeStruct((ms, d), F32)],
        compiler_params=_cparams("arbitrary"),
        name="mix",
    )(x, mod, y_ssm, attn, xs, mods, y_ssm_s, attn_s, npre, npost, glu_w, glu_v, wba, w_gates, w_out)


def _pad_cols(w, n):
    return jnp.pad(w, ((0, 0), (0, n - w.shape[1])))


def kernel(x_prompt, x_sample, cache_k, cache_v, cache_idx_k, state_ssm_re, state_ssm_im, page_table,
           c_prompt, c_sample, mod_w, mod_b, norm_pre, norm_post, ffn1_in, ffn1_out, w_in,
           ssm_log_dt, ssm_a_re, ssm_a_im, ssm_b_re, ssm_b_im, ssm_c_re, ssm_c_im, ssm_d,
           glu_w, glu_v, w_branch_attn, w_out, ffn2_in, ffn2_out):
    batch, t_len, d = x_prompt.shape
    dec_batch, dec_seq, _ = x_sample.shape
    depth = mod_w.shape[0]
    d_ssm = ssm_d.shape[1]
    d_att = N_HEADS * HEAD_DIM
    n_groups = d_ssm // SSM_GROUP
    assert dec_seq == 1 and t_len % QB == 0 and d_ssm % LANES == 0
    m = batch * t_len
    bm = 512 if m % 512 == 0 else QB
    topk_p = min(INDEX_TOPK, t_len // 4)
    n_pages = page_table.shape[1]
    n_past = n_pages * PAGE_SIZE
    assert n_pages % SUBLANES == 0
    topk_s = min(INDEX_TOPK, (n_past + dec_seq) // 4)

    mod = _modulation(jnp.concatenate([c_prompt, c_sample], 0), mod_w, mod_b).reshape(depth, -1, 9, d)

    widths = (d_ssm, d_att, d_att, d_att, IDX_HEADS * IDX_DIM, IDX_DIM, IDX_HEADS, d, d)
    off = np.concatenate([[0], np.cumsum(widths)])
    o_u, o_q, o_k, o_v, o_qi, o_ki, o_wi, o_ga, o_gb, o_end = (int(v) for v in off)

    cache_k_t = cache_k.transpose(0, 1, 3, 4, 2)
    cache_v_t = cache_v.transpose(0, 1, 3, 4, 2)
    cache_idx_t = cache_idx_k.transpose(0, 1, 3, 2)

    xp = x_prompt.reshape(m, d)
    xs = x_sample.reshape(dec_batch, d)
    new_p, new_s = [], []
    kv_t = ()
    for l in range(depth):
        bf = lambda w: w[l].astype(BF16)
        modp = mod[l, :batch].transpose(1, 0, 2).reshape(9, batch, 1, d)
        mods = mod[l, batch:].transpose(1, 0, 2).reshape(9, 1, dec_batch, d)
        npre, npost = norm_pre[l], norm_post[l]
        f1_in, f1_out, f2_in, f2_out = bf(ffn1_in), bf(ffn1_out), bf(ffn2_in), bf(ffn2_out)
        wl = w_in[l]
        w_t = wl[:, o_q:o_ga].T.astype(BF16)
        w_gates = wl[:, o_ga:o_end].astype(BF16)
        w_all = _pad_cols(wl, 37 * LANES).astype(BF16)
        g_w, g_v, wba, wo = bf(glu_w), bf(glu_v), bf(w_branch_attn), bf(w_out)
        prep = _ssm_prep(ssm_log_dt[l], ssm_a_re[l], ssm_a_im[l], ssm_b_re[l], ssm_b_im[l],
                         ssm_c_re[l], ssm_c_im[l])
        ops = _ssm_operators(prep, ssm_c_re[l], ssm_c_im[l])

        xp, xs = _ffn(xp, modp, xs, mods, npre, npost, f1_in, f1_out, k=0, bm=bm, blocks_per_batch=t_len // bm)
        u, kh, kidxb, *kv_t, v3, qt, qit, wt = _proj_prompt(
            xp, modp, npre, w_all, w_t, tuple(kv_t), batch=batch, bm=bm, blocks_per_batch=t_len // bm,
            d_ssm=d_ssm, d_att=d_att)
        y_ssm, sp_re, sp_im = _ssm_prompt(u, ops, ssm_d[l], batch=batch, t_len=t_len)
        attn = _attn_prompt(qt, qit, wt, kh, kidxb, v3, batch=batch, t_len=t_len, topk=topk_p)
        new_p.append((sp_re, sp_im))

        pr = _proj_sample(xs, mods, npre, w_all)
        u_s, q_s, k_s, v_s = pr[:, o_u:o_q], pr[:, o_q:o_k], pr[:, o_k:o_v], pr[:, o_v:o_qi]
        qi_s, ki_s, wi_s = pr[:, o_qi:o_ki], pr[:, o_ki:o_wi], pr[:, o_wi:o_ga]
        y_ssm_s, ss_re, ss_im = _ssm_sample(u_s, state_ssm_re[l], state_ssm_im[l], ops, ops["c_step"], ssm_d[l])
        hcol = lambda a: a.reshape(dec_batch, N_HEADS, HEAD_DIM, 1)
        sc_s = _idx_sample(page_table, qi_s.reshape(dec_batch, IDX_HEADS, IDX_DIM),
                           wi_s.reshape(dec_batch, IDX_HEADS, 1), ki_s.reshape(dec_batch, 1, IDX_DIM),
                           cache_idx_t, layer=l)
        thr_s, tie_s = _threshold_sample(sc_s.reshape(dec_batch, -1), n_keys=n_past + 1, topk=topk_s)
        attn_s = _attend_sample(page_table, sc_s[:, 0, :n_past].reshape(dec_batch, n_pages, PAGE_SIZE),
                                sc_s[:, :, n_past:n_past + 1], thr_s[:, :1].reshape(dec_batch, 1, 1),
                                tie_s[:, :1].reshape(dec_batch, 1, 1), hcol(q_s), hcol(k_s), hcol(v_s),
                                cache_k_t, cache_v_t, layer=l, topk=topk_s)
        xp, xs = _mix(xp, modp, y_ssm, attn, xs, mods, y_ssm_s, attn_s.reshape(dec_batch, d_att),
                      npre, npost, g_w, g_v, wba, w_gates, wo, bm=bm, blocks_per_batch=t_len // bm)
        new_s.append((k_s.reshape(dec_batch, 1, N_HEADS, HEAD_DIM), v_s.reshape(dec_batch, 1, N_HEADS, HEAD_DIM),
                      ki_s.reshape(dec_batch, 1, IDX_DIM), ss_re, ss_im))

        xp, xs = _ffn(xp, modp, xs, mods, npre, npost, f2_in, f2_out, k=2, bm=bm, blocks_per_batch=t_len // bm)

    stack = lambda states, i: jnp.stack([s[i] for s in states])
    k_t, v_t, kidx_t = kv_t
    heads_last = lambda a: a.reshape(depth, batch, N_HEADS, HEAD_DIM, t_len).transpose(0, 1, 4, 2, 3)
    return (xp.reshape(batch, t_len, d), xs.reshape(dec_batch, 1, d),
            heads_last(k_t), heads_last(v_t), kidx_t.transpose(0, 1, 3, 2), stack(new_p, 0), stack(new_p, 1),
            stack(new_s, 0), stack(new_s, 1), stack(new_s, 2), stack(new_s, 3), stack(new_s, 4))
```
